```python
import math
import jax, jax.numpy as jnp
from jax import lax
import numpy as np

D_MODEL = 1024
BATCH = 16
SEQ = 2048
DEPTH = 4

N_MIXERS = 3
BLOCK = 128
ROPE_THETA = 10000.0
NORM_EPS = 1e-5
D_FF = 4 * D_MODEL

A_HEAD_DIM = 64
A_N_HEADS = D_MODEL // A_HEAD_DIM
A_N_KV_HEADS = A_N_HEADS // 8
A_WINDOW = 128
A_Q_DIM = A_N_HEADS * A_HEAD_DIM
A_KV_DIM = A_N_KV_HEADS * A_HEAD_DIM

SSM_D_INNER = 2 * D_MODEL
SSM_HEAD_DIM = 64
SSM_N_HEADS = SSM_D_INNER // SSM_HEAD_DIM
SSM_N_GROUPS = 8
SSM_HEADS_PER_GROUP = SSM_N_HEADS // SSM_N_GROUPS
SSM_D_STATE = 128
SSM_CONV = 4
SSM_CHUNK = 128
SSM_BC_DIM = SSM_N_GROUPS * SSM_D_STATE
SSM_CONV_DIM = SSM_D_INNER + 2 * SSM_BC_DIM
SSM_IN_DIM = SSM_D_INNER + SSM_CONV_DIM + SSM_N_HEADS

C_HEAD_DIM = 64
C_HEADS_PER_GROUP = D_MODEL // C_HEAD_DIM
C_PATTERNS = ((128, 1), (512, 4), (2048, 16))
C_N_GROUPS = len(C_PATTERNS)
C_QKV_DIM = 3 * C_N_GROUPS * C_HEADS_PER_GROUP * C_HEAD_DIM

N_A = (DEPTH + 2) // 3
N_B = (DEPTH + 1) // 3
N_C = DEPTH // 3

kernel_name = "hybrid_swa_mamba2_dilated_trunk"


def rmsnorm(x, w):
    xf = x.astype(jnp.float32)
    y = xf * lax.rsqrt(jnp.mean(xf * xf, axis=-1, keepdims=True) + NORM_EPS)
    return (y * w.astype(jnp.float32)).astype(x.dtype)


def rope(t, positions):
    half = t.shape[-1] // 2
    inv = ROPE_THETA ** (-jnp.arange(half, dtype=jnp.float32) / half)
    ang = positions.astype(jnp.float32)[..., None] * inv
    cos = jnp.cos(ang)[:, :, None, :]
    sin = jnp.sin(ang)[:, :, None, :]
    tf = t.astype(jnp.float32)
    t1, t2 = tf[..., :half], tf[..., half:]
    return jnp.concatenate([t1 * cos - t2 * sin, t2 * cos + t1 * sin], axis=-1).astype(t.dtype)


def banded_attention(q, k, v, max_dist, sinks=None):
    n, L, H, Dh = q.shape
    Hkv = k.shape[2]
    R = H // Hkv
    nb = -(-L // BLOCK)
    pad = nb * BLOCK - L
    padw = ((0, 0), (0, pad), (0, 0), (0, 0))
    qb = jnp.pad(q, padw).reshape(n, nb, BLOCK, Hkv, R, Dh)
    kb = jnp.pad(k, padw).reshape(n, nb, BLOCK, Hkv, Dh)
    vb = jnp.pad(v, padw).reshape(n, nb, BLOCK, Hkv, Dh)

    def with_prev(t):
        prev = jnp.pad(t, ((0, 0), (1, 0), (0, 0), (0, 0), (0, 0)))[:, :-1]
        return jnp.concatenate([prev, t], axis=2)

    kk, vv = with_prev(kb), with_prev(vb)
    s = jnp.einsum('nbqgrd,nbkgd->nbgrqk', qb, kk,
                   preferred_element_type=jnp.float32) * (Dh ** -0.5)
    blk = jnp.arange(nb)[:, None, None] * BLOCK
    qpos = blk + jnp.arange(BLOCK)[None, :, None]
    kpos = blk - BLOCK + jnp.arange(2 * BLOCK)[None, None, :]
    dist = qpos - kpos
    mask = (dist >= 0) & (dist <= max_dist) & (kpos >= 0)
    s = jnp.where(mask[None, :, None, None], s, -jnp.inf)
    m = jnp.max(s, axis=-1)
    if sinks is not None:
        sk = sinks.astype(jnp.float32).reshape(Hkv, R)[None, None, :, :, None]
        m = jnp.maximum(m, sk)
    p = jnp.exp(s - m[..., None])
    denom = jnp.sum(p, axis=-1)
    if sinks is not None:
        denom = denom + jnp.exp(sk - m)
    lse = m + jnp.log(denom)
    o = jnp.einsum('nbgrqk,nbkgd->nbqgrd', (p / denom[..., None]).astype(v.dtype), vv)
    o = o.reshape(n, nb * BLOCK, H, Dh)[:, :L]
    lse = jnp.transpose(lse, (0, 1, 4, 2, 3)).reshape(n, nb * BLOCK, H)[:, :L]
    return o, lse


def swa_sink_mixer(h, positions, w_qkv, b_qkv, sinks, w_o, b_o):
    B, S, _ = h.shape
    qkv = h @ w_qkv + b_qkv
    q = qkv[..., :A_Q_DIM].reshape(B, S, A_N_HEADS, A_HEAD_DIM)
    k = qkv[..., A_Q_DIM:A_Q_DIM + A_KV_DIM].reshape(B, S, A_N_KV_HEADS, A_HEAD_DIM)
    v = qkv[..., A_Q_DIM + A_KV_DIM:].reshape(B, S, A_N_KV_HEADS, A_HEAD_DIM)
    q, k = rope(q, positions), rope(k, positions)
    o, _ = banded_attention(q, k, v, A_WINDOW - 1, sinks)
    return o.reshape(B, S, A_Q_DIM) @ w_o + b_o


def dilated_attention(q, k, v, window, dilation):
    B, S, H, Dh = q.shape
    Ls = S // dilation

    def to_sub(t):
        return t.reshape(B, Ls, dilation, H, Dh).transpose(0, 2, 1, 3, 4).reshape(B * dilation, Ls, H, Dh)

    o, lse = banded_attention(to_sub(q), to_sub(k), to_sub(v), window // dilation)
    o = o.reshape(B, dilation, Ls, H, Dh).transpose(0, 2, 1, 3, 4).reshape(B, S, H, Dh)
    lse = lse.reshape(B, dilation, Ls, H).transpose(0, 2, 1, 3).reshape(B, S, H)
    return o, lse


def dilated_mixer(h, positions, w_qkv, w_o):
    B, S, _ = h.shape
    GH = C_N_GROUPS * C_HEADS_PER_GROUP
    qkv = (h @ w_qkv).reshape(B, S, 3, GH, C_HEAD_DIM)
    q = rope(qkv[:, :, 0], positions).reshape(B, S, C_N_GROUPS, C_HEADS_PER_GROUP, C_HEAD_DIM)
    k = rope(qkv[:, :, 1], positions).reshape(B, S, C_N_GROUPS, C_HEADS_PER_GROUP, C_HEAD_DIM)
    v = qkv[:, :, 2].reshape(B, S, C_N_GROUPS, C_HEADS_PER_GROUP, C_HEAD_DIM)
    outs, lses = [], []
    for g, (window, dilation) in enumerate(C_PATTERNS):
        o, lse = dilated_attention(q[:, :, g], k[:, :, g], v[:, :, g], window, dilation)
        outs.append(o.astype(jnp.float32))
        lses.append(lse)
    wts = jax.nn.softmax(jnp.stack(lses, axis=0), axis=0)
    o = jnp.sum(wts[..., None] * jnp.stack(outs, axis=0), axis=0).astype(h.dtype)
    return o.reshape(B, S, C_HEADS_PER_GROUP * C_HEAD_DIM) @ w_o


def causal_depthwise_conv(x, w, b):
    C = x.shape[-1]
    y = lax.conv_general_dilated(x, w[:, None, :].astype(x.dtype), window_strides=(1,),
                                 padding=((SSM_CONV - 1, 0),),
                                 dimension_numbers=('NWC', 'WIO', 'NWC'),
                                 feature_group_count=C)
    return y + b


def ssd_chunked(xh, dt, A, Bm, Cm):
    b, S, H, P = xh.shape
    G, N, HG, Q = SSM_N_GROUPS, SSM_D_STATE, SSM_HEADS_PER_GROUP, SSM_CHUNK
    nc = S // Q
    x = (xh * dt[..., None]).reshape(b, nc, Q, G, HG, P)
    dA = (dt * A).reshape(b, nc, Q, G, HG).transpose(0, 1, 3, 4, 2)
    cs = jnp.cumsum(dA, axis=-1)
    Bc = Bm.reshape(b, nc, Q, G, N)
    Cc = Cm.reshape(b, nc, Q, G, N)
    tril = jnp.arange(Q)[:, None] >= jnp.arange(Q)[None, :]
    Lmat = jnp.exp(jnp.where(tril, cs[..., :, None] - cs[..., None, :], -jnp.inf))
    CB = jnp.einsum('bclgn,bcsgn->bcgls', Cc, Bc)
    y_diag = jnp.einsum('bcghls,bcsghp->bclghp', CB[:, :, :, None] * Lmat, x)
    decay_s = jnp.exp(cs[..., -1:] - cs)
    states = jnp.einsum('bclgn,bcghl,bclghp->bcghpn', Bc, decay_s, x)
    chunk_decay = jnp.exp(cs[..., -1])

    def step(state, inp):
        st, dec = inp
        return state * dec[..., None, None] + st, state

    init = jnp.zeros((b, G, HG, P, N), jnp.float32)
    _, prev = lax.scan(step, init, (jnp.moveaxis(states, 1, 0), jnp.moveaxis(chunk_decay, 1, 0)))
    prev = jnp.moveaxis(prev, 0, 1)
    y_off = jnp.einsum('bclgn,bcghpn,bcghl->bclghp', Cc, prev, jnp.exp(cs))
    return (y_diag + y_off).reshape(b, S, H, P)


def mamba2_mixer(h, in_w, conv_w, conv_b, dt_bias, a_log, d_skip, norm_w, out_w):
    B, S, _ = h.shape
    zxbcdt = h @ in_w
    z = zxbcdt[..., :SSM_D_INNER]
    xbc = zxbcdt[..., SSM_D_INNER:SSM_D_INNER + SSM_CONV_DIM]
    dt = zxbcdt[..., SSM_D_INNER + SSM_CONV_DIM:]
    xbc = jax.nn.silu(causal_depthwise_conv(xbc, conv_w, conv_b)).astype(jnp.float32)
    xs = xbc[..., :SSM_D_INNER].reshape(B, S, SSM_N_HEADS, SSM_HEAD_DIM)
    Bm = xbc[..., SSM_D_INNER:SSM_D_INNER + SSM_BC_DIM].reshape(B, S, SSM_N_GROUPS, SSM_D_STATE)
    Cm = xbc[..., SSM_D_INNER + SSM_BC_DIM:].reshape(B, S, SSM_N_GROUPS, SSM_D_STATE)
    dt = jax.nn.softplus(dt.astype(jnp.float32) + dt_bias.astype(jnp.float32))
    A = -jnp.exp(a_log.astype(jnp.float32))
    y = ssd_chunked(xs, dt, A, Bm, Cm) + d_skip.astype(jnp.float32)[:, None] * xs
    g = (y.reshape(B, S, SSM_D_INNER) * jax.nn.silu(z.astype(jnp.float32)))
    g = g.reshape(B, S, SSM_N_GROUPS, SSM_D_INNER // SSM_N_GROUPS)
    g = g * lax.rsqrt(jnp.mean(g * g, axis=-1, keepdims=True) + NORM_EPS)
    g = g.reshape(B, S, SSM_D_INNER) * norm_w.astype(jnp.float32)
    return g.astype(h.dtype) @ out_w


def sqrelu_mlp(h, w_up, w_down):
    u = jax.nn.relu(h @ w_up)
    return (u * u) @ w_down


def setup_inputs(seed: int = 0) -> dict:
    key = jax.random.key(seed)
    ks = jax.random.split(key, 24)
    f32 = jnp.float32
    res_scale = (2.0 * DEPTH) ** -0.5

    def nrm(k, shape, scale):
        return jax.random.normal(k, shape, f32) * scale

    x = jax.random.normal(ks[0], (BATCH, SEQ, D_MODEL), f32)
    offs = jax.random.randint(ks[1], (BATCH, 1), 0, 4096, dtype=jnp.int32)
    positions = offs + jnp.arange(SEQ, dtype=jnp.int32)[None, :]
    norm_mix_w = 1.0 + nrm(ks[2], (DEPTH, D_MODEL), 0.02)
    norm_mlp_w = 1.0 + nrm(ks[3], (DEPTH, D_MODEL), 0.02)
    a_w_qkv = nrm(ks[4], (N_A, D_MODEL, A_Q_DIM + 2 * A_KV_DIM), D_MODEL ** -0.5)
    a_b_qkv = nrm(ks[5], (N_A, A_Q_DIM + 2 * A_KV_DIM), 0.02)
    a_sinks = nrm(ks[6], (N_A, A_N_HEADS), 0.5)
    a_w_o = nrm(ks[7], (N_A, A_Q_DIM, D_MODEL), A_Q_DIM ** -0.5 * res_scale)
    a_b_o = nrm(ks[8], (N_A, D_MODEL), 0.02)
    b_in_w = nrm(ks[9], (N_B, D_MODEL, SSM_IN_DIM), D_MODEL ** -0.5)
    b_conv_w = nrm(ks[10], (N_B, SSM_CONV, SSM_CONV_DIM), SSM_CONV ** -0.5)
    b_conv_b = nrm(ks[11], (N_B, SSM_CONV_DIM), 0.02)
    dt0 = jnp.exp(jax.random.uniform(ks[12], (N_B, SSM_N_HEADS), f32, math.log(1e-3), math.log(1e-1)))
    b_dt_bias = dt0 + jnp.log(-jnp.expm1(-dt0))
    b_a_log = jnp.log(jax.random.uniform(ks[13], (N_B, SSM_N_HEADS), f32, 1.0, 16.0))
    b_d = 1.0 + nrm(ks[14], (N_B, SSM_N_HEADS), 0.02)
    b_norm_w = 1.0 + nrm(ks[15], (N_B, SSM_D_INNER), 0.02)
    b_out_w = nrm(ks[16], (N_B, SSM_D_INNER, D_MODEL), SSM_D_INNER ** -0.5 * res_scale)
    c_w_qkv = nrm(ks[17], (N_C, D_MODEL, C_QKV_DIM), D_MODEL ** -0.5)
    c_w_o = nrm(ks[18], (N_C, C_HEADS_PER_GROUP * C_HEAD_DIM, D_MODEL),
                (C_HEADS_PER_GROUP * C_HEAD_DIM) ** -0.5 * res_scale)
    mlp_w_up = nrm(ks[19], (DEPTH, D_MODEL, D_FF), D_MODEL ** -0.5)
    mlp_w_down = nrm(ks[20], (DEPTH, D_FF, D_MODEL), D_FF ** -0.5 * res_scale)
    final_norm_w = 1.0 + nrm(ks[21], (D_MODEL,), 0.02)
    return {"x": x, "positions": positions, "norm_mix_w": norm_mix_w, "norm_mlp_w": norm_mlp_w,
            "a_w_qkv": a_w_qkv, "a_b_qkv": a_b_qkv, "a_sinks": a_sinks, "a_w_o": a_w_o, "a_b_o": a_b_o,
            "b_in_w": b_in_w, "b_conv_w": b_conv_w, "b_conv_b": b_conv_b, "b_dt_bias": b_dt_bias,
            "b_a_log": b_a_log, "b_d": b_d, "b_norm_w": b_norm_w, "b_out_w": b_out_w,
            "c_w_qkv": c_w_qkv, "c_w_o": c_w_o, "mlp_w_up": mlp_w_up, "mlp_w_down": mlp_w_down,
            "final_norm_w": final_norm_w}


def reference(x, positions, norm_mix_w, norm_mlp_w, a_w_qkv, a_b_qkv, a_sinks, a_w_o, a_b_o,
              b_in_w, b_conv_w, b_conv_b, b_dt_bias, b_a_log, b_d, b_norm_w, b_out_w,
              c_w_qkv, c_w_o, mlp_w_up, mlp_w_down, final_norm_w):
    h = x
    for i in range(DEPTH):
        kind, j = i % N_MIXERS, i // N_MIXERS
        u = rmsnorm(h, norm_mix_w[i])
        if kind == 0:
            mix = swa_sink_mixer(u, positions, a_w_qkv[j], a_b_qkv[j], a_sinks[j], a_w_o[j], a_b_o[j])
        elif kind == 1:
            mix = mamba2_mixer(u, b_in_w[j], b_conv_w[j], b_conv_b[j], b_dt_bias[j], b_a_log[j],
                               b_d[j], b_norm_w[j], b_out_w[j])
        else:
            mix = dilated_mixer(u, positions, c_w_qkv[j], c_w_o[j])
        h = h + mix
        u = rmsnorm(h, norm_mlp_w[i])
        h = h + sqrelu_mlp(u, mlp_w_up[i], mlp_w_down[i])
    return rmsnorm(h, final_norm_w)
```

```python
import functools
import math

import jax
import jax.numpy as jnp
from jax import lax
from jax.experimental import pallas as pl
from jax.experimental.pallas import tpu as pltpu

F32 = jnp.float32
BF16 = jnp.bfloat16

N_MIXERS = 3
ATTN_BLOCK = 128
HEAD_DIM = 64
ROPE_THETA = 10000.0
NORM_EPS = 1e-5
A_Q_PER_KV = 8
A_WINDOW = 128
C_PATTERNS = ((128, 1), (512, 4), (2048, 16))
SSM_HEAD_DIM = 64
SSM_N_GROUPS = 8
SSM_D_STATE = 128
SSM_CONV = 4
SSM_CHUNK = 128

LANES = 128
SUBLANES = 8
V7X_VMEM_LIMIT_CAP = 56 * 1024 * 1024


def _vmem_limit(nbytes):
    return int(min(max(nbytes, 16 * 1024 * 1024), V7X_VMEM_LIMIT_CAP))


def _row_tile(m):
    for t in (1024, 512, 256, 128):
        if m % t == 0:
            return t
    raise ValueError(f"token count {m} must be a multiple of 128")


def _rms(xf):
    return xf * lax.rsqrt(jnp.mean(xf * xf, axis=-1, keepdims=True) + NORM_EPS)


def _sigmoid(v):
    return 1.0 / (1.0 + jnp.exp(-v))


def _split_bf16(v, n):
    parts, r = [], v
    for _ in range(n):
        p = r.astype(BF16)
        parts.append(p)
        r = r - p.astype(F32)
    return parts


def _dot_split(parts, rhs):
    acc = None
    for p in parts:
        d = jnp.dot(p, rhs, preferred_element_type=F32)
        acc = d if acc is None else acc + d
    return acc


def _expand_heads(v, e_bf16):
    return _dot_split(_split_bf16(v, 2), e_bf16)


def _head_expander(n_heads, head_dim):
    rows = jnp.arange(LANES)[:, None]
    cols = jnp.arange(n_heads * head_dim)[None, :] // head_dim
    return (rows == cols).astype(BF16)


def _rope_table_kernel(pos_ref, inv_ref, sign_ref, cos_ref, sin_ref):
    ang = pos_ref[...].astype(F32) * inv_ref[...]
    cos_ref[...] = jnp.cos(ang)
    sin_ref[...] = jnp.sin(ang) * sign_ref[...]


def _rope_tables(pos_col):
    m = pos_col.shape[0]
    tm = _row_tile(m)
    half = HEAD_DIM // 2
    inv = ROPE_THETA ** (-jnp.arange(half, dtype=F32) / half)
    inv_l = jnp.tile(inv, LANES // half)[None, :]
    sign_l = jnp.where((jnp.arange(LANES) % HEAD_DIM) < half, -1.0, 1.0).astype(F32)[None, :]
    row = pl.BlockSpec((tm, LANES), lambda i: (i, 0))
    const = pl.BlockSpec((1, LANES), lambda i: (0, 0))
    return pl.pallas_call(
        _rope_table_kernel,
        out_shape=(jax.ShapeDtypeStruct((m, LANES), F32), jax.ShapeDtypeStruct((m, LANES), F32)),
        grid=(m // tm,),
        in_specs=[pl.BlockSpec((tm, 1), lambda i: (i, 0)), const, const],
        out_specs=(row, row),
        compiler_params=pltpu.CompilerParams(dimension_semantics=("parallel",)),
        name="rope_tables",
    )(pos_col, inv_l, sign_l)


def _apply_rope(y, cos, sin_signed):
    lane = lax.broadcasted_iota(jnp.int32, (1, LANES), 1)
    first_half = (lane % HEAD_DIM) < (HEAD_DIM // 2)
    outs = []
    for c in range(y.shape[1] // LANES):
        yc = y[:, c * LANES:(c + 1) * LANES]
        partner = jnp.where(first_half,
                            pltpu.roll(yc, LANES - HEAD_DIM // 2, 1),
                            pltpu.roll(yc, HEAD_DIM // 2, 1))
        outs.append(yc * cos + partner * sin_signed)
    return outs[0] if len(outs) == 1 else jnp.concatenate(outs, axis=1)


def _linear_kernel(*refs, has_gain, has_bias, has_res, rope_cols, cache_x):
    refs = list(refs)
    x_ref, w_ref = refs.pop(0), refs.pop(0)
    g_ref = refs.pop(0) if has_gain else None
    b_ref = refs.pop(0) if has_bias else None
    cos_ref, sin_ref = (refs.pop(0), refs.pop(0)) if rope_cols else (None, None)
    r_ref = refs.pop(0) if has_res else None
    o_ref = refs.pop(0)
    xn_ref = refs.pop(0) if cache_x else None
    j = pl.program_id(1)
    tn = o_ref.shape[1]

    if cache_x:
        @pl.when(j == 0)
        def _():
            xf = x_ref[...].astype(F32)
            if has_gain:
                xf = _rms(xf) * g_ref[...]
            xn_ref[...] = xf.astype(BF16)
        xb = xn_ref[...]
    else:
        xb = x_ref[...]

    acc = jnp.dot(xb, w_ref[...], preferred_element_type=F32)
    if has_bias:
        acc = acc + b_ref[...]
    if has_res:
        acc = acc + r_ref[...]

    if not rope_cols:
        o_ref[...] = acc.astype(o_ref.dtype)
    elif rope_cols % tn == 0:
        @pl.when(j < rope_cols // tn)
        def _():
            o_ref[...] = _apply_rope(acc, cos_ref[...], sin_ref[...]).astype(o_ref.dtype)

        @pl.when(j >= rope_cols // tn)
        def _():
            o_ref[...] = acc.astype(o_ref.dtype)
    else:
        roped = _apply_rope(acc[:, :rope_cols], cos_ref[...], sin_ref[...])
        o_ref[:, :rope_cols] = roped.astype(o_ref.dtype)
        o_ref[:, rope_cols:] = acc[:, rope_cols:].astype(o_ref.dtype)


def _linear(x, w, *, gain=None, bias=None, rope=None, rope_cols=0, residual=None,
            out_dtype=BF16, tn=None, name="linear"):
    m, k = x.shape
    n = w.shape[1]
    tm = _row_tile(m)
    tn = n if tn is None else tn
    assert n % tn == 0 and tn % LANES == 0
    if rope_cols and rope_cols % tn:
        assert tn == n and rope_cols % LANES == 0
    cache_x = gain is not None or x.dtype != BF16
    row = lambda i, j: (i, 0)
    col = lambda i, j: (0, j)
    args, specs = [x, w], [pl.BlockSpec((tm, k), row), pl.BlockSpec((k, tn), col)]
    if gain is not None:
        args.append(gain.reshape(1, k).astype(F32))
        specs.append(pl.BlockSpec((1, k), lambda i, j: (0, 0)))
    if bias is not None:
        args.append(bias.reshape(1, n).astype(F32))
        specs.append(pl.BlockSpec((1, tn), col))
    if rope_cols:
        args += list(rope)
        specs += [pl.BlockSpec((tm, LANES), row)] * 2
    if residual is not None:
        args.append(residual)
        specs.append(pl.BlockSpec((tm, tn), lambda i, j: (i, j)))
    out_bytes = jnp.dtype(out_dtype).itemsize
    est = (2 * tm * k * x.dtype.itemsize + 2 * k * tn * 2 + 2 * tm * tn * out_bytes
           + (2 * tm * tn * 4 if residual is not None else 0) + tm * k * 2 + 3 * tm * tn * 4
           + 4 * tm * LANES * 4)
    return pl.pallas_call(
        functools.partial(_linear_kernel, has_gain=gain is not None, has_bias=bias is not None,
                          has_res=residual is not None, rope_cols=rope_cols, cache_x=cache_x),
        out_shape=jax.ShapeDtypeStruct((m, n), out_dtype),
        grid=(m // tm, n // tn),
        in_specs=specs,
        out_specs=pl.BlockSpec((tm, tn), lambda i, j: (i, j)),
        scratch_shapes=[pltpu.VMEM((tm, k), BF16)] if cache_x else [],
        compiler_params=pltpu.CompilerParams(dimension_semantics=("parallel", "arbitrary"),
                                             vmem_limit_bytes=_vmem_limit(est)),
        name=name,
    )(*args)


def _mlp_kernel(*refs, has_final):
    refs = list(refs)
    x_ref, g_ref, wu_ref, wd_ref = refs[:4]
    gf_ref = refs[4] if has_final else None
    o_ref, xn_ref, acc_ref = refs[-3:]
    f = pl.program_id(1)

    @pl.when(f == 0)
    def _():
        xn_ref[...] = (_rms(x_ref[...]) * g_ref[...]).astype(BF16)
        acc_ref[...] = jnp.zeros_like(acc_ref)

    u = jnp.maximum(jnp.dot(xn_ref[...], wu_ref[...], preferred_element_type=F32), 0.0)
    acc_ref[...] += jnp.dot((u * u).astype(BF16), wd_ref[...], preferred_element_type=F32)

    @pl.when(f == pl.num_programs(1) - 1)
    def _():
        y = x_ref[...] + acc_ref[...]
        if has_final:
            y = _rms(y) * gf_ref[...]
        o_ref[...] = y


def _mlp(h, gain, w_up, w_down, final_gain=None):
    m, d = h.shape
    ff = w_up.shape[1]
    tm = _row_tile(m)
    tf = 512 if ff % 512 == 0 else ff
    row = lambda i, f: (i, 0)
    vec = pl.BlockSpec((1, d), lambda i, f: (0, 0))
    args = [h, gain.reshape(1, d).astype(F32), w_up, w_down]
    specs = [pl.BlockSpec((tm, d), row), vec,
             pl.BlockSpec((d, tf), lambda i, f: (0, f)), pl.BlockSpec((tf, d), lambda i, f: (f, 0))]
    if final_gain is not None:
        args.append(final_gain.reshape(1, d).astype(F32))
        specs.append(vec)
    est = 4 * tm * d * 4 + tm * d * 4 + tm * d * 2 + 8 * d * tf + 3 * tm * tf * 4 + 2 * tm * d * 4
    return pl.pallas_call(
        functools.partial(_mlp_kernel, has_final=final_gain is not None),
        out_shape=jax.ShapeDtypeStruct((m, d), F32),
        grid=(m // tm, ff // tf),
        in_specs=specs,
        out_specs=pl.BlockSpec((tm, d), row),
        scratch_shapes=[pltpu.VMEM((tm, d), BF16), pltpu.VMEM((tm, d), F32)],
        compiler_params=pltpu.CompilerParams(dimension_semantics=("parallel", "arbitrary"),
                                             vmem_limit_bytes=_vmem_limit(est)),
        name="sqrelu_mlp",
    )(*args)


def _stack_two_heads(t):
    low = lax.broadcasted_iota(jnp.int32, t.shape, 1) < HEAD_DIM
    zero = jnp.zeros_like(t)
    return jnp.concatenate([jnp.where(low, t, zero), jnp.where(low, zero, t)], axis=0)


def _band_mask(n_rows, first_block, max_dist):
    r = lax.broadcasted_iota(jnp.int32, (n_rows, 2 * ATTN_BLOCK), 0) % ATTN_BLOCK
    c = lax.broadcasted_iota(jnp.int32, (n_rows, 2 * ATTN_BLOCK), 1)
    dist = r + ATTN_BLOCK - c
    ok = (dist >= 0) & (dist <= max_dist)
    return ok & ((c >= ATTN_BLOCK) | jnp.logical_not(first_block))


def _softmax_half(s, mask, sink):
    s = jnp.where(mask, s, -jnp.inf)
    m = jnp.max(s, axis=-1, keepdims=True)
    if sink is not None:
        m = jnp.maximum(m, sink)
    p = jnp.exp(s - m)
    den = jnp.sum(p, axis=-1, keepdims=True)
    if sink is not None:
        den = den + jnp.exp(sink - m)
    return p, den, m


def _attend_pairs(q2, k_stack, v_stack, mask, sinks):
    s = lax.dot_general(q2, k_stack, (((1,), (1,)), ((), ())), preferred_element_type=F32)
    kw = 2 * ATTN_BLOCK
    p0, d0, m0 = _softmax_half(s[:, :kw], mask, None if sinks is None else sinks[0])
    p1, d1, m1 = _softmax_half(s[:, kw:], mask, None if sinks is None else sinks[1])
    p2 = jnp.concatenate([p0, p1], axis=1).astype(BF16)
    o = jnp.dot(p2, v_stack, preferred_element_type=F32)
    low = lax.broadcasted_iota(jnp.int32, o.shape, 1) < HEAD_DIM
    o = o * jnp.where(low, 1.0 / d0, 1.0 / d1)
    return o, (m0 + jnp.log(d0), m1 + jnp.log(d1))


def _swa_kernel(sink_ref, q_ref, kc_ref, kp_ref, vc_ref, vp_ref, o_ref):
    first = pl.program_id(1) == 0
    scale = jnp.asarray(HEAD_DIM ** -0.5, BF16)
    kk = jnp.concatenate([kp_ref[...], kc_ref[...]], axis=0)
    vv = jnp.concatenate([vp_ref[...], vc_ref[...]], axis=0)
    kk_sw = pltpu.roll(kk.astype(F32), HEAD_DIM, 1).astype(BF16)
    vv_sw = pltpu.roll(vv.astype(F32), HEAD_DIM, 1).astype(BF16)
    low = lax.broadcasted_iota(jnp.int32, kk.shape, 1) < HEAD_DIM
    pairs = A_Q_PER_KV // 2
    mask = _band_mask(pairs * ATTN_BLOCK, first, A_WINDOW - 1)
    for g in range(2):
        k_both = jnp.where(low, kk, kk_sw) if g == 0 else jnp.where(low, kk_sw, kk)
        v_both = jnp.where(low, vv, vv_sw) if g == 0 else jnp.where(low, vv_sw, vv)
        k_stack, v_stack = _stack_two_heads(k_both), _stack_two_heads(v_both)
        q_g = jnp.concatenate(
            [q_ref[:, (pairs * g + p) * LANES:(pairs * g + p + 1) * LANES] for p in range(pairs)], axis=0) * scale
        sinks = []
        for half in range(2):
            sinks.append(jnp.concatenate(
                [jnp.full((ATTN_BLOCK, 1), sink_ref[A_Q_PER_KV * g + 2 * p + half], F32) for p in range(pairs)],
                axis=0))
        o, _ = _attend_pairs(q_g, k_stack, v_stack, mask, sinks)
        for p in range(pairs):
            o_ref[:, (pairs * g + p) * LANES:(pairs * g + p + 1) * LANES] = (
                o[p * ATTN_BLOCK:(p + 1) * ATTN_BLOCK].astype(o_ref.dtype))


def _swa_attention(qkv, sinks, batch, seq):
    m = qkv.shape[0]
    q_dim = sinks.shape[0] * HEAD_DIM
    assert q_dim // HEAD_DIM // A_Q_PER_KV == 2 and seq % ATTN_BLOCK == 0
    q3 = qkv.reshape(batch, seq, qkv.shape[1])
    kcol, vcol = q_dim // LANES, q_dim // LANES + 1
    prev = lambda i: jnp.maximum(i - 1, 0)
    blk = lambda w: (None, ATTN_BLOCK, w)
    out = pl.pallas_call(
        _swa_kernel,
        out_shape=jax.ShapeDtypeStruct((batch, seq, q_dim), BF16),
        grid=(batch, seq // ATTN_BLOCK),
        in_specs=[
            pl.BlockSpec(memory_space=pltpu.SMEM),
            pl.BlockSpec(blk(q_dim), lambda b, i: (b, i, 0)),
            pl.BlockSpec(blk(LANES), lambda b, i: (b, i, kcol)),
            pl.BlockSpec(blk(LANES), lambda b, i: (b, prev(i), kcol)),
            pl.BlockSpec(blk(LANES), lambda b, i: (b, i, vcol)),
            pl.BlockSpec(blk(LANES), lambda b, i: (b, prev(i), vcol)),
        ],
        out_specs=pl.BlockSpec(blk(q_dim), lambda b, i: (b, i, 0)),
        compiler_params=pltpu.CompilerParams(dimension_semantics=("parallel", "parallel"),
                                             vmem_limit_bytes=_vmem_limit(24 * 1024 * 1024)),
        name="swa_sink_attention",
    )(sinks.astype(F32), q3, q3, q3, q3, q3)
    return out.reshape(m, q_dim)


def _dilated_kernel(q_ref, kc_ref, kp_ref, vc_ref, vp_ref, o_ref, lse_ref, *, max_dist):
    first = pl.program_id(2) == 0
    scale = jnp.asarray(HEAD_DIM ** -0.5, BF16)
    mask = _band_mask(ATTN_BLOCK, first, max_dist)
    lane = lax.broadcasted_iota(jnp.int32, (ATTN_BLOCK, LANES), 1)
    lse_tile = jnp.zeros((ATTN_BLOCK, LANES), F32)
    for p in range(q_ref.shape[1] // LANES):
        sl = slice(p * LANES, (p + 1) * LANES)
        k_stack = _stack_two_heads(jnp.concatenate([kp_ref[:, sl], kc_ref[:, sl]], axis=0))
        v_stack = _stack_two_heads(jnp.concatenate([vp_ref[:, sl], vc_ref[:, sl]], axis=0))
        o, (l0, l1) = _attend_pairs(q_ref[:, sl] * scale, k_stack, v_stack, mask, None)
        o_ref[:, sl] = o.astype(o_ref.dtype)
        lse_tile = jnp.where(lane == 2 * p, l0, jnp.where(lane == 2 * p + 1, l1, lse_tile))
    lse_ref[...] = lse_tile


def _dilated_attention(qkv, group, n_groups, batch, seq):
    m, width = qkv.shape
    window, dil = C_PATTERNS[group]
    gw = width // (3 * n_groups)
    ls = seq // dil
    assert ls % ATTN_BLOCK == 0 and gw % LANES == 0
    view = qkv.reshape(batch, ls, dil * width)
    per_row = width // gw
    qcol, kcol, vcol = group, n_groups + group, 2 * n_groups + group
    prev = lambda i: jnp.maximum(i - 1, 0)
    blk = lambda w: (None, ATTN_BLOCK, w)
    spec = lambda c, pv: pl.BlockSpec(
        blk(gw), (lambda b, r, i: (b, prev(i), r * per_row + c)) if pv else (lambda b, r, i: (b, i, r * per_row + c)))
    out, lse = pl.pallas_call(
        functools.partial(_dilated_kernel, max_dist=window // dil),
        out_shape=(jax.ShapeDtypeStruct((batch, ls, dil * gw), BF16),
                   jax.ShapeDtypeStruct((batch, ls, dil * LANES), F32)),
        grid=(batch, dil, ls // ATTN_BLOCK),
        in_specs=[spec(qcol, False), spec(kcol, False), spec(kcol, True), spec(vcol, False), spec(vcol, True)],
        out_specs=(pl.BlockSpec(blk(gw), lambda b, r, i: (b, i, r)),
                   pl.BlockSpec(blk(LANES), lambda b, r, i: (b, i, r))),
        compiler_params=pltpu.CompilerParams(dimension_semantics=("parallel", "parallel", "parallel"),
                                             vmem_limit_bytes=_vmem_limit(24 * 1024 * 1024)),
        name=f"dilated_attention_g{group}",
    )(view, view, view, view, view)
    return out.reshape(m, gw), lse.reshape(m, LANES)


def _merge_oproj_kernel(*refs, n_groups):
    o_refs, l_refs = refs[:n_groups], refs[n_groups:2 * n_groups]
    e_ref, w_ref, r_ref, out_ref = refs[2 * n_groups:]
    lses = [l[...] for l in l_refs]
    lmax = functools.reduce(jnp.maximum, lses)
    es = [jnp.exp(l - lmax) for l in lses]
    inv = 1.0 / functools.reduce(lambda a, b: a + b, es)
    merged = None
    for o_ref, e in zip(o_refs, es):
        term = _expand_heads(e * inv, e_ref[...]) * o_ref[...].astype(F32)
        merged = term if merged is None else merged + term
    out_ref[...] = r_ref[...] + jnp.dot(merged.astype(BF16), w_ref[...], preferred_element_type=F32)


def _merge_oproj(outs, lses, w_o, residual):
    m, d_in = outs[0].shape
    d = w_o.shape[1]
    tm = _row_tile(m) // 2
    n_groups = len(outs)
    row = lambda i: (i, 0)
    const = lambda i: (0, 0)
    expander = _head_expander(d_in // HEAD_DIM, HEAD_DIM)
    est = 2 * n_groups * tm * (d_in * 2 + LANES * 4) + 4 * d_in * d + 4 * tm * d * 4 + 6 * tm * d_in * 4
    return pl.pallas_call(
        functools.partial(_merge_oproj_kernel, n_groups=n_groups),
        out_shape=jax.ShapeDtypeStruct((m, d), F32),
        grid=(m // tm,),
        in_specs=([pl.BlockSpec((tm, d_in), row)] * n_groups + [pl.BlockSpec((tm, LANES), row)] * n_groups
                  + [pl.BlockSpec((LANES, d_in), const), pl.BlockSpec((d_in, d), const), pl.BlockSpec((tm, d), row)]),
        out_specs=pl.BlockSpec((tm, d), row),
        compiler_params=pltpu.CompilerParams(dimension_semantics=("parallel",),
                                             vmem_limit_bytes=_vmem_limit(est)),
        name="dilated_merge_oproj",
    )(*outs, *lses, expander, w_o, residual)


def _ssd_kernel(z_ref, x_ref, b_ref, c_ref, dt_ref, cw_ref, cb_ref, dtb_ref, alog_ref, dskip_ref, nw_ref, e_ref,
                o_ref, xpad_ref, state_ref, xs_ref, bm_ref, cm_ref):
    q = SSM_CHUNK
    d_inner = x_ref.shape[1]
    bc_dim = b_ref.shape[1]
    n_heads = d_inner // SSM_HEAD_DIM
    group_w = d_inner // SSM_N_GROUPS
    pad = SUBLANES

    @pl.when(pl.program_id(1) == 0)
    def _():
        xpad_ref[0:pad, :] = jnp.zeros((pad, xpad_ref.shape[1]), F32)
        state_ref[...] = jnp.zeros_like(state_ref)

    xpad_ref[pad:pad + q, 0:d_inner] = x_ref[...].astype(F32)
    xpad_ref[pad:pad + q, d_inner:d_inner + bc_dim] = b_ref[...].astype(F32)
    xpad_ref[pad:pad + q, d_inner + bc_dim:] = c_ref[...].astype(F32)
    conv_tile = 512
    for t in range(xpad_ref.shape[1] // conv_tile):
        sl = slice(t * conv_tile, (t + 1) * conv_tile)
        acc = cb_ref[:, sl] + cw_ref[SSM_CONV - 1:SSM_CONV, sl] * xpad_ref[pad:pad + q, sl]
        for k in range(1, SSM_CONV):
            acc = acc + cw_ref[SSM_CONV - 1 - k:SSM_CONV - k, sl] * xpad_ref[pad - k:pad - k + q, sl]
        y = acc * _sigmoid(acc)
        lo = t * conv_tile
        if lo < d_inner:
            xs_ref[:, lo:lo + conv_tile] = y
        elif lo < d_inner + bc_dim:
            bm_ref[:, lo - d_inner:lo - d_inner + conv_tile] = y
        else:
            cm_ref[:, lo - d_inner - bc_dim:lo - d_inner - bc_dim + conv_tile] = y.astype(BF16)
    xpad_ref[0:pad, :] = xpad_ref[q:q + pad, :]

    head_lane = lax.broadcasted_iota(jnp.int32, (q, LANES), 1) < n_heads
    dtr = dt_ref[...] + dtb_ref[...]
    dt = jnp.maximum(dtr, 0.0) + jnp.log(1.0 + jnp.exp(-jnp.abs(dtr)))
    dt = jnp.where(head_lane, dt, 0.0)
    d_a = dt * (-jnp.exp(alog_ref[...]))
    rr = lax.broadcasted_iota(jnp.int32, (q, q), 0)
    cc = lax.broadcasted_iota(jnp.int32, (q, q), 1)
    tril = rr >= cc
    tri16 = jnp.where(tril, 1.0, 0.0).astype(BF16)
    cs = None
    for part in _split_bf16(d_a, 3):
        term = jnp.dot(tri16, part, preferred_element_type=F32)
        cs = term if cs is None else cs + term
    cs_t = cs.T
    ecs = jnp.exp(cs)
    dec = jnp.exp(cs[q - 1:q, :] - cs)
    e16 = e_ref[...]
    dt_e = _expand_heads(dt, e16)
    ecs_e = _expand_heads(ecs, e16)
    dec_e = _expand_heads(dec, e16)
    chunk_decay_e = ecs_e[q - 1:q, :]

    heads_per_group = n_heads // SSM_N_GROUPS
    for g in range(SSM_N_GROUPS):
        gs = slice(g * group_w, (g + 1) * group_w)
        ns = slice(g * SSM_D_STATE, (g + 1) * SSM_D_STATE)
        b_f = bm_ref[:, ns]
        b16 = b_f.astype(BF16)
        bt16 = b_f.T.astype(BF16)
        c16 = cm_ref[:, ns]
        cb = lax.dot_general(c16, b16, (((1,), (1,)), ((), ())), preferred_element_type=F32)
        xs_g = xs_ref[:, gs]
        xdt = xs_g * dt_e[:, gs]
        xdt16 = xdt.astype(BF16)
        y_parts = []
        for pr in range(heads_per_group // 2):
            h0 = g * heads_per_group + 2 * pr
            ms = []
            for hh in (h0, h0 + 1):
                diff = cs[:, hh:hh + 1] - cs_t[hh:hh + 1, :]
                ms.append(cb * jnp.exp(jnp.where(tril, diff, -jnp.inf)))
            m2 = jnp.concatenate(ms, axis=1).astype(BF16)
            x_stack = _stack_two_heads(xdt16[:, pr * LANES:(pr + 1) * LANES])
            y_parts.append(jnp.dot(m2, x_stack, preferred_element_type=F32))
        y_diag = jnp.concatenate(y_parts, axis=1)
        st_old = state_ref[:, gs]
        y_off = jnp.dot(c16, st_old.astype(BF16), preferred_element_type=F32) * ecs_e[:, gs]
        st_new = jnp.dot(bt16, (xdt * dec_e[:, gs]).astype(BF16), preferred_element_type=F32)
        state_ref[:, gs] = st_old * chunk_decay_e[:, gs] + st_new
        y = y_diag + y_off + dskip_ref[:, gs] * xs_g
        zf = z_ref[:, gs].astype(F32)
        gated = y * (zf * _sigmoid(zf))
        o_ref[:, gs] = (_rms(gated) * nw_ref[:, gs]).astype(o_ref.dtype)


def _ssd_mixer(zx, dt_raw, conv_w, conv_b, dt_bias, a_log, d_skip, norm_w, batch, seq):
    m = zx.shape[0]
    n_heads = a_log.shape[0]
    d_inner = n_heads * SSM_HEAD_DIM
    bc_dim = SSM_N_GROUPS * SSM_D_STATE
    conv_dim = d_inner + 2 * bc_dim
    assert zx.shape[1] == d_inner + conv_dim and seq % SSM_CHUNK == 0 and n_heads <= LANES
    assert d_inner % bc_dim == 0 and conv_dim % 512 == 0 and d_inner // SSM_N_GROUPS == 2 * LANES
    zx3 = zx.reshape(batch, seq, zx.shape[1])
    dt3 = dt_raw.reshape(batch, seq, LANES)
    pad_heads = lambda v: jnp.pad(v.astype(F32), (0, LANES - n_heads))[None, :]
    blk = lambda w: (None, SSM_CHUNK, w)
    const = lambda shape: pl.BlockSpec(shape, lambda b, c: (0,) * len(shape))
    r = d_inner // bc_dim
    out = pl.pallas_call(
        _ssd_kernel,
        out_shape=jax.ShapeDtypeStruct((batch, seq, d_inner), BF16),
        grid=(batch, seq // SSM_CHUNK),
        in_specs=[
            pl.BlockSpec(blk(d_inner), lambda b, c: (b, c, 0)),
            pl.BlockSpec(blk(d_inner), lambda b, c: (b, c, 1)),
            pl.BlockSpec(blk(bc_dim), lambda b, c: (b, c, 2 * r)),
            pl.BlockSpec(blk(bc_dim), lambda b, c: (b, c, 2 * r + 1)),
            pl.BlockSpec(blk(LANES), lambda b, c: (b, c, 0)),
            const((SSM_CONV, conv_dim)), const((1, conv_dim)), const((1, LANES)), const((1, LANES)),
            const((1, d_inner)), const((1, d_inner)), const((LANES, d_inner)),
        ],
        out_specs=pl.BlockSpec(blk(d_inner), lambda b, c: (b, c, 0)),
        scratch_shapes=[
            pltpu.VMEM((SSM_CHUNK + SUBLANES, conv_dim), F32),
            pltpu.VMEM((SSM_D_STATE, d_inner), F32),
            pltpu.VMEM((SSM_CHUNK, d_inner), F32),
            pltpu.VMEM((SSM_CHUNK, bc_dim), F32),
            pltpu.VMEM((SSM_CHUNK, bc_dim), BF16),
        ],
        compiler_params=pltpu.CompilerParams(dimension_semantics=("parallel", "arbitrary"),
                                             vmem_limit_bytes=_vmem_limit(40 * 1024 * 1024)),
        name="ssd_mixer",
    )(zx3, zx3, zx3, zx3, dt3, conv_w.astype(F32), conv_b.astype(F32)[None, :], pad_heads(dt_bias),
      pad_heads(a_log), jnp.repeat(d_skip.astype(F32), SSM_HEAD_DIM)[None, :], norm_w.astype(F32)[None, :],
      _head_expander(n_heads, SSM_HEAD_DIM))
    return out.reshape(m, d_inner)


def kernel(x, positions, norm_mix_w, norm_mlp_w, a_w_qkv, a_b_qkv, a_sinks, a_w_o, a_b_o, b_in_w, b_conv_w, b_conv_b, b_dt_bias, b_a_log, b_d, b_norm_w, b_out_w, c_w_qkv, c_w_o, mlp_w_up, mlp_w_down, final_norm_w):
    batch, seq, d_model = x.shape
    m = batch * seq
    depth = norm_mix_w.shape[0]
    h = x.reshape(m, d_model).astype(F32)
    rope = _rope_tables(positions.reshape(m, 1))

    for i in range(depth):
        kind, j = i % N_MIXERS, i // N_MIXERS
        if kind == 0:
            q_dim = a_sinks.shape[1] * HEAD_DIM
            kv_dim = (a_w_qkv.shape[2] - q_dim) // 2
            qkv = _linear(h, a_w_qkv[j].astype(BF16), gain=norm_mix_w[i], bias=a_b_qkv[j], rope=rope,
                          rope_cols=q_dim + kv_dim, name="swa_qkv")
            att = _swa_attention(qkv, a_sinks[j], batch, seq)
            h = _linear(att, a_w_o[j].astype(BF16), bias=a_b_o[j], residual=h, out_dtype=F32, name="swa_out")
        elif kind == 1:
            n_heads = b_a_log.shape[1]
            w_in = b_in_w[j]
            main = w_in.shape[1] - n_heads
            zx = _linear(h, w_in[:, :main].astype(BF16), gain=norm_mix_w[i], tn=1024, name="ssm_in")
            w_dt = jnp.pad(w_in[:, main:], ((0, 0), (0, LANES - n_heads))).astype(BF16)
            dt_raw = _linear(h, w_dt, gain=norm_mix_w[i], out_dtype=F32, name="ssm_dt")
            g = _ssd_mixer(zx, dt_raw, b_conv_w[j], b_conv_b[j], b_dt_bias[j], b_a_log[j], b_d[j], b_norm_w[j],
                           batch, seq)
            h = _linear(g, b_out_w[j].astype(BF16), residual=h, out_dtype=F32, name="ssm_out")
        else:
            n_groups = len(C_PATTERNS)
            width = c_w_qkv.shape[2]
            qkv = _linear(h, c_w_qkv[j].astype(BF16), gain=norm_mix_w[i], rope=rope,
                          rope_cols=2 * width // 3, tn=1024, name="dilated_qkv")
            outs, lses = zip(*[_dilated_attention(qkv, g, n_groups, batch, seq) for g in range(n_groups)])
            h = _merge_oproj(outs, lses, c_w_o[j].astype(BF16), h)
        final_gain = final_norm_w if i == depth - 1 else None
        h = _mlp(h, norm_mlp_w[i], mlp_w_up[i].astype(BF16), mlp_w_down[i].astype(BF16), final_gain)
    return h.reshape(batch, seq, d_model).astype(x.dtype)
```

```python
import functools
import math

import jax
import jax.numpy as jnp
from jax import lax
from jax.experimental import pallas as pl
from jax.experimental.pallas import tpu as pltpu

F32 = jnp.float32
BF16 = jnp.bfloat16

N_MIXERS = 3
ATTN_BLOCK = 128
HEAD_DIM = 64
ROPE_THETA = 10000.0
NORM_EPS = 1e-5
A_Q_PER_KV = 8
A_WINDOW = 128
C_PATTERNS = ((128, 1), (512, 4), (2048, 16))
SSM_HEAD_DIM = 64
SSM_N_GROUPS = 8
SSM_D_STATE = 128
SSM_CONV = 4
SSM_CHUNK = 128

LANES = 128
SUBLANES = 8
V7X_VMEM_LIMIT_CAP = 56 * 1024 * 1024


def _vmem_limit(nbytes):
    return int(min(max(nbytes, 16 * 1024 * 1024), V7X_VMEM_LIMIT_CAP))


def _row_tile(m):
    for t in (1024, 512, 256, 128):
        if m % t == 0:
            return t
    raise ValueError(f"token count {m} must be a multiple of 128")


def _rms(xf):
    return xf * lax.rsqrt(jnp.mean(xf * xf, axis=-1, keepdims=True) + NORM_EPS)


def _sigmoid(v):
    return 1.0 / (1.0 + jnp.exp(-v))


def _split_bf16(v, n):
    parts, r = [], v
    for _ in range(n):
        p = r.astype(BF16)
        parts.append(p)
        r = r - p.astype(F32)
    return parts


def _dot_split(parts, rhs):
    acc = None
    for p in parts:
        d = jnp.dot(p, rhs, preferred_element_type=F32)
        acc = d if acc is None else acc + d
    return acc


def _expand_heads(v, e_bf16):
    return _dot_split(_split_bf16(v, 2), e_bf16)


def _head_expander(n_heads, head_dim):
    rows = jnp.arange(LANES)[:, None]
    cols = jnp.arange(n_heads * head_dim)[None, :] // head_dim
    return (rows == cols).astype(BF16)


def _rope_table_kernel(pos_ref, inv_ref, sign_ref, cos_ref, sin_ref):
    ang = pos_ref[...].astype(F32) * inv_ref[...]
    cos_ref[...] = jnp.cos(ang)
    sin_ref[...] = jnp.sin(ang) * sign_ref[...]


def _rope_tables(pos_col):
    m = pos_col.shape[0]
    tm = _row_tile(m)
    half = HEAD_DIM // 2
    inv = ROPE_THETA ** (-jnp.arange(half, dtype=F32) / half)
    inv_l = jnp.tile(inv, LANES // half)[None, :]
    sign_l = jnp.where((jnp.arange(LANES) % HEAD_DIM) < half, -1.0, 1.0).astype(F32)[None, :]
    row = pl.BlockSpec((tm, LANES), lambda i: (i, 0))
    const = pl.BlockSpec((1, LANES), lambda i: (0, 0))
    return pl.pallas_call(
        _rope_table_kernel,
        out_shape=(jax.ShapeDtypeStruct((m, LANES), F32), jax.ShapeDtypeStruct((m, LANES), F32)),
        grid=(m // tm,),
        in_specs=[pl.BlockSpec((tm, 1), lambda i: (i, 0)), const, const],
        out_specs=(row, row),
        compiler_params=pltpu.CompilerParams(dimension_semantics=("parallel",)),
        name="rope_tables",
    )(pos_col, inv_l, sign_l)


def _apply_rope(y, cos, sin_signed):
    lane = lax.broadcasted_iota(jnp.int32, (1, LANES), 1)
    first_half = (lane % HEAD_DIM) < (HEAD_DIM // 2)
    outs = []
    for c in range(y.shape[1] // LANES):
        yc = y[:, c * LANES:(c + 1) * LANES]
        partner = jnp.where(first_half,
                            pltpu.roll(yc, LANES - HEAD_DIM // 2, 1),
                            pltpu.roll(yc, HEAD_DIM // 2, 1))
        outs.append(yc * cos + partner * sin_signed)
    return outs[0] if len(outs) == 1 else jnp.concatenate(outs, axis=1)


def _linear_kernel(*refs, has_gain, has_bias, has_res, rope_cols, cache_x):
    refs = list(refs)
    x_ref, w_ref = refs.pop(0), refs.pop(0)
    g_ref = refs.pop(0) if has_gain else None
    b_ref = refs.pop(0) if has_bias else None
    cos_ref, sin_ref = (refs.pop(0), refs.pop(0)) if rope_cols else (None, None)
    r_ref = refs.pop(0) if has_res else None
    o_ref = refs.pop(0)
    xn_ref = refs.pop(0) if cache_x else None
    j = pl.program_id(1)
    tn = o_ref.shape[1]

    if cache_x:
        @pl.when(j == 0)
        def _():
            xf = x_ref[...].astype(F32)
            if has_gain:
                xf = _rms(xf) * g_ref[...]
            xn_ref[...] = xf.astype(BF16)
        xb = xn_ref[...]
    else:
        xb = x_ref[...]

    acc = jnp.dot(xb, w_ref[...], preferred_element_type=F32)
    if has_bias:
        acc = acc + b_ref[...]
    if has_res:
        acc = acc + r_ref[...]

    if not rope_cols:
        o_ref[...] = acc.astype(o_ref.dtype)
    elif rope_cols % tn == 0:
        @pl.when(j < rope_cols // tn)
        def _():
            o_ref[...] = _apply_rope(acc, cos_ref[...], sin_ref[...]).astype(o_ref.dtype)

        @pl.when(j >= rope_cols // tn)
        def _():
            o_ref[...] = acc.astype(o_ref.dtype)
    else:
        roped = _apply_rope(acc[:, :rope_cols], cos_ref[...], sin_ref[...])
        o_ref[:, :rope_cols] = roped.astype(o_ref.dtype)
        o_ref[:, rope_cols:] = acc[:, rope_cols:].astype(o_ref.dtype)


def _linear(x, w, *, gain=None, bias=None, rope=None, rope_cols=0, residual=None,
            out_dtype=BF16, tn=None, name="linear"):
    m, k = x.shape
    n = w.shape[1]
    tm = _row_tile(m)
    tn = n if tn is None else tn
    assert n % tn == 0 and tn % LANES == 0
    if rope_cols and rope_cols % tn:
        assert tn == n and rope_cols % LANES == 0
    cache_x = gain is not None or x.dtype != BF16
    row = lambda i, j: (i, 0)
    col = lambda i, j: (0, j)
    args, specs = [x, w], [pl.BlockSpec((tm, k), row), pl.BlockSpec((k, tn), col)]
    if gain is not None:
        args.append(gain.reshape(1, k).astype(F32))
        specs.append(pl.BlockSpec((1, k), lambda i, j: (0, 0)))
    if bias is not None:
        args.append(bias.reshape(1, n).astype(F32))
        specs.append(pl.BlockSpec((1, tn), col))
    if rope_cols:
        args += list(rope)
        specs += [pl.BlockSpec((tm, LANES), row)] * 2
    if residual is not None:
        args.append(residual)
        specs.append(pl.BlockSpec((tm, tn), lambda i, j: (i, j)))
    out_bytes = jnp.dtype(out_dtype).itemsize
    est = (2 * tm * k * x.dtype.itemsize + 2 * k * tn * 2 + 2 * tm * tn * out_bytes
           + (2 * tm * tn * 4 if residual is not None else 0) + tm * k * 2 + 3 * tm * tn * 4
           + 4 * tm * LANES * 4)
    return pl.pallas_call(
        functools.partial(_linear_kernel, has_gain=gain is not None, has_bias=bias is not None,
                          has_res=residual is not None, rope_cols=rope_cols, cache_x=cache_x),
        out_shape=jax.ShapeDtypeStruct((m, n), out_dtype),
        grid=(m // tm, n // tn),
        in_specs=specs,
        out_specs=pl.BlockSpec((tm, tn), lambda i, j: (i, j)),
        scratch_shapes=[pltpu.VMEM((tm, k), BF16)] if cache_x else [],
        compiler_params=pltpu.CompilerParams(dimension_semantics=("parallel", "arbitrary"),
                                             vmem_limit_bytes=_vmem_limit(est)),
        name=name,
    )(*args)


def _mlp_kernel(*refs, has_final):
    refs = list(refs)
    x_ref, g_ref, wu_ref, wd_ref = refs[:4]
    gf_ref = refs[4] if has_final else None
    o_ref, xn_ref, acc_ref = refs[-3:]
    f = pl.program_id(1)

    @pl.when(f == 0)
    def _():
        xn_ref[...] = (_rms(x_ref[...]) * g_ref[...]).astype(BF16)
        acc_ref[...] = jnp.zeros_like(acc_ref)

    u = jnp.maximum(jnp.dot(xn_ref[...], wu_ref[...], preferred_element_type=F32), 0.0)
    acc_ref[...] += jnp.dot((u * u).astype(BF16), wd_ref[...], preferred_element_type=F32)

    @pl.when(f == pl.num_programs(1) - 1)
    def _():
        y = x_ref[...] + acc_ref[...]
        if has_final:
            y = _rms(y) * gf_ref[...]
        o_ref[...] = y


def _mlp(h, gain, w_up, w_down, final_gain=None):
    m, d = h.shape
    ff = w_up.shape[1]
    tm = _row_tile(m)
    tf = 512 if ff % 512 == 0 else ff
    row = lambda i, f: (i, 0)
    vec = pl.BlockSpec((1, d), lambda i, f: (0, 0))
    args = [h, gain.reshape(1, d).astype(F32), w_up, w_down]
    specs = [pl.BlockSpec((tm, d), row), vec,
             pl.BlockSpec((d, tf), lambda i, f: (0, f)), pl.BlockSpec((tf, d), lambda i, f: (f, 0))]
    if final_gain is not None:
        args.append(final_gain.reshape(1, d).astype(F32))
        specs.append(vec)
    est = 4 * tm * d * 4 + tm * d * 4 + tm * d * 2 + 8 * d * tf + 3 * tm * tf * 4 + 2 * tm * d * 4
    return pl.pallas_call(
        functools.partial(_mlp_kernel, has_final=final_gain is not None),
        out_shape=jax.ShapeDtypeStruct((m, d), F32),
        grid=(m // tm, ff // tf),
        in_specs=specs,
        out_specs=pl.BlockSpec((tm, d), row),
        scratch_shapes=[pltpu.VMEM((tm, d), BF16), pltpu.VMEM((tm, d), F32)],
        compiler_params=pltpu.CompilerParams(dimension_semantics=("parallel", "arbitrary"),
                                             vmem_limit_bytes=_vmem_limit(est)),
        name="sqrelu_mlp",
    )(*args)


def _stack_two_heads(t):
    low = lax.broadcasted_iota(jnp.int32, t.shape, 1) < HEAD_DIM
    zero = jnp.zeros_like(t)
    return jnp.concatenate([jnp.where(low, t, zero), jnp.where(low, zero, t)], axis=0)


ATTN_PIPE_DEPTH = 2


def _init_band_consts(bias_ref, ones_ref, max_dist):
    r = lax.broadcasted_iota(jnp.int32, (ATTN_BLOCK, 2 * ATTN_BLOCK), 0)
    c = lax.broadcasted_iota(jnp.int32, (ATTN_BLOCK, 2 * ATTN_BLOCK), 1)
    dist = r + ATTN_BLOCK - c
    ok = (dist >= 0) & (dist <= max_dist)
    bias_ref[0] = jnp.where(ok & (c >= ATTN_BLOCK), 0.0, -jnp.inf)
    bias_ref[1] = jnp.where(ok, 0.0, -jnp.inf)
    low = lax.broadcasted_iota(jnp.int32, (2 * ATTN_BLOCK, LANES), 1) < HEAD_DIM
    ones_ref[0:2 * ATTN_BLOCK, :] = jnp.where(low, 1.0, 0.0).astype(BF16)
    ones_ref[2 * ATTN_BLOCK:, :] = jnp.where(low, 0.0, 1.0).astype(BF16)


def _scores(q2, kk):
    return lax.dot_general(q2, _stack_two_heads(kk), (((1,), (1,)), ((), ())), preferred_element_type=F32)


def _probs(s, bias, sinks):
    kw = 2 * ATTN_BLOCK
    s0, s1 = s[:, :kw] + bias, s[:, kw:] + bias
    m0 = jnp.max(s0, axis=-1, keepdims=True)
    m1 = jnp.max(s1, axis=-1, keepdims=True)
    if sinks is not None:
        m0, m1 = jnp.maximum(m0, sinks[0]), jnp.maximum(m1, sinks[1])
    p2 = jnp.concatenate([jnp.exp(s0 - m0), jnp.exp(s1 - m1)], axis=1).astype(BF16)
    low = lax.broadcasted_iota(jnp.int32, (s.shape[0], LANES), 1) < HEAD_DIM
    return p2, jnp.where(low, m0, m1)


def _weighted(p2, m_l, vv, ones_stack, sinks):
    o = jnp.dot(p2, _stack_two_heads(vv), preferred_element_type=F32)
    den = jnp.dot(p2, ones_stack, preferred_element_type=F32)
    if sinks is not None:
        low = lax.broadcasted_iota(jnp.int32, o.shape, 1) < HEAD_DIM
        den = den + jnp.exp(jnp.where(low, sinks[0], sinks[1]) - m_l)
    return o / den, m_l + jnp.log(den)


def _run_pipelined(n, score_stage, prob_stage, out_stage):
    ahead = {i: score_stage(i) for i in range(min(ATTN_PIPE_DEPTH, n))}
    for i in range(n):
        mid = prob_stage(i, ahead.pop(i))
        if i + ATTN_PIPE_DEPTH < n:
            ahead[i + ATTN_PIPE_DEPTH] = score_stage(i + ATTN_PIPE_DEPTH)
        out_stage(i, mid)


def _swa_kernel(sink_ref, q_ref, k_ref, v_ref, o_ref, kb_ref, vb_ref, bias_ref, ones_ref):
    seq = q_ref.shape[0]
    p = pl.program_id(1)
    _init_band_consts(bias_ref, ones_ref, A_WINDOW - 1)
    rows = 2 * ATTN_BLOCK
    low = lax.broadcasted_iota(jnp.int32, (rows, LANES), 1) < HEAD_DIM
    keep = jnp.logical_xor(low, p >= A_Q_PER_KV // 2)
    for src, dst in ((k_ref, kb_ref), (v_ref, vb_ref)):
        for c in range(seq // rows):
            t = src[c * rows:(c + 1) * rows, :].astype(F32)
            dst[c * rows:(c + 1) * rows, :] = jnp.where(keep, t, pltpu.roll(t, HEAD_DIM, 1)).astype(BF16)
    sinks = (sink_ref[2 * p], sink_ref[2 * p + 1])
    scale = jnp.asarray(HEAD_DIM ** -0.5, BF16)
    blk = lambda i: slice(i * ATTN_BLOCK, (i + 1) * ATTN_BLOCK)

    def keys(ref, i):
        if i == 0:
            return jnp.concatenate([ref[blk(0), :], ref[blk(0), :]], axis=0)
        return ref[(i - 1) * ATTN_BLOCK:(i + 1) * ATTN_BLOCK, :]

    def score_stage(i):
        return _scores(q_ref[blk(i), :] * scale, keys(kb_ref, i))

    def prob_stage(i, s):
        return _probs(s, bias_ref[min(i, 1)], sinks)

    def out_stage(i, mid):
        o, _ = _weighted(*mid, keys(vb_ref, i), ones_ref[...], sinks)
        o_ref[blk(i), :] = o.astype(o_ref.dtype)

    _run_pipelined(seq // ATTN_BLOCK, score_stage, prob_stage, out_stage)


def _swa_attention(qkv, sinks, batch, seq):
    m = qkv.shape[0]
    q_dim = sinks.shape[0] * HEAD_DIM
    assert q_dim // HEAD_DIM // A_Q_PER_KV == 2 and seq % (2 * ATTN_BLOCK) == 0
    q3 = qkv.reshape(batch, seq, qkv.shape[1])
    kcol, vcol = q_dim // LANES, q_dim // LANES + 1
    slab = (None, seq, LANES)
    out = pl.pallas_call(
        _swa_kernel,
        out_shape=jax.ShapeDtypeStruct((batch, seq, q_dim), BF16),
        grid=(batch, q_dim // LANES),
        in_specs=[
            pl.BlockSpec(memory_space=pltpu.SMEM),
            pl.BlockSpec(slab, lambda b, p: (b, 0, p)),
            pl.BlockSpec(slab, lambda b, p: (b, 0, kcol)),
            pl.BlockSpec(slab, lambda b, p: (b, 0, vcol)),
        ],
        out_specs=pl.BlockSpec(slab, lambda b, p: (b, 0, p)),
        scratch_shapes=[pltpu.VMEM((seq, LANES), BF16), pltpu.VMEM((seq, LANES), BF16),
                        pltpu.VMEM((2, ATTN_BLOCK, 2 * ATTN_BLOCK), F32), pltpu.VMEM((4 * ATTN_BLOCK, LANES), BF16)],
        compiler_params=pltpu.CompilerParams(dimension_semantics=("parallel", "arbitrary"),
                                             vmem_limit_bytes=_vmem_limit(16 * seq * LANES * 2)),
        name="swa_sink_attention",
    )(sinks.astype(F32), q3, q3, q3)
    return out.reshape(m, q_dim)


def _dilated_kernel(*refs):
    n_groups = len(C_PATTERNS)
    in_refs, o_ref = refs[:3 * n_groups], refs[3 * n_groups]
    acc_o, acc_l, bias_ref, ones_ref = refs[3 * n_groups + 1:]
    seq = o_ref.shape[0]
    scale = HEAD_DIM ** -0.5
    max_dist = C_PATTERNS[0][0] // C_PATTERNS[0][1]
    assert all(w // d == max_dist for w, d in C_PATTERNS)
    _init_band_consts(bias_ref, ones_ref, max_dist)

    blocks = []
    for g, (window, dil) in enumerate(C_PATTERNS):
        for r in range(dil):
            for i in range(seq // dil // ATTN_BLOCK):
                blocks.append((g, dil, r, i))

    def rows(dil, r, i):
        start = i * ATTN_BLOCK * dil + r
        return pl.ds(start, ATTN_BLOCK, stride=dil) if dil > 1 else pl.ds(start, ATTN_BLOCK)

    def keys(ref, dil, r, i):
        prev = ref[rows(dil, r, max(i - 1, 0)), :]
        return jnp.concatenate([prev, ref[rows(dil, r, i), :]], axis=0).astype(BF16)

    def score_stage(n):
        g, dil, r, i = blocks[n]
        q2 = (in_refs[3 * g][rows(dil, r, i), :] * scale).astype(BF16)
        return _scores(q2, keys(in_refs[3 * g + 1], dil, r, i))

    def prob_stage(n, s):
        return _probs(s, bias_ref[min(blocks[n][3], 1)], None)

    def out_stage(n, mid):
        g, dil, r, i = blocks[n]
        cur = rows(dil, r, i)
        o, lse = _weighted(*mid, keys(in_refs[3 * g + 2], dil, r, i), ones_ref[...], None)
        if g > 0:
            o_old, l_old = acc_o[cur, :], acc_l[cur, :]
            l_max = jnp.maximum(l_old, lse)
            e_old, e_new = jnp.exp(l_old - l_max), jnp.exp(lse - l_max)
            inv = 1.0 / (e_old + e_new)
            o = o_old * (e_old * inv) + o * (e_new * inv)
            lse = l_max + jnp.log(e_old + e_new)
        acc_o[cur, :] = o
        if g < n_groups - 1:
            acc_l[cur, :] = lse

    _run_pipelined(len(blocks), score_stage, prob_stage, out_stage)
    o_ref[...] = acc_o[...].astype(o_ref.dtype)


def _dilated_attention(qkv, batch, seq):
    m, width = qkv.shape
    n_groups = len(C_PATTERNS)
    gw = width // (3 * n_groups)
    assert gw % LANES == 0 and all(seq % (d * ATTN_BLOCK) == 0 for _, d in C_PATTERNS)
    pairs = gw // LANES
    q3 = qkv.reshape(batch, seq, width)
    slab = lambda c: pl.BlockSpec((None, seq, LANES), lambda b, p: (b, 0, c * pairs + p))
    specs = []
    for g in range(n_groups):
        specs += [slab(g), slab(n_groups + g), slab(2 * n_groups + g)]
    out = pl.pallas_call(
        _dilated_kernel,
        out_shape=jax.ShapeDtypeStruct((batch, seq, gw), BF16),
        grid=(batch, pairs),
        in_specs=specs,
        out_specs=pl.BlockSpec((None, seq, LANES), lambda b, p: (b, 0, p)),
        scratch_shapes=[pltpu.VMEM((seq, LANES), F32), pltpu.VMEM((seq, LANES), F32),
                        pltpu.VMEM((2, ATTN_BLOCK, 2 * ATTN_BLOCK), F32), pltpu.VMEM((4 * ATTN_BLOCK, LANES), BF16)],
        compiler_params=pltpu.CompilerParams(
            dimension_semantics=("parallel", "parallel"),
            vmem_limit_bytes=_vmem_limit(2 * (3 * n_groups + 2) * seq * LANES * 4 + 8 * 1024 * 1024)),
        name="dilated_attention",
    )(*([q3] * (3 * n_groups)))
    return out.reshape(m, gw)


def _ssd_kernel(z_ref, x_ref, b_ref, c_ref, dt_ref, cw_ref, cb_ref, dtb_ref, alog_ref, dskip_ref, nw_ref, e_ref,
                o_ref, xpad_ref, state_ref, xs_ref, bm_ref, cm_ref):
    q = SSM_CHUNK
    d_inner = x_ref.shape[1]
    bc_dim = b_ref.shape[1]
    n_heads = d_inner // SSM_HEAD_DIM
    group_w = d_inner // SSM_N_GROUPS
    pad = SUBLANES

    @pl.when(pl.program_id(1) == 0)
    def _():
        xpad_ref[0:pad, :] = jnp.zeros((pad, xpad_ref.shape[1]), F32)
        state_ref[...] = jnp.zeros_like(state_ref)

    xpad_ref[pad:pad + q, 0:d_inner] = x_ref[...].astype(F32)
    xpad_ref[pad:pad + q, d_inner:d_inner + bc_dim] = b_ref[...].astype(F32)
    xpad_ref[pad:pad + q, d_inner + bc_dim:] = c_ref[...].astype(F32)
    conv_tile = 512
    for t in range(xpad_ref.shape[1] // conv_tile):
        sl = slice(t * conv_tile, (t + 1) * conv_tile)
        acc = cb_ref[:, sl] + cw_ref[SSM_CONV - 1:SSM_CONV, sl] * xpad_ref[pad:pad + q, sl]
        for k in range(1, SSM_CONV):
            acc = acc + cw_ref[SSM_CONV - 1 - k:SSM_CONV - k, sl] * xpad_ref[pad - k:pad - k + q, sl]
        y = acc * _sigmoid(acc)
        lo = t * conv_tile
        if lo < d_inner:
            xs_ref[:, lo:lo + conv_tile] = y
        elif lo < d_inner + bc_dim:
            bm_ref[:, lo - d_inner:lo - d_inner + conv_tile] = y
        else:
            cm_ref[:, lo - d_inner - bc_dim:lo - d_inner - bc_dim + conv_tile] = y.astype(BF16)
    xpad_ref[0:pad, :] = xpad_ref[q:q + pad, :]

    head_lane = lax.broadcasted_iota(jnp.int32, (q, LANES), 1) < n_heads
    dtr = dt_ref[...] + dtb_ref[...]
    dt = jnp.maximum(dtr, 0.0) + jnp.log(1.0 + jnp.exp(-jnp.abs(dtr)))
    dt = jnp.where(head_lane, dt, 0.0)
    d_a = dt * (-jnp.exp(alog_ref[...]))
    rr = lax.broadcasted_iota(jnp.int32, (q, q), 0)
    cc = lax.broadcasted_iota(jnp.int32, (q, q), 1)
    tril = rr >= cc
    tri16 = jnp.where(tril, 1.0, 0.0).astype(BF16)
    cs = None
    for part in _split_bf16(d_a, 3):
        term = jnp.dot(tri16, part, preferred_element_type=F32)
        cs = term if cs is None else cs + term
    cs_t = cs.T
    ecs = jnp.exp(cs)
    dec = jnp.exp(cs[q - 1:q, :] - cs)
    e16 = e_ref[...]
    dt_e = _expand_heads(dt, e16)
    ecs_e = _expand_heads(ecs, e16)
    dec_e = _expand_heads(dec, e16)
    chunk_decay_e = ecs_e[q - 1:q, :]

    heads_per_group = n_heads // SSM_N_GROUPS
    for g in range(SSM_N_GROUPS):
        gs = slice(g * group_w, (g + 1) * group_w)
        ns = slice(g * SSM_D_STATE, (g + 1) * SSM_D_STATE)
        b_f = bm_ref[:, ns]
        b16 = b_f.astype(BF16)
        bt16 = b_f.T.astype(BF16)
        c16 = cm_ref[:, ns]
        cb = lax.dot_general(c16, b16, (((1,), (1,)), ((), ())), preferred_element_type=F32)
        xs_g = xs_ref[:, gs]
        xdt = xs_g * dt_e[:, gs]
        xdt16 = xdt.astype(BF16)
        y_parts = []
        for pr in range(heads_per_group // 2):
            h0 = g * heads_per_group + 2 * pr
            ms = []
            for hh in (h0, h0 + 1):
                diff = cs[:, hh:hh + 1] - cs_t[hh:hh + 1, :]
                ms.append(cb * jnp.exp(jnp.where(tril, diff, -jnp.inf)))
            m2 = jnp.concatenate(ms, axis=1).astype(BF16)
            x_stack = _stack_two_heads(xdt16[:, pr * LANES:(pr + 1) * LANES])
            y_parts.append(jnp.dot(m2, x_stack, preferred_element_type=F32))
        y_diag = jnp.concatenate(y_parts, axis=1)
        st_old = state_ref[:, gs]
        y_off = jnp.dot(c16, st_old.astype(BF16), preferred_element_type=F32) * ecs_e[:, gs]
        st_new = jnp.dot(bt16, (xdt * dec_e[:, gs]).astype(BF16), preferred_element_type=F32)
        state_ref[:, gs] = st_old * chunk_decay_e[:, gs] + st_new
        y = y_diag + y_off + dskip_ref[:, gs] * xs_g
        zf = z_ref[:, gs].astype(F32)
        gated = y * (zf * _sigmoid(zf))
        o_ref[:, gs] = (_rms(gated) * nw_ref[:, gs]).astype(o_ref.dtype)


def _ssd_mixer(zx, dt_raw, conv_w, conv_b, dt_bias, a_log, d_skip, norm_w, batch, seq):
    m = zx.shape[0]
    n_heads = a_log.shape[0]
    d_inner = n_heads * SSM_HEAD_DIM
    bc_dim = SSM_N_GROUPS * SSM_D_STATE
    conv_dim = d_inner + 2 * bc_dim
    assert zx.shape[1] == d_inner + conv_dim and seq % SSM_CHUNK == 0 and n_heads <= LANES
    assert d_inner % bc_dim == 0 and conv_dim % 512 == 0 and d_inner // SSM_N_GROUPS == 2 * LANES
    zx3 = zx.reshape(batch, seq, zx.shape[1])
    dt3 = dt_raw.reshape(batch, seq, LANES)
    pad_heads = lambda v: jnp.pad(v.astype(F32), (0, LANES - n_heads))[None, :]
    blk = lambda w: (None, SSM_CHUNK, w)
    const = lambda shape: pl.BlockSpec(shape, lambda b, c: (0,) * len(shape))
    r = d_inner // bc_dim
    out = pl.pallas_call(
        _ssd_kernel,
        out_shape=jax.ShapeDtypeStruct((batch, seq, d_inner), BF16),
        grid=(batch, seq // SSM_CHUNK),
        in_specs=[
            pl.BlockSpec(blk(d_inner), lambda b, c: (b, c, 0)),
            pl.BlockSpec(blk(d_inner), lambda b, c: (b, c, 1)),
            pl.BlockSpec(blk(bc_dim), lambda b, c: (b, c, 2 * r)),
            pl.BlockSpec(blk(bc_dim), lambda b, c: (b, c, 2 * r + 1)),
            pl.BlockSpec(blk(LANES), lambda b, c: (b, c, 0)),
            const((SSM_CONV, conv_dim)), const((1, conv_dim)), const((1, LANES)), const((1, LANES)),
            const((1, d_inner)), const((1, d_inner)), const((LANES, d_inner)),
        ],
        out_specs=pl.BlockSpec(blk(d_inner), lambda b, c: (b, c, 0)),
        scratch_shapes=[
            pltpu.VMEM((SSM_CHUNK + SUBLANES, conv_dim), F32),
            pltpu.VMEM((SSM_D_STATE, d_inner), F32),
            pltpu.VMEM((SSM_CHUNK, d_inner), F32),
            pltpu.VMEM((SSM_CHUNK, bc_dim), F32),
            pltpu.VMEM((SSM_CHUNK, bc_dim), BF16),
        ],
        compiler_params=pltpu.CompilerParams(dimension_semantics=("parallel", "arbitrary"),
                                             vmem_limit_bytes=_vmem_limit(40 * 1024 * 1024)),
        name="ssd_mixer",
    )(zx3, zx3, zx3, zx3, dt3, conv_w.astype(F32), conv_b.astype(F32)[None, :], pad_heads(dt_bias),
      pad_heads(a_log), jnp.repeat(d_skip.astype(F32), SSM_HEAD_DIM)[None, :], norm_w.astype(F32)[None, :],
      _head_expander(n_heads, SSM_HEAD_DIM))
    return out.reshape(m, d_inner)


def kernel(x, positions, norm_mix_w, norm_mlp_w, a_w_qkv, a_b_qkv, a_sinks, a_w_o, a_b_o, b_in_w, b_conv_w, b_conv_b, b_dt_bias, b_a_log, b_d, b_norm_w, b_out_w, c_w_qkv, c_w_o, mlp_w_up, mlp_w_down, final_norm_w):
    batch, seq, d_model = x.shape
    m = batch * seq
    depth = norm_mix_w.shape[0]
    h = x.reshape(m, d_model).astype(F32)
    rope = _rope_tables(positions.reshape(m, 1))

    for i in range(depth):
        kind, j = i % N_MIXERS, i // N_MIXERS
        if kind == 0:
            q_dim = a_sinks.shape[1] * HEAD_DIM
            kv_dim = (a_w_qkv.shape[2] - q_dim) // 2
            qkv = _linear(h, a_w_qkv[j].astype(BF16), gain=norm_mix_w[i], bias=a_b_qkv[j], rope=rope,
                          rope_cols=q_dim + kv_dim, name="swa_qkv")
            att = _swa_attention(qkv, a_sinks[j], batch, seq)
            h = _linear(att, a_w_o[j].astype(BF16), bias=a_b_o[j], residual=h, out_dtype=F32, name="swa_out")
        elif kind == 1:
            n_heads = b_a_log.shape[1]
            w_in = b_in_w[j]
            main = w_in.shape[1] - n_heads
            zx = _linear(h, w_in[:, :main].astype(BF16), gain=norm_mix_w[i], tn=1024, name="ssm_in")
            w_dt = jnp.pad(w_in[:, main:], ((0, 0), (0, LANES - n_heads))).astype(BF16)
            dt_raw = _linear(h, w_dt, gain=norm_mix_w[i], out_dtype=F32, name="ssm_dt")
            g = _ssd_mixer(zx, dt_raw, b_conv_w[j], b_conv_b[j], b_dt_bias[j], b_a_log[j], b_d[j], b_norm_w[j],
                           batch, seq)
            h = _linear(g, b_out_w[j].astype(BF16), residual=h, out_dtype=F32, name="ssm_out")
        else:
            width = c_w_qkv.shape[2]
            qkv = _linear(h, c_w_qkv[j].astype(BF16), gain=norm_mix_w[i], rope=rope,
                          rope_cols=2 * width // 3, out_dtype=F32, tn=1024, name="dilated_qkv")
            att = _dilated_attention(qkv, batch, seq)
            h = _linear(att, c_w_o[j].astype(BF16), residual=h, out_dtype=F32, name="dilated_out")
        final_gain = final_norm_w if i == depth - 1 else None
        h = _mlp(h, norm_mlp_w[i], mlp_w_up[i].astype(BF16), mlp_w_down[i].astype(BF16), final_gain)
    return h.reshape(batch, seq, d_model).astype(x.dtype)
```

```python
import functools
import math

import jax
import jax.numpy as jnp
from jax import lax
from jax.experimental import pallas as pl
from jax.experimental.pallas import tpu as pltpu

F32 = jnp.float32
BF16 = jnp.bfloat16

N_MIXERS = 3
ATTN_BLOCK = 128
HEAD_DIM = 64
ROPE_THETA = 10000.0
NORM_EPS = 1e-5
A_Q_PER_KV = 8
A_WINDOW = 128
C_PATTERNS = ((128, 1), (512, 4), (2048, 16))
SSM_HEAD_DIM = 64
SSM_N_GROUPS = 8
SSM_D_STATE = 128
SSM_CONV = 4
SSM_CHUNK = 128

LANES = 128
LOG2_E = math.log2(math.e)
SUBLANES = 8
BF16_ROWS = 16
V7X_VMEM_LIMIT_CAP = 56 * 1024 * 1024


def _vmem_limit(nbytes):
    return int(min(max(nbytes, 16 * 1024 * 1024), V7X_VMEM_LIMIT_CAP))


def _row_tile(m):
    for t in (1024, 512, 256, 128):
        if m % t == 0:
            return t
    raise ValueError(f"token count {m} must be a multiple of 128")


def _rms(xf):
    return xf * lax.rsqrt(jnp.mean(xf * xf, axis=-1, keepdims=True) + NORM_EPS)


def _sigmoid(v):
    return 1.0 / (1.0 + jnp.exp2(v * (-LOG2_E)))


def _split_bf16(v, n):
    parts, r = [], v
    for _ in range(n):
        p = r.astype(BF16)
        parts.append(p)
        r = r - p.astype(F32)
    return parts


def _dot_split(parts, rhs):
    acc = None
    for p in parts:
        d = jnp.dot(p, rhs, preferred_element_type=F32)
        acc = d if acc is None else acc + d
    return acc


def _expand_heads(v, e_bf16):
    return _dot_split(_split_bf16(v, 2), e_bf16)


def _head_expander(n_heads, head_dim):
    rows = jnp.arange(LANES)[:, None]
    cols = jnp.arange(n_heads * head_dim)[None, :] // head_dim
    return (rows == cols).astype(BF16)


def _rope_table_kernel(pos_ref, inv_ref, sign_ref, cos_ref, sin_ref):
    ang = pos_ref[...].astype(F32) * inv_ref[...]
    cos_ref[...] = jnp.cos(ang)
    sin_ref[...] = jnp.sin(ang) * sign_ref[...]


def _rope_tables(pos_col):
    m = pos_col.shape[0]
    tm = _row_tile(m)
    half = HEAD_DIM // 2
    inv = ROPE_THETA ** (-jnp.arange(half, dtype=F32) / half)
    inv_l = jnp.tile(inv, LANES // half)[None, :]
    sign_l = jnp.where(jnp.arange(LANES) < LANES // 2, -1.0, 1.0).astype(F32)[None, :]
    row = pl.BlockSpec((tm, LANES), lambda i: (i, 0))
    const = pl.BlockSpec((1, LANES), lambda i: (0, 0))
    return pl.pallas_call(
        _rope_table_kernel,
        out_shape=(jax.ShapeDtypeStruct((m, LANES), F32), jax.ShapeDtypeStruct((m, LANES), F32)),
        grid=(m // tm,),
        in_specs=[pl.BlockSpec((tm, 1), lambda i: (i, 0)), const, const],
        out_specs=(row, row),
        compiler_params=pltpu.CompilerParams(dimension_semantics=("parallel",)),
        name="rope_tables",
    )(pos_col, inv_l, sign_l)


def _pair_interleave(w, n_rope):
    lead = w.shape[:-1]
    head = w[..., :n_rope].reshape(*lead, n_rope // LANES, 4, LANES // 4)
    head = jnp.stack([head[..., 0, :], head[..., 2, :], head[..., 1, :], head[..., 3, :]], axis=-2)
    return jnp.concatenate([head.reshape(*lead, n_rope), w[..., n_rope:]], axis=-1)


def _rope_lanes(yc, cos, sin_signed):
    return yc * cos + pltpu.roll(yc, LANES // 2, 1) * sin_signed


MXU_COLS = 256


def _linear_kernel(*refs, has_gain, has_bias, has_res, rope_cols, n_cols, cache_x):
    refs = list(refs)
    x_ref, w_ref = refs.pop(0), refs.pop(0)
    g_ref = refs.pop(0) if has_gain else None
    b_ref = refs.pop(0) if has_bias else None
    cos_ref, sin_ref = (refs.pop(0), refs.pop(0)) if rope_cols else (None, None)
    r_ref = refs.pop(0) if has_res else None
    o_ref = refs.pop(0)
    xn_ref = refs.pop(0) if cache_x else None
    j = pl.program_id(1)
    tn = o_ref.shape[1]

    if cache_x:
        @pl.when(j == 0)
        def _():
            xf = x_ref[...].astype(F32)
            if has_gain:
                xf = _rms(xf) * g_ref[...]
            xn_ref[...] = xf.astype(BF16)
        xb = xn_ref[...]
    else:
        xb = x_ref[...]

    tiled_rope = rope_cols and tn < n_cols
    if rope_cols:
        cos, sin = cos_ref[...], sin_ref[...]
        if tiled_rope:
            is_rope = j < rope_cols // tn
            cos, sin = jnp.where(is_rope, cos, 1.0), jnp.where(is_rope, sin, 0.0)

    step = MXU_COLS if tn % MXU_COLS == 0 else LANES
    for c0 in range(0, tn, step):
        acc = jnp.dot(xb, w_ref[:, c0:c0 + step], preferred_element_type=F32)
        if has_bias:
            acc = acc + b_ref[:, c0:c0 + step]
        if has_res:
            acc = acc + r_ref[:, c0:c0 + step]
        for l0 in range(0, step, LANES):
            yc = acc[:, l0:l0 + LANES]
            if rope_cols and (tiled_rope or c0 + l0 < rope_cols):
                yc = _rope_lanes(yc, cos, sin)
            o_ref[:, c0 + l0:c0 + l0 + LANES] = yc.astype(o_ref.dtype)


def _linear(x, w, *, gain=None, bias=None, rope=None, rope_cols=0, residual=None,
            out_dtype=BF16, tn=None, name="linear"):
    m, k = x.shape
    n = w.shape[1]
    tm = _row_tile(m)
    tn = n if tn is None else tn
    assert n % tn == 0 and tn % LANES == 0
    if rope_cols and rope_cols % tn:
        assert tn == n and rope_cols % LANES == 0
    cache_x = gain is not None or x.dtype != BF16
    row = lambda i, j: (i, 0)
    col = lambda i, j: (0, j)
    args, specs = [x, w], [pl.BlockSpec((tm, k), row), pl.BlockSpec((k, tn), col)]
    if gain is not None:
        args.append(gain.reshape(1, k).astype(F32))
        specs.append(pl.BlockSpec((1, k), lambda i, j: (0, 0)))
    if bias is not None:
        args.append(bias.reshape(1, n).astype(F32))
        specs.append(pl.BlockSpec((1, tn), col))
    if rope_cols:
        args += list(rope)
        specs += [pl.BlockSpec((tm, LANES), row)] * 2
    if residual is not None:
        args.append(residual)
        specs.append(pl.BlockSpec((tm, tn), lambda i, j: (i, j)))
    out_bytes = jnp.dtype(out_dtype).itemsize
    est = (2 * tm * k * x.dtype.itemsize + 2 * k * tn * 2 + 2 * tm * tn * out_bytes
           + (2 * tm * tn * 4 if residual is not None else 0) + tm * k * 2 + 3 * tm * tn * 4
           + 4 * tm * LANES * 4)
    return pl.pallas_call(
        functools.partial(_linear_kernel, has_gain=gain is not None, has_bias=bias is not None,
                          has_res=residual is not None, rope_cols=rope_cols, n_cols=n, cache_x=cache_x),
        out_shape=jax.ShapeDtypeStruct((m, n), out_dtype),
        grid=(m // tm, n // tn),
        in_specs=specs,
        out_specs=pl.BlockSpec((tm, tn), lambda i, j: (i, j)),
        scratch_shapes=[pltpu.VMEM((tm, k), BF16)] if cache_x else [],
        compiler_params=pltpu.CompilerParams(dimension_semantics=("parallel", "arbitrary"),
                                             vmem_limit_bytes=_vmem_limit(est)),
        name=name,
    )(*args)


MLP_ROWS = 512
MLP_FF_CHUNK = 512


def _mlp_kernel(*refs, has_final):
    refs = list(refs)
    x_ref, g_ref, wu_ref, wd_ref = refs[:4]
    gf_ref = refs[4] if has_final else None
    o_ref = refs[-1]
    x = x_ref[...]
    xn = (_rms(x) * g_ref[...]).astype(BF16)
    y = x
    ff = wu_ref.shape[1]
    step = MLP_FF_CHUNK if ff % MLP_FF_CHUNK == 0 else ff
    for c0 in range(0, ff, step):
        u = jnp.maximum(jnp.dot(xn, wu_ref[:, c0:c0 + step], preferred_element_type=F32), 0.0)
        y = y + jnp.dot((u * u).astype(BF16), wd_ref[c0:c0 + step, :], preferred_element_type=F32)
    if has_final:
        y = _rms(y) * gf_ref[...]
    o_ref[...] = y


def _mlp(h, gain, w_up, w_down, final_gain=None):
    m, d = h.shape
    ff = w_up.shape[1]
    tm = MLP_ROWS if m % MLP_ROWS == 0 else _row_tile(m)
    row = lambda i: (i, 0)
    const = lambda i: (0, 0)
    vec = pl.BlockSpec((1, d), const)
    resident = lambda shape: pl.BlockSpec(shape, const, pipeline_mode=pl.Buffered(1))
    args = [h, gain.reshape(1, d).astype(F32), w_up, w_down]
    specs = [pl.BlockSpec((tm, d), row), vec, resident((d, ff)), resident((ff, d))]
    if final_gain is not None:
        args.append(final_gain.reshape(1, d).astype(F32))
        specs.append(vec)
    est = 4 * d * ff + 4 * tm * d * 4 + 3 * tm * d * 4 + 3 * tm * MLP_FF_CHUNK * 4
    return pl.pallas_call(
        functools.partial(_mlp_kernel, has_final=final_gain is not None),
        out_shape=jax.ShapeDtypeStruct((m, d), F32),
        grid=(m // tm,),
        in_specs=specs,
        out_specs=pl.BlockSpec((tm, d), row),
        compiler_params=pltpu.CompilerParams(dimension_semantics=("parallel",),
                                             vmem_limit_bytes=_vmem_limit(est)),
        name="sqrelu_mlp",
    )(*args)


Q_SCALE = HEAD_DIM ** -0.5 * LOG2_E
ATTN_PIPE_DEPTH = 2


def _low_head_lanes(shape, interleaved):
    lane = lax.broadcasted_iota(jnp.int32, shape, 1)
    return (lane % HEAD_DIM) < HEAD_DIM // 2 if interleaved else lane < HEAD_DIM


def _stack_two_heads(t, interleaved=False):
    low = _low_head_lanes(t.shape, interleaved)
    zero = jnp.zeros_like(t)
    return jnp.concatenate([jnp.where(low, t, zero), jnp.where(low, zero, t)], axis=0)


def _init_band_consts(bias_ref, ones_ref, max_dist):
    r = lax.broadcasted_iota(jnp.int32, (ATTN_BLOCK, 2 * ATTN_BLOCK), 0)
    c = lax.broadcasted_iota(jnp.int32, (ATTN_BLOCK, 2 * ATTN_BLOCK), 1)
    dist = r + ATTN_BLOCK - c
    ok = (dist >= 0) & (dist <= max_dist)
    bias_ref[0] = jnp.where(ok & (c >= ATTN_BLOCK), 0.0, -jnp.inf)
    bias_ref[1] = jnp.where(ok, 0.0, -jnp.inf)
    low = _low_head_lanes((2 * ATTN_BLOCK, LANES), False)
    ones_ref[0:2 * ATTN_BLOCK, :] = jnp.where(low, 1.0, 0.0).astype(BF16)
    ones_ref[2 * ATTN_BLOCK:, :] = jnp.where(low, 0.0, 1.0).astype(BF16)


def _scores(q2, kk):
    return lax.dot_general(q2, _stack_two_heads(kk, interleaved=True), (((1,), (1,)), ((), ())),
                           preferred_element_type=F32)


def _probs(s, bias, sinks):
    kw = 2 * ATTN_BLOCK
    s0, s1 = s[:, :kw] + bias, s[:, kw:] + bias
    m0 = jnp.max(s0, axis=-1, keepdims=True)
    m1 = jnp.max(s1, axis=-1, keepdims=True)
    if sinks is not None:
        m0, m1 = jnp.maximum(m0, sinks[0]), jnp.maximum(m1, sinks[1])
    p2 = jnp.concatenate([jnp.exp2(s0 - m0), jnp.exp2(s1 - m1)], axis=1).astype(BF16)
    return p2, jnp.where(_low_head_lanes((s.shape[0], LANES), False), m0, m1)


def _weighted(p2, m_l, vv, ones_stack, sinks):
    od = jnp.dot(p2, jnp.concatenate([_stack_two_heads(vv), ones_stack], axis=1), preferred_element_type=F32)
    o, den = od[:, :LANES], od[:, LANES:]
    if sinks is not None:
        den = den + jnp.exp2(jnp.where(_low_head_lanes(o.shape, False), sinks[0], sinks[1]) - m_l)
    return o / den, m_l + jnp.log2(den)


def _run_pipelined(n, first_stage, mid_stage, last_stage):
    ahead = {i: first_stage(i) for i in range(min(ATTN_PIPE_DEPTH, n))}
    for i in range(n):
        mid = mid_stage(i, ahead.pop(i))
        if i + ATTN_PIPE_DEPTH < n:
            ahead[i + ATTN_PIPE_DEPTH] = first_stage(i + ATTN_PIPE_DEPTH)
        last_stage(i, mid)


def _swa_kernel(sink_ref, q_ref, k_ref, v_ref, o_ref, kb_ref, vb_ref, bias_ref, ones_ref):
    seq = q_ref.shape[0]
    p = pl.program_id(1)
    pairs_per_kv = A_Q_PER_KV // 2
    rows = 2 * ATTN_BLOCK

    @pl.when(p == 0)
    def _():
        _init_band_consts(bias_ref, ones_ref, A_WINDOW - 1)

    def spread_kv_head(kv):
        k_own = _low_head_lanes((rows, LANES), True)
        v_own = _low_head_lanes((rows, LANES), False)
        if kv == 1:
            k_own, v_own = jnp.logical_not(k_own), jnp.logical_not(v_own)
        k_shift = HEAD_DIM // 2 if kv == 0 else LANES - HEAD_DIM // 2
        for c in range(seq // rows):
            sl = slice(c * rows, (c + 1) * rows)
            kt, vt = k_ref[sl, :].astype(F32), v_ref[sl, :].astype(F32)
            kb_ref[sl, :] = jnp.where(k_own, kt, pltpu.roll(kt, k_shift, 1)).astype(BF16)
            vb_ref[sl, :] = jnp.where(v_own, vt, pltpu.roll(vt, HEAD_DIM, 1)).astype(BF16)

    for kv in range(2):
        pl.when(p == kv * pairs_per_kv)(functools.partial(spread_kv_head, kv))

    sinks = (sink_ref[2 * p] * LOG2_E, sink_ref[2 * p + 1] * LOG2_E)
    blk = lambda i: slice(i * ATTN_BLOCK, (i + 1) * ATTN_BLOCK)

    def keys(ref, i):
        if i == 0:
            return jnp.concatenate([ref[blk(0), :], ref[blk(0), :]], axis=0)
        return ref[(i - 1) * ATTN_BLOCK:(i + 1) * ATTN_BLOCK, :]

    def score_stage(i):
        return _scores((q_ref[blk(i), :].astype(F32) * Q_SCALE).astype(BF16), keys(kb_ref, i))

    def prob_stage(i, s):
        return _probs(s, bias_ref[min(i, 1)], sinks)

    def out_stage(i, mid):
        o, _ = _weighted(*mid, keys(vb_ref, i), ones_ref[...], sinks)
        o_ref[blk(i), :] = o.astype(o_ref.dtype)

    _run_pipelined(seq // ATTN_BLOCK, score_stage, prob_stage, out_stage)


def _swa_attention(qkv, sinks, batch, seq):
    m = qkv.shape[0]
    q_dim = sinks.shape[0] * HEAD_DIM
    assert q_dim // HEAD_DIM // A_Q_PER_KV == 2 and seq % (2 * ATTN_BLOCK) == 0
    q3 = qkv.reshape(batch, seq, qkv.shape[1])
    kcol, vcol = q_dim // LANES, q_dim // LANES + 1
    slab = (None, seq, LANES)
    out = pl.pallas_call(
        _swa_kernel,
        out_shape=jax.ShapeDtypeStruct((batch, seq, q_dim), BF16),
        grid=(batch, q_dim // LANES),
        in_specs=[
            pl.BlockSpec(memory_space=pltpu.SMEM),
            pl.BlockSpec(slab, lambda b, p: (b, 0, p)),
            pl.BlockSpec(slab, lambda b, p: (b, 0, kcol)),
            pl.BlockSpec(slab, lambda b, p: (b, 0, vcol)),
        ],
        out_specs=pl.BlockSpec(slab, lambda b, p: (b, 0, p)),
        scratch_shapes=[pltpu.VMEM((seq, LANES), BF16), pltpu.VMEM((seq, LANES), BF16),
                        pltpu.VMEM((2, ATTN_BLOCK, 2 * ATTN_BLOCK), F32), pltpu.VMEM((4 * ATTN_BLOCK, LANES), BF16)],
        compiler_params=pltpu.CompilerParams(dimension_semantics=("parallel", "arbitrary"),
                                             vmem_limit_bytes=_vmem_limit(16 * seq * LANES * 2)),
        name="swa_sink_attention",
    )(sinks.astype(F32), q3, q3, q3)
    return out.reshape(m, q_dim)


def _dilated_kernel(*refs):
    n_groups = len(C_PATTERNS)
    in_refs, o_ref = refs[:3 * n_groups], refs[3 * n_groups]
    acc_o, acc_l, bias_ref, ones_ref = refs[3 * n_groups + 1:]
    seq = o_ref.shape[0]
    max_dist = C_PATTERNS[0][0] // C_PATTERNS[0][1]
    assert all(w // d == max_dist for w, d in C_PATTERNS)
    _init_band_consts(bias_ref, ones_ref, max_dist)

    blocks = []
    for g, (window, dil) in enumerate(C_PATTERNS):
        for r in range(dil):
            for i in range(seq // dil // ATTN_BLOCK):
                blocks.append((g, dil, r, i))

    def rows(dil, r, i):
        start = i * ATTN_BLOCK * dil + r
        return pl.ds(start, ATTN_BLOCK, stride=dil) if dil > 1 else pl.ds(start, ATTN_BLOCK)

    def keys(ref, dil, r, i):
        prev = ref[rows(dil, r, max(i - 1, 0)), :]
        return jnp.concatenate([prev, ref[rows(dil, r, i), :]], axis=0).astype(BF16)

    def score_stage(n):
        g, dil, r, i = blocks[n]
        q2 = (in_refs[3 * g][rows(dil, r, i), :] * Q_SCALE).astype(BF16)
        return _scores(q2, keys(in_refs[3 * g + 1], dil, r, i))

    def prob_stage(n, s):
        return _probs(s, bias_ref[min(blocks[n][3], 1)], None)

    def out_stage(n, mid):
        g, dil, r, i = blocks[n]
        cur = rows(dil, r, i)
        o, lse = _weighted(*mid, keys(in_refs[3 * g + 2], dil, r, i), ones_ref[...], None)
        if g > 0:
            o_old, l_old = acc_o[cur, :], acc_l[cur, :]
            l_max = jnp.maximum(l_old, lse)
            e_old, e_new = jnp.exp2(l_old - l_max), jnp.exp2(lse - l_max)
            inv = 1.0 / (e_old + e_new)
            o = o_old * (e_old * inv) + o * (e_new * inv)
            lse = l_max + jnp.log2(e_old + e_new)
        acc_o[cur, :] = o
        if g < n_groups - 1:
            acc_l[cur, :] = lse

    _run_pipelined(len(blocks), score_stage, prob_stage, out_stage)
    o_ref[...] = acc_o[...].astype(o_ref.dtype)


def _dilated_attention(qkv, batch, seq):
    m, width = qkv.shape
    n_groups = len(C_PATTERNS)
    gw = width // (3 * n_groups)
    assert gw % LANES == 0 and all(seq % (d * ATTN_BLOCK) == 0 for _, d in C_PATTERNS)
    pairs = gw // LANES
    q3 = qkv.reshape(batch, seq, width)
    slab = lambda c: pl.BlockSpec((None, seq, LANES), lambda b, p: (b, 0, c * pairs + p))
    specs = []
    for g in range(n_groups):
        specs += [slab(g), slab(n_groups + g), slab(2 * n_groups + g)]
    out = pl.pallas_call(
        _dilated_kernel,
        out_shape=jax.ShapeDtypeStruct((batch, seq, gw), BF16),
        grid=(batch, pairs),
        in_specs=specs,
        out_specs=pl.BlockSpec((None, seq, LANES), lambda b, p: (b, 0, p)),
        scratch_shapes=[pltpu.VMEM((seq, LANES), F32), pltpu.VMEM((seq, LANES), F32),
                        pltpu.VMEM((2, ATTN_BLOCK, 2 * ATTN_BLOCK), F32), pltpu.VMEM((4 * ATTN_BLOCK, LANES), BF16)],
        compiler_params=pltpu.CompilerParams(
            dimension_semantics=("parallel", "parallel"),
            vmem_limit_bytes=_vmem_limit(2 * (3 * n_groups + 2) * seq * LANES * 4 + 8 * 1024 * 1024)),
        name="dilated_attention",
    )(*([q3] * (3 * n_groups)))
    return out.reshape(m, gw)


def _ssd_kernel(z_ref, x_ref, b_ref, c_ref, dt_ref, cw_ref, cb_ref, dtb_ref, alog_ref, dskip_ref, nw_ref, e_ref,
                o_ref, xpad_ref, state_ref, xs_ref, bm_ref, cm_ref):
    q = SSM_CHUNK
    d_inner = x_ref.shape[1]
    bc_dim = b_ref.shape[1]
    n_heads = d_inner // SSM_HEAD_DIM
    group_w = d_inner // SSM_N_GROUPS
    pad = BF16_ROWS

    @pl.when(pl.program_id(1) == 0)
    def _():
        xpad_ref[0:pad, :] = jnp.zeros((pad, xpad_ref.shape[1]), BF16)
        state_ref[...] = jnp.zeros_like(state_ref)

    xpad_ref[pad:pad + q, 0:d_inner] = x_ref[...]
    xpad_ref[pad:pad + q, d_inner:d_inner + bc_dim] = b_ref[...]
    xpad_ref[pad:pad + q, d_inner + bc_dim:] = c_ref[...]
    out_row = lax.broadcasted_iota(jnp.int32, (q, pad + q), 0)
    src_row = lax.broadcasted_iota(jnp.int32, (q, pad + q), 1)
    shifters = [jnp.where(src_row == pad + out_row - k, 1.0, 0.0).astype(BF16) for k in range(1, SSM_CONV)]
    conv_tile = MXU_COLS
    for t in range(xpad_ref.shape[1] // conv_tile):
        sl = slice(t * conv_tile, (t + 1) * conv_tile)
        acc = cb_ref[:, sl] + cw_ref[SSM_CONV - 1:SSM_CONV, sl] * xpad_ref[pad:pad + q, sl].astype(F32)
        for k in range(1, SSM_CONV):
            shifted = jnp.dot(shifters[k - 1], xpad_ref[:, sl], preferred_element_type=F32)
            acc = acc + cw_ref[SSM_CONV - 1 - k:SSM_CONV - k, sl] * shifted
        y = acc * _sigmoid(acc)
        lo = t * conv_tile
        if lo < d_inner:
            xs_ref[:, lo:lo + conv_tile] = y
        elif lo < d_inner + bc_dim:
            bm_ref[:, lo - d_inner:lo - d_inner + conv_tile] = y
        else:
            cm_ref[:, lo - d_inner - bc_dim:lo - d_inner - bc_dim + conv_tile] = y.astype(BF16)
    xpad_ref[0:pad, :] = xpad_ref[q:q + pad, :]

    head_lane = lax.broadcasted_iota(jnp.int32, (q, LANES), 1) < n_heads
    dtr = dt_ref[...] + dtb_ref[...]
    dt = jnp.maximum(dtr, 0.0) + jnp.log(1.0 + jnp.exp(-jnp.abs(dtr)))
    dt = jnp.where(head_lane, dt, 0.0)
    d_a = dt * (-jnp.exp(alog_ref[...]))
    rr = lax.broadcasted_iota(jnp.int32, (q, q), 0)
    cc = lax.broadcasted_iota(jnp.int32, (q, q), 1)
    tril = rr >= cc
    tri16 = jnp.where(tril, 1.0, 0.0).astype(BF16)
    cs = None
    for part in _split_bf16(d_a, 3):
        term = jnp.dot(tri16, part, preferred_element_type=F32)
        cs = term if cs is None else cs + term
    cs2 = cs * LOG2_E
    cs2_t = cs2.T
    ecs = jnp.exp(cs)
    dec = jnp.exp(cs[q - 1:q, :] - cs)
    e16 = e_ref[...]
    dt_e = _expand_heads(dt, e16)
    ecs_e = _expand_heads(ecs, e16)
    dec_e = _expand_heads(dec, e16)
    chunk_decay_e = ecs_e[q - 1:q, :]

    heads_per_group = n_heads // SSM_N_GROUPS
    lanes_of = lambda g: slice(g * group_w, (g + 1) * group_w)

    def state_stage(g):
        gs, ns = lanes_of(g), slice(g * SSM_D_STATE, (g + 1) * SSM_D_STATE)
        b_f = bm_ref[:, ns]
        c16 = cm_ref[:, ns]
        cb = lax.dot_general(c16, b_f.astype(BF16), (((1,), (1,)), ((), ())), preferred_element_type=F32)
        xdt = xs_ref[:, gs] * dt_e[:, gs]
        st_old = state_ref[:, gs]
        y_off = jnp.dot(c16, st_old.astype(BF16), preferred_element_type=F32) * ecs_e[:, gs]
        st_new = jnp.dot(b_f.T.astype(BF16), (xdt * dec_e[:, gs]).astype(BF16), preferred_element_type=F32)
        state_ref[:, gs] = st_old * chunk_decay_e[:, gs] + st_new
        return cb, y_off, xdt.astype(BF16)

    def diag_stage(g, carried):
        cb, y_off, xdt16 = carried
        y_parts = []
        for pr in range(heads_per_group // 2):
            h0 = g * heads_per_group + 2 * pr
            ms = []
            for hh in (h0, h0 + 1):
                diff = cs2[:, hh:hh + 1] - cs2_t[hh:hh + 1, :]
                ms.append(cb * jnp.exp2(jnp.where(tril, diff, -jnp.inf)))
            m2 = jnp.concatenate(ms, axis=1).astype(BF16)
            x_stack = _stack_two_heads(xdt16[:, pr * LANES:(pr + 1) * LANES])
            y_parts.append(jnp.dot(m2, x_stack, preferred_element_type=F32))
        return jnp.concatenate(y_parts, axis=1) + y_off

    def gate_stage(g, y):
        gs = lanes_of(g)
        y = y + dskip_ref[:, gs] * xs_ref[:, gs]
        zf = z_ref[:, gs].astype(F32)
        gated = y * (zf * _sigmoid(zf))
        o_ref[:, gs] = (_rms(gated) * nw_ref[:, gs]).astype(o_ref.dtype)

    _run_pipelined(SSM_N_GROUPS, state_stage, diag_stage, gate_stage)


def _ssd_mixer(zx, dt_raw, conv_w, conv_b, dt_bias, a_log, d_skip, norm_w, batch, seq):
    m = zx.shape[0]
    n_heads = a_log.shape[0]
    d_inner = n_heads * SSM_HEAD_DIM
    bc_dim = SSM_N_GROUPS * SSM_D_STATE
    conv_dim = d_inner + 2 * bc_dim
    assert zx.shape[1] == d_inner + conv_dim and seq % SSM_CHUNK == 0 and n_heads <= LANES
    assert d_inner % bc_dim == 0 and conv_dim % 512 == 0 and d_inner // SSM_N_GROUPS == 2 * LANES
    zx3 = zx.reshape(batch, seq, zx.shape[1])
    dt3 = dt_raw.reshape(batch, seq, LANES)
    pad_heads = lambda v: jnp.pad(v.astype(F32), (0, LANES - n_heads))[None, :]
    blk = lambda w: (None, SSM_CHUNK, w)
    const = lambda shape: pl.BlockSpec(shape, lambda b, c: (0,) * len(shape))
    r = d_inner // bc_dim
    out = pl.pallas_call(
        _ssd_kernel,
        out_shape=jax.ShapeDtypeStruct((batch, seq, d_inner), BF16),
        grid=(batch, seq // SSM_CHUNK),
        in_specs=[
            pl.BlockSpec(blk(d_inner), lambda b, c: (b, c, 0)),
            pl.BlockSpec(blk(d_inner), lambda b, c: (b, c, 1)),
            pl.BlockSpec(blk(bc_dim), lambda b, c: (b, c, 2 * r)),
            pl.BlockSpec(blk(bc_dim), lambda b, c: (b, c, 2 * r + 1)),
            pl.BlockSpec(blk(LANES), lambda b, c: (b, c, 0)),
            const((SSM_CONV, conv_dim)), const((1, conv_dim)), const((1, LANES)), const((1, LANES)),
            const((1, d_inner)), const((1, d_inner)), const((LANES, d_inner)),
        ],
        out_specs=pl.BlockSpec(blk(d_inner), lambda b, c: (b, c, 0)),
        scratch_shapes=[
            pltpu.VMEM((SSM_CHUNK + BF16_ROWS, conv_dim), BF16),
            pltpu.VMEM((SSM_D_STATE, d_inner), F32),
            pltpu.VMEM((SSM_CHUNK, d_inner), F32),
            pltpu.VMEM((SSM_CHUNK, bc_dim), F32),
            pltpu.VMEM((SSM_CHUNK, bc_dim), BF16),
        ],
        compiler_params=pltpu.CompilerParams(dimension_semantics=("parallel", "arbitrary"),
                                             vmem_limit_bytes=_vmem_limit(40 * 1024 * 1024)),
        name="ssd_mixer",
    )(zx3, zx3, zx3, zx3, dt3, conv_w.astype(F32), conv_b.astype(F32)[None, :], pad_heads(dt_bias),
      pad_heads(a_log), jnp.repeat(d_skip.astype(F32), SSM_HEAD_DIM)[None, :], norm_w.astype(F32)[None, :],
      _head_expander(n_heads, SSM_HEAD_DIM))
    return out.reshape(m, d_inner)


def kernel(x, positions, norm_mix_w, norm_mlp_w, a_w_qkv, a_b_qkv, a_sinks, a_w_o, a_b_o, b_in_w, b_conv_w, b_conv_b, b_dt_bias, b_a_log, b_d, b_norm_w, b_out_w, c_w_qkv, c_w_o, mlp_w_up, mlp_w_down, final_norm_w):
    batch, seq, d_model = x.shape
    m = batch * seq
    depth = norm_mix_w.shape[0]
    h = x.reshape(m, d_model).astype(F32)
    rope = _rope_tables(positions.reshape(m, 1))

    for i in range(depth):
        kind, j = i % N_MIXERS, i // N_MIXERS
        if kind == 0:
            q_dim = a_sinks.shape[1] * HEAD_DIM
            kv_dim = (a_w_qkv.shape[2] - q_dim) // 2
            n_rope = q_dim + kv_dim
            qkv = _linear(h, _pair_interleave(a_w_qkv[j], n_rope).astype(BF16), gain=norm_mix_w[i],
                          bias=_pair_interleave(a_b_qkv[j], n_rope), rope=rope, rope_cols=n_rope, name="swa_qkv")
            att = _swa_attention(qkv, a_sinks[j], batch, seq)
            h = _linear(att, a_w_o[j].astype(BF16), bias=a_b_o[j], residual=h, out_dtype=F32, name="swa_out")
        elif kind == 1:
            n_heads = b_a_log.shape[1]
            w_in = b_in_w[j]
            main = w_in.shape[1] - n_heads
            zx = _linear(h, w_in[:, :main].astype(BF16), gain=norm_mix_w[i], tn=1024, name="ssm_in")
            w_dt = jnp.pad(w_in[:, main:], ((0, 0), (0, LANES - n_heads))).astype(BF16)
            dt_raw = _linear(h, w_dt, gain=norm_mix_w[i], out_dtype=F32, name="ssm_dt")
            g = _ssd_mixer(zx, dt_raw, b_conv_w[j], b_conv_b[j], b_dt_bias[j], b_a_log[j], b_d[j], b_norm_w[j],
                           batch, seq)
            h = _linear(g, b_out_w[j].astype(BF16), residual=h, out_dtype=F32, name="ssm_out")
        else:
            width = c_w_qkv.shape[2]
            n_rope = 2 * width // 3
            qkv = _linear(h, _pair_interleave(c_w_qkv[j], n_rope).astype(BF16), gain=norm_mix_w[i], rope=rope,
                          rope_cols=n_rope, out_dtype=F32, tn=1024, name="dilated_qkv")
            att = _dilated_attention(qkv, batch, seq)
            h = _linear(att, c_w_o[j].astype(BF16), residual=h, out_dtype=F32, name="dilated_out")
        final_gain = final_norm_w if i == depth - 1 else None
        h = _mlp(h, norm_mlp_w[i], mlp_w_up[i].astype(BF16), mlp_w_down[i].astype(BF16), final_gain)
    return h.reshape(batch, seq, d_model).astype(x.dtype)
```

```python
import functools
import math

import jax
import jax.numpy as jnp
from jax import lax
from jax.experimental import pallas as pl
from jax.experimental.pallas import tpu as pltpu

F32 = jnp.float32
BF16 = jnp.bfloat16

N_MIXERS = 3
ATTN_BLOCK = 128
HEAD_DIM = 64
ROPE_THETA = 10000.0
NORM_EPS = 1e-5
A_Q_PER_KV = 8
A_WINDOW = 128
C_PATTERNS = ((128, 1), (512, 4), (2048, 16))
SSM_HEAD_DIM = 64
SSM_N_GROUPS = 8
SSM_D_STATE = 128
SSM_CONV = 4
SSM_CHUNK = 128

LANES = 128
LOG2_E = math.log2(math.e)
SUBLANES = 8
BF16_ROWS = 16
V7X_VMEM_LIMIT_CAP = 56 * 1024 * 1024


def _vmem_limit(nbytes):
    return int(min(max(nbytes, 16 * 1024 * 1024), V7X_VMEM_LIMIT_CAP))


def _row_tile(m):
    for t in (1024, 512, 256, 128):
        if m % t == 0:
            return t
    raise ValueError(f"token count {m} must be a multiple of 128")


def _rms(xf):
    return xf * lax.rsqrt(jnp.mean(xf * xf, axis=-1, keepdims=True) + NORM_EPS)


def _sigmoid(v):
    return 1.0 / (1.0 + jnp.exp2(v * (-LOG2_E)))


def _split_bf16(v, n):
    parts, r = [], v
    for _ in range(n):
        p = r.astype(BF16)
        parts.append(p)
        r = r - p.astype(F32)
    return parts


def _dot_split(parts, rhs):
    acc = None
    for p in parts:
        d = jnp.dot(p, rhs, preferred_element_type=F32)
        acc = d if acc is None else acc + d
    return acc


def _expand_heads(v, e_bf16):
    return _dot_split(_split_bf16(v, 2), e_bf16)


def _head_expander(n_heads, head_dim):
    rows = jnp.arange(LANES)[:, None]
    cols = jnp.arange(n_heads * head_dim)[None, :] // head_dim
    return (rows == cols).astype(BF16)


def _rope_table_kernel(pos_ref, inv_ref, sign_ref, cos_ref, sin_ref):
    ang = pos_ref[...].astype(F32) * inv_ref[...]
    cos_ref[...] = jnp.cos(ang)
    sin_ref[...] = jnp.sin(ang) * sign_ref[...]


def _rope_tables(pos_col):
    m = pos_col.shape[0]
    tm = _row_tile(m)
    half = HEAD_DIM // 2
    inv = ROPE_THETA ** (-jnp.arange(half, dtype=F32) / half)
    inv_l = jnp.tile(inv, LANES // half)[None, :]
    sign_l = jnp.where(jnp.arange(LANES) < LANES // 2, -1.0, 1.0).astype(F32)[None, :]
    row = pl.BlockSpec((tm, LANES), lambda i: (i, 0))
    const = pl.BlockSpec((1, LANES), lambda i: (0, 0))
    return pl.pallas_call(
        _rope_table_kernel,
        out_shape=(jax.ShapeDtypeStruct((m, LANES), F32), jax.ShapeDtypeStruct((m, LANES), F32)),
        grid=(m // tm,),
        in_specs=[pl.BlockSpec((tm, 1), lambda i: (i, 0)), const, const],
        out_specs=(row, row),
        compiler_params=pltpu.CompilerParams(dimension_semantics=("parallel",)),
        name="rope_tables",
    )(pos_col, inv_l, sign_l)


def _pair_interleave(w, n_rope):
    lead = w.shape[:-1]
    head = w[..., :n_rope].reshape(*lead, n_rope // LANES, 4, LANES // 4)
    head = jnp.stack([head[..., 0, :], head[..., 2, :], head[..., 1, :], head[..., 3, :]], axis=-2)
    return jnp.concatenate([head.reshape(*lead, n_rope), w[..., n_rope:]], axis=-1)


def _rope_lanes(yc, cos, sin_signed):
    return yc * cos + pltpu.roll(yc, LANES // 2, 1) * sin_signed


MXU_COLS = 256


def _linear_kernel(*refs, has_gain, has_bias, has_res, has_side, rope_cols, n_cols, prep_x):
    refs = list(refs)
    x_ref, w_ref = refs.pop(0), refs.pop(0)
    g_ref = refs.pop(0) if has_gain else None
    b_ref = refs.pop(0) if has_bias else None
    cos_ref, sin_ref = (refs.pop(0), refs.pop(0)) if rope_cols else (None, None)
    r_ref = refs.pop(0) if has_res else None
    ws_ref = refs.pop(0) if has_side else None
    o_ref = refs.pop(0)
    side_ref = refs.pop(0) if has_side else None
    j = pl.program_id(1)
    tn = o_ref.shape[1]

    def prepared():
        xf = x_ref[...].astype(F32)
        if has_gain:
            xf = _rms(xf) * g_ref[...]
        return xf.astype(BF16)

    if not prep_x:
        xb = x_ref[...]
    elif tn == n_cols:
        xb = prepared()
    else:
        xn_ref = refs.pop(0)

        @pl.when(j == 0)
        def _():
            xn_ref[...] = prepared()
            if has_side:
                side_ref[...] = jnp.dot(xn_ref[...], ws_ref[...], preferred_element_type=F32)
        xb = xn_ref[...]
    if has_side and (not prep_x or tn == n_cols):
        side_ref[...] = jnp.dot(xb, ws_ref[...], preferred_element_type=F32)

    tiled_rope = rope_cols and tn < n_cols
    if rope_cols:
        cos, sin = cos_ref[...], sin_ref[...]
        if tiled_rope:
            is_rope = j < rope_cols // tn
            cos, sin = jnp.where(is_rope, cos, 1.0), jnp.where(is_rope, sin, 0.0)

    step = MXU_COLS if tn % MXU_COLS == 0 else LANES
    for c0 in range(0, tn, step):
        acc = jnp.dot(xb, w_ref[:, c0:c0 + step], preferred_element_type=F32)
        if has_bias:
            acc = acc + b_ref[:, c0:c0 + step]
        if has_res:
            acc = acc + r_ref[:, c0:c0 + step]
        for l0 in range(0, step, LANES):
            yc = acc[:, l0:l0 + LANES]
            if rope_cols and (tiled_rope or c0 + l0 < rope_cols):
                yc = _rope_lanes(yc, cos, sin)
            o_ref[:, c0 + l0:c0 + l0 + LANES] = yc.astype(o_ref.dtype)


def _linear(x, w, *, gain=None, bias=None, rope=None, rope_cols=0, residual=None, side_w=None,
            out_dtype=BF16, tn=None, name="linear"):
    m, k = x.shape
    n = w.shape[1]
    tm = _row_tile(m)
    tn = n if tn is None else tn
    assert n % tn == 0 and tn % LANES == 0
    if rope_cols and rope_cols % tn:
        assert tn == n and rope_cols % LANES == 0
    prep_x = gain is not None or x.dtype != BF16
    row = lambda i, j: (i, 0)
    col = lambda i, j: (0, j)
    const = lambda i, j: (0, 0)
    args, specs = [x, w], [pl.BlockSpec((tm, k), row), pl.BlockSpec((k, tn), col)]
    if gain is not None:
        args.append(gain.reshape(1, k).astype(F32))
        specs.append(pl.BlockSpec((1, k), const))
    if bias is not None:
        args.append(bias.reshape(1, n).astype(F32))
        specs.append(pl.BlockSpec((1, tn), col))
    if rope_cols:
        args += list(rope)
        specs += [pl.BlockSpec((tm, LANES), row)] * 2
    if residual is not None:
        args.append(residual)
        specs.append(pl.BlockSpec((tm, tn), lambda i, j: (i, j)))
    out_shape = jax.ShapeDtypeStruct((m, n), out_dtype)
    out_specs = pl.BlockSpec((tm, tn), lambda i, j: (i, j))
    if side_w is not None:
        args.append(side_w)
        specs.append(pl.BlockSpec((k, LANES), const))
        out_shape = (out_shape, jax.ShapeDtypeStruct((m, LANES), F32))
        out_specs = (out_specs, pl.BlockSpec((tm, LANES), row))
    out_bytes = jnp.dtype(out_dtype).itemsize
    est = (2 * tm * k * x.dtype.itemsize + 2 * k * tn * 2 + 2 * tm * tn * out_bytes
           + (2 * tm * tn * 4 if residual is not None else 0) + tm * k * 2 + 2 * tm * k * 4
           + 4 * tm * MXU_COLS * 4 + 6 * tm * LANES * 4)
    return pl.pallas_call(
        functools.partial(_linear_kernel, has_gain=gain is not None, has_bias=bias is not None,
                          has_res=residual is not None, has_side=side_w is not None, rope_cols=rope_cols,
                          n_cols=n, prep_x=prep_x),
        out_shape=out_shape,
        grid=(m // tm, n // tn),
        in_specs=specs,
        out_specs=out_specs,
        scratch_shapes=[pltpu.VMEM((tm, k), BF16)] if prep_x and tn < n else [],
        compiler_params=pltpu.CompilerParams(dimension_semantics=("parallel", "arbitrary"),
                                             vmem_limit_bytes=_vmem_limit(est)),
        name=name,
    )(*args)


MLP_ROWS = 512
MLP_FF_CHUNK = 512


def _mlp_kernel(*refs, has_bias, has_final):
    refs = list(refs)
    x_ref, a_ref, wo_ref = refs.pop(0), refs.pop(0), refs.pop(0)
    bo_ref = refs.pop(0) if has_bias else None
    g_ref, wu_ref, wd_ref = refs.pop(0), refs.pop(0), refs.pop(0)
    gf_ref = refs.pop(0) if has_final else None
    o_ref = refs.pop(0)
    x = x_ref[...] + jnp.dot(a_ref[...], wo_ref[...], preferred_element_type=F32)
    if has_bias:
        x = x + bo_ref[...]
    xn = (_rms(x) * g_ref[...]).astype(BF16)
    y = x
    ff = wu_ref.shape[1]
    step = MLP_FF_CHUNK if ff % MLP_FF_CHUNK == 0 else ff
    for c0 in range(0, ff, step):
        u = jnp.maximum(jnp.dot(xn, wu_ref[:, c0:c0 + step], preferred_element_type=F32), 0.0)
        y = y + jnp.dot((u * u).astype(BF16), wd_ref[c0:c0 + step, :], preferred_element_type=F32)
    if has_final:
        y = _rms(y) * gf_ref[...]
    o_ref[...] = y


def _mix_out_mlp(h, mix, w_o, b_o, gain, w_up, w_down, final_gain=None):
    m, d = h.shape
    k = mix.shape[1]
    ff = w_up.shape[1]
    tm = MLP_ROWS if m % MLP_ROWS == 0 else _row_tile(m)
    row = lambda i: (i, 0)
    const = lambda i: (0, 0)
    vec = pl.BlockSpec((1, d), const)
    resident = lambda shape: pl.BlockSpec(shape, const, pipeline_mode=pl.Buffered(1))
    args = [h, mix, w_o]
    specs = [pl.BlockSpec((tm, d), row), pl.BlockSpec((tm, k), row), resident((k, d))]
    if b_o is not None:
        args.append(b_o.reshape(1, d).astype(F32))
        specs.append(vec)
    args += [gain.reshape(1, d).astype(F32), w_up, w_down]
    specs += [vec, resident((d, ff)), resident((ff, d))]
    if final_gain is not None:
        args.append(final_gain.reshape(1, d).astype(F32))
        specs.append(vec)
    est = 4 * d * ff + 2 * k * d + 4 * tm * k + 4 * tm * d * 4 + 4 * tm * d * 4 + 3 * tm * MLP_FF_CHUNK * 4
    return pl.pallas_call(
        functools.partial(_mlp_kernel, has_bias=b_o is not None, has_final=final_gain is not None),
        out_shape=jax.ShapeDtypeStruct((m, d), F32),
        grid=(m // tm,),
        in_specs=specs,
        out_specs=pl.BlockSpec((tm, d), row),
        compiler_params=pltpu.CompilerParams(dimension_semantics=("parallel",),
                                             vmem_limit_bytes=_vmem_limit(est)),
        name="sqrelu_mlp",
    )(*args)


Q_SCALE = HEAD_DIM ** -0.5 * LOG2_E
ATTN_PIPE_DEPTH = 2


def _low_head_lanes(shape, interleaved):
    lane = lax.broadcasted_iota(jnp.int32, shape, 1)
    return (lane % HEAD_DIM) < HEAD_DIM // 2 if interleaved else lane < HEAD_DIM


def _stack_two_heads(t, interleaved=False):
    low = _low_head_lanes(t.shape, interleaved)
    zero = jnp.zeros_like(t)
    return jnp.concatenate([jnp.where(low, t, zero), jnp.where(low, zero, t)], axis=0)


def _init_band_consts(bias_ref, ones_ref, max_dist):
    r = lax.broadcasted_iota(jnp.int32, (ATTN_BLOCK, 2 * ATTN_BLOCK), 0)
    c = lax.broadcasted_iota(jnp.int32, (ATTN_BLOCK, 2 * ATTN_BLOCK), 1)
    dist = r + ATTN_BLOCK - c
    ok = (dist >= 0) & (dist <= max_dist)
    bias_ref[0] = jnp.where(ok & (c >= ATTN_BLOCK), 0.0, -jnp.inf)
    bias_ref[1] = jnp.where(ok, 0.0, -jnp.inf)
    low = _low_head_lanes((2 * ATTN_BLOCK, LANES), False)
    ones_ref[0:2 * ATTN_BLOCK, :] = jnp.where(low, 1.0, 0.0).astype(BF16)
    ones_ref[2 * ATTN_BLOCK:, :] = jnp.where(low, 0.0, 1.0).astype(BF16)


def _head_pieces(t, interleaved):
    low = _low_head_lanes(t.shape, interleaved)
    return jnp.where(low, t, 0.0).astype(BF16), jnp.where(low, 0.0, t).astype(BF16)


def _piece_cache(load, interleaved):
    cache = {}

    def get(key):
        if key not in cache:
            cache[key] = _head_pieces(load(key), interleaved)
        return cache[key]
    return get


def _stack_blocks(prev, cur):
    return jnp.concatenate([prev[0], cur[0], prev[1], cur[1]], axis=0)


def _scores(q2, k_stack):
    return lax.dot_general(q2, k_stack, (((1,), (1,)), ((), ())), preferred_element_type=F32)


def _probs(s, bias, sinks):
    kw = 2 * ATTN_BLOCK
    s0, s1 = s[:, :kw] + bias, s[:, kw:] + bias
    m0 = jnp.max(s0, axis=-1, keepdims=True)
    m1 = jnp.max(s1, axis=-1, keepdims=True)
    if sinks is not None:
        m0, m1 = jnp.maximum(m0, sinks[0]), jnp.maximum(m1, sinks[1])
    p2 = jnp.concatenate([jnp.exp2(s0 - m0), jnp.exp2(s1 - m1)], axis=1).astype(BF16)
    return p2, jnp.where(_low_head_lanes((s.shape[0], LANES), False), m0, m1)


def _weighted(p2, m_l, v_stack, ones_stack, sinks):
    od = jnp.dot(p2, jnp.concatenate([v_stack, ones_stack], axis=1), preferred_element_type=F32)
    o, den = od[:, :LANES], od[:, LANES:]
    if sinks is not None:
        den = den + jnp.exp2(jnp.where(_low_head_lanes(o.shape, False), sinks[0], sinks[1]) - m_l)
    return o / den, m_l + jnp.log2(den)


def _run_pipelined(n, first_stage, mid_stage, last_stage):
    ahead = {i: first_stage(i) for i in range(min(ATTN_PIPE_DEPTH, n))}
    for i in range(n):
        mid = mid_stage(i, ahead.pop(i))
        if i + ATTN_PIPE_DEPTH < n:
            ahead[i + ATTN_PIPE_DEPTH] = first_stage(i + ATTN_PIPE_DEPTH)
        last_stage(i, mid)


def _swa_kernel(sink_ref, q_ref, k_ref, v_ref, o_ref, kb_ref, vb_ref, bias_ref, ones_ref):
    seq = q_ref.shape[0]
    p = pl.program_id(1)
    pairs_per_kv = A_Q_PER_KV // 2
    rows = 2 * ATTN_BLOCK

    @pl.when(p == 0)
    def _():
        _init_band_consts(bias_ref, ones_ref, A_WINDOW - 1)

    def spread_kv_head(kv):
        k_low, v_low = _low_head_lanes((rows, LANES), True), _low_head_lanes((rows, LANES), False)
        k_own, v_own = (k_low, v_low) if kv == 0 else (jnp.logical_not(k_low), jnp.logical_not(v_low))
        k_shift = HEAD_DIM // 2 if kv == 0 else LANES - HEAD_DIM // 2
        for c in range(seq // rows):
            sl = slice(c * rows, (c + 1) * rows)
            kt, vt = k_ref[sl, :].astype(F32), v_ref[sl, :].astype(F32)
            kt = jnp.where(k_own, kt, pltpu.roll(kt, k_shift, 1))
            vt = jnp.where(v_own, vt, pltpu.roll(vt, HEAD_DIM, 1))
            for dst, t, low in ((kb_ref, kt, k_low), (vb_ref, vt, v_low)):
                dst[0, sl, :] = jnp.where(low, t, 0.0).astype(BF16)
                dst[1, sl, :] = jnp.where(low, 0.0, t).astype(BF16)

    for kv in range(2):
        pl.when(p == kv * pairs_per_kv)(functools.partial(spread_kv_head, kv))

    sinks = (sink_ref[2 * p] * LOG2_E, sink_ref[2 * p + 1] * LOG2_E)
    blk = lambda i: slice(i * ATTN_BLOCK, (i + 1) * ATTN_BLOCK)

    def stack(ref, i):
        if i == 0:
            return jnp.concatenate([ref[h, blk(0), :] for h in (0, 0, 1, 1)], axis=0)
        both = slice((i - 1) * ATTN_BLOCK, (i + 1) * ATTN_BLOCK)
        return jnp.concatenate([ref[0, both, :], ref[1, both, :]], axis=0)

    def score_stage(i):
        return _scores((q_ref[blk(i), :].astype(F32) * Q_SCALE).astype(BF16), stack(kb_ref, i))

    def prob_stage(i, s):
        return _probs(s, bias_ref[min(i, 1)], sinks)

    def out_stage(i, mid):
        o, _ = _weighted(*mid, stack(vb_ref, i), ones_ref[...], sinks)
        o_ref[blk(i), :] = o.astype(o_ref.dtype)

    _run_pipelined(seq // ATTN_BLOCK, score_stage, prob_stage, out_stage)


def _swa_attention(qkv, sinks, batch, seq):
    m = qkv.shape[0]
    q_dim = sinks.shape[0] * HEAD_DIM
    assert q_dim // HEAD_DIM // A_Q_PER_KV == 2 and seq % (2 * ATTN_BLOCK) == 0
    q3 = qkv.reshape(batch, seq, qkv.shape[1])
    kcol, vcol = q_dim // LANES, q_dim // LANES + 1
    slab = (None, seq, LANES)
    out = pl.pallas_call(
        _swa_kernel,
        out_shape=jax.ShapeDtypeStruct((batch, seq, q_dim), BF16),
        grid=(batch, q_dim // LANES),
        in_specs=[
            pl.BlockSpec(memory_space=pltpu.SMEM),
            pl.BlockSpec(slab, lambda b, p: (b, 0, p)),
            pl.BlockSpec(slab, lambda b, p: (b, 0, kcol)),
            pl.BlockSpec(slab, lambda b, p: (b, 0, vcol)),
        ],
        out_specs=pl.BlockSpec(slab, lambda b, p: (b, 0, p)),
        scratch_shapes=[pltpu.VMEM((2, seq, LANES), BF16), pltpu.VMEM((2, seq, LANES), BF16),
                        pltpu.VMEM((2, ATTN_BLOCK, 2 * ATTN_BLOCK), F32), pltpu.VMEM((4 * ATTN_BLOCK, LANES), BF16)],
        compiler_params=pltpu.CompilerParams(dimension_semantics=("parallel", "arbitrary"),
                                             vmem_limit_bytes=_vmem_limit(16 * seq * LANES * 2)),
        name="swa_sink_attention",
    )(sinks.astype(F32), q3, q3, q3)
    return out.reshape(m, q_dim)


def _dilated_kernel(*refs):
    n_groups = len(C_PATTERNS)
    in_refs, o_ref = refs[:3 * n_groups], refs[3 * n_groups]
    acc_o, acc_l, bias_ref, ones_ref = refs[3 * n_groups + 1:]
    seq = o_ref.shape[0]
    max_dist = C_PATTERNS[0][0] // C_PATTERNS[0][1]
    assert all(w // d == max_dist for w, d in C_PATTERNS)
    _init_band_consts(bias_ref, ones_ref, max_dist)

    blocks = []
    for g, (window, dil) in enumerate(C_PATTERNS):
        for r in range(dil):
            for i in range(seq // dil // ATTN_BLOCK):
                blocks.append((g, dil, r, i))

    def rows(dil, r, i):
        start = i * ATTN_BLOCK * dil + r
        return pl.ds(start, ATTN_BLOCK, stride=dil) if dil > 1 else pl.ds(start, ATTN_BLOCK)

    def loader(which):
        return lambda key: in_refs[3 * key[0] + which][rows(*key[1:]), :]

    k_pieces, v_pieces = _piece_cache(loader(1), True), _piece_cache(loader(2), False)

    def stack(pieces, g, dil, r, i):
        return _stack_blocks(pieces((g, dil, r, max(i - 1, 0))), pieces((g, dil, r, i)))

    def score_stage(n):
        g, dil, r, i = blocks[n]
        q2 = (in_refs[3 * g][rows(dil, r, i), :] * Q_SCALE).astype(BF16)
        return _scores(q2, stack(k_pieces, g, dil, r, i))

    def prob_stage(n, s):
        return _probs(s, bias_ref[min(blocks[n][3], 1)], None)

    def out_stage(n, mid):
        g, dil, r, i = blocks[n]
        cur = rows(dil, r, i)
        o, lse = _weighted(*mid, stack(v_pieces, g, dil, r, i), ones_ref[...], None)
        if g > 0:
            o_old, l_old = acc_o[cur, :], acc_l[cur, :]
            l_max = jnp.maximum(l_old, lse)
            e_old, e_new = jnp.exp2(l_old - l_max), jnp.exp2(lse - l_max)
            inv = 1.0 / (e_old + e_new)
            o = o_old * (e_old * inv) + o * (e_new * inv)
            lse = l_max + jnp.log2(e_old + e_new)
        acc_o[cur, :] = o
        if g < n_groups - 1:
            acc_l[cur, :] = lse

    _run_pipelined(len(blocks), score_stage, prob_stage, out_stage)
    o_ref[...] = acc_o[...].astype(o_ref.dtype)


def _dilated_attention(qkv, batch, seq):
    m, width = qkv.shape
    n_groups = len(C_PATTERNS)
    gw = width // (3 * n_groups)
    assert gw % LANES == 0 and all(seq % (d * ATTN_BLOCK) == 0 for _, d in C_PATTERNS)
    pairs = gw // LANES
    q3 = qkv.reshape(batch, seq, width)
    slab = lambda c: pl.BlockSpec((None, seq, LANES), lambda b, p: (b, 0, c * pairs + p))
    specs = []
    for g in range(n_groups):
        specs += [slab(g), slab(n_groups + g), slab(2 * n_groups + g)]
    out = pl.pallas_call(
        _dilated_kernel,
        out_shape=jax.ShapeDtypeStruct((batch, seq, gw), BF16),
        grid=(batch, pairs),
        in_specs=specs,
        out_specs=pl.BlockSpec((None, seq, LANES), lambda b, p: (b, 0, p)),
        scratch_shapes=[pltpu.VMEM((seq, LANES), F32), pltpu.VMEM((seq, LANES), F32),
                        pltpu.VMEM((2, ATTN_BLOCK, 2 * ATTN_BLOCK), F32), pltpu.VMEM((4 * ATTN_BLOCK, LANES), BF16)],
        compiler_params=pltpu.CompilerParams(
            dimension_semantics=("parallel", "parallel"),
            vmem_limit_bytes=_vmem_limit(2 * (3 * n_groups + 2) * seq * LANES * 4 + 8 * 1024 * 1024)),
        name="dilated_attention",
    )(*([q3] * (3 * n_groups)))
    return out.reshape(m, gw)


def _ssd_kernel(z_ref, x_ref, b_ref, c_ref, dt_ref, cw_ref, cb_ref, dtb_ref, alog_ref, dskip_ref, nw_ref, e_ref,
                o_ref, xpad_ref, state_ref, xs_ref, bm_ref, cm_ref):
    q = SSM_CHUNK
    d_inner = x_ref.shape[1]
    bc_dim = b_ref.shape[1]
    n_heads = d_inner // SSM_HEAD_DIM
    group_w = d_inner // SSM_N_GROUPS
    pad = BF16_ROWS

    @pl.when(pl.program_id(1) == 0)
    def _():
        xpad_ref[0:pad, :] = jnp.zeros((pad, xpad_ref.shape[1]), BF16)
        state_ref[...] = jnp.zeros_like(state_ref)

    xpad_ref[pad:pad + q, 0:d_inner] = x_ref[...]
    xpad_ref[pad:pad + q, d_inner:d_inner + bc_dim] = b_ref[...]
    xpad_ref[pad:pad + q, d_inner + bc_dim:] = c_ref[...]
    out_row = lax.broadcasted_iota(jnp.int32, (q, pad + q), 0)
    src_row = lax.broadcasted_iota(jnp.int32, (q, pad + q), 1)
    shifters = [jnp.where(src_row == pad + out_row - k, 1.0, 0.0).astype(BF16) for k in range(1, SSM_CONV)]
    conv_tile = MXU_COLS
    n_tiles = xpad_ref.shape[1] // conv_tile

    def conv_tiles(t_lo, t_hi):
        for t in range(t_lo, t_hi):
            sl = slice(t * conv_tile, (t + 1) * conv_tile)
            acc = cb_ref[:, sl] + cw_ref[SSM_CONV - 1:SSM_CONV, sl] * xpad_ref[pad:pad + q, sl].astype(F32)
            for k in range(1, SSM_CONV):
                shifted = jnp.dot(shifters[k - 1], xpad_ref[:, sl], preferred_element_type=F32)
                acc = acc + cw_ref[SSM_CONV - 1 - k:SSM_CONV - k, sl] * shifted
            y = acc * _sigmoid(acc)
            lo = t * conv_tile
            if lo < d_inner:
                xs_ref[:, lo:lo + conv_tile] = y
            elif lo < d_inner + bc_dim:
                bm_ref[:, lo - d_inner:lo - d_inner + conv_tile] = y
            else:
                cm_ref[:, lo - d_inner - bc_dim:lo - d_inner - bc_dim + conv_tile] = y.astype(BF16)

    head_lane = lax.broadcasted_iota(jnp.int32, (q, LANES), 1) < n_heads
    dtr = dt_ref[...] + dtb_ref[...]
    dt = jnp.maximum(dtr, 0.0) + jnp.log(1.0 + jnp.exp(-jnp.abs(dtr)))
    dt = jnp.where(head_lane, dt, 0.0)
    d_a = dt * (-jnp.exp(alog_ref[...]))
    rr = lax.broadcasted_iota(jnp.int32, (q, q), 0)
    cc = lax.broadcasted_iota(jnp.int32, (q, q), 1)
    tril = rr >= cc
    tri16 = jnp.where(tril, 1.0, 0.0).astype(BF16)
    d_a_parts = _split_bf16(d_a, 3)
    conv_tiles(0, n_tiles // 4)
    cs = None
    for part in d_a_parts:
        term = jnp.dot(tri16, part, preferred_element_type=F32)
        cs = term if cs is None else cs + term
    conv_tiles(n_tiles // 4, n_tiles // 2)
    cs2 = cs * LOG2_E
    cs2_t = cs2.T
    ecs = jnp.exp(cs)
    dec = jnp.exp(cs[q - 1:q, :] - cs)
    conv_tiles(n_tiles // 2, 3 * n_tiles // 4)
    e16 = e_ref[...]
    dt_e = _expand_heads(dt, e16)
    ecs_e = _expand_heads(ecs, e16)
    dec_e = _expand_heads(dec, e16)
    chunk_decay_e = ecs_e[q - 1:q, :]
    conv_tiles(3 * n_tiles // 4, n_tiles)
    xpad_ref[0:pad, :] = xpad_ref[q:q + pad, :]

    heads_per_group = n_heads // SSM_N_GROUPS
    lanes_of = lambda g: slice(g * group_w, (g + 1) * group_w)

    def state_stage(g):
        gs, ns = lanes_of(g), slice(g * SSM_D_STATE, (g + 1) * SSM_D_STATE)
        b_f = bm_ref[:, ns]
        c16 = cm_ref[:, ns]
        cb = lax.dot_general(c16, b_f.astype(BF16), (((1,), (1,)), ((), ())), preferred_element_type=F32)
        xdt = xs_ref[:, gs] * dt_e[:, gs]
        st_old = state_ref[:, gs]
        y_off = jnp.dot(c16, st_old.astype(BF16), preferred_element_type=F32) * ecs_e[:, gs]
        st_new = jnp.dot(b_f.T.astype(BF16), (xdt * dec_e[:, gs]).astype(BF16), preferred_element_type=F32)
        state_ref[:, gs] = st_old * chunk_decay_e[:, gs] + st_new
        return cb, y_off, xdt.astype(BF16)

    def diag_stage(g, carried):
        cb, y_off, xdt16 = carried
        y_parts = []
        for pr in range(heads_per_group // 2):
            h0 = g * heads_per_group + 2 * pr
            ms = []
            for hh in (h0, h0 + 1):
                diff = cs2[:, hh:hh + 1] - cs2_t[hh:hh + 1, :]
                ms.append(cb * jnp.exp2(jnp.where(tril, diff, -jnp.inf)))
            m2 = jnp.concatenate(ms, axis=1).astype(BF16)
            x_stack = _stack_two_heads(xdt16[:, pr * LANES:(pr + 1) * LANES])
            y_parts.append(jnp.dot(m2, x_stack, preferred_element_type=F32))
        return jnp.concatenate(y_parts, axis=1) + y_off

    def gate_stage(g, y):
        gs = lanes_of(g)
        y = y + dskip_ref[:, gs] * xs_ref[:, gs]
        zf = z_ref[:, gs].astype(F32)
        gated = y * (zf * _sigmoid(zf))
        o_ref[:, gs] = (_rms(gated) * nw_ref[:, gs]).astype(o_ref.dtype)

    _run_pipelined(SSM_N_GROUPS, state_stage, diag_stage, gate_stage)


def _ssd_mixer(zx, dt_raw, conv_w, conv_b, dt_bias, a_log, d_skip, norm_w, batch, seq):
    m = zx.shape[0]
    n_heads = a_log.shape[0]
    d_inner = n_heads * SSM_HEAD_DIM
    bc_dim = SSM_N_GROUPS * SSM_D_STATE
    conv_dim = d_inner + 2 * bc_dim
    assert zx.shape[1] == d_inner + conv_dim and seq % SSM_CHUNK == 0 and n_heads <= LANES
    assert d_inner % bc_dim == 0 and conv_dim % 512 == 0 and d_inner // SSM_N_GROUPS == 2 * LANES
    zx3 = zx.reshape(batch, seq, zx.shape[1])
    dt3 = dt_raw.reshape(batch, seq, LANES)
    pad_heads = lambda v: jnp.pad(v.astype(F32), (0, LANES - n_heads))[None, :]
    blk = lambda w: (None, SSM_CHUNK, w)
    const = lambda shape: pl.BlockSpec(shape, lambda b, c: (0,) * len(shape))
    r = d_inner // bc_dim
    out = pl.pallas_call(
        _ssd_kernel,
        out_shape=jax.ShapeDtypeStruct((batch, seq, d_inner), BF16),
        grid=(batch, seq // SSM_CHUNK),
        in_specs=[
            pl.BlockSpec(blk(d_inner), lambda b, c: (b, c, 0)),
            pl.BlockSpec(blk(d_inner), lambda b, c: (b, c, 1)),
            pl.BlockSpec(blk(bc_dim), lambda b, c: (b, c, 2 * r)),
            pl.BlockSpec(blk(bc_dim), lambda b, c: (b, c, 2 * r + 1)),
            pl.BlockSpec(blk(LANES), lambda b, c: (b, c, 0)),
            const((SSM_CONV, conv_dim)), const((1, conv_dim)), const((1, LANES)), const((1, LANES)),
            const((1, d_inner)), const((1, d_inner)), const((LANES, d_inner)),
        ],
        out_specs=pl.BlockSpec(blk(d_inner), lambda b, c: (b, c, 0)),
        scratch_shapes=[
            pltpu.VMEM((SSM_CHUNK + BF16_ROWS, conv_dim), BF16),
            pltpu.VMEM((SSM_D_STATE, d_inner), F32),
            pltpu.VMEM((SSM_CHUNK, d_inner), F32),
            pltpu.VMEM((SSM_CHUNK, bc_dim), F32),
            pltpu.VMEM((SSM_CHUNK, bc_dim), BF16),
        ],
        compiler_params=pltpu.CompilerParams(dimension_semantics=("parallel", "arbitrary"),
                                             vmem_limit_bytes=_vmem_limit(40 * 1024 * 1024)),
        name="ssd_mixer",
    )(zx3, zx3, zx3, zx3, dt3, conv_w.astype(F32), conv_b.astype(F32)[None, :], pad_heads(dt_bias),
      pad_heads(a_log), jnp.repeat(d_skip.astype(F32), SSM_HEAD_DIM)[None, :], norm_w.astype(F32)[None, :],
      _head_expander(n_heads, SSM_HEAD_DIM))
    return out.reshape(m, d_inner)


def kernel(x, positions, norm_mix_w, norm_mlp_w, a_w_qkv, a_b_qkv, a_sinks, a_w_o, a_b_o, b_in_w, b_conv_w, b_conv_b, b_dt_bias, b_a_log, b_d, b_norm_w, b_out_w, c_w_qkv, c_w_o, mlp_w_up, mlp_w_down, final_norm_w):
    batch, seq, d_model = x.shape
    m = batch * seq
    depth = norm_mix_w.shape[0]
    h = x.reshape(m, d_model).astype(F32)
    rope = _rope_tables(positions.reshape(m, 1))

    for i in range(depth):
        kind, j = i % N_MIXERS, i // N_MIXERS
        if kind == 0:
            q_dim = a_sinks.shape[1] * HEAD_DIM
            kv_dim = (a_w_qkv.shape[2] - q_dim) // 2
            n_rope = q_dim + kv_dim
            qkv = _linear(h, _pair_interleave(a_w_qkv[j], n_rope).astype(BF16), gain=norm_mix_w[i],
                          bias=_pair_interleave(a_b_qkv[j], n_rope), rope=rope, rope_cols=n_rope, name="swa_qkv")
            mix, w_o, b_o = _swa_attention(qkv, a_sinks[j], batch, seq), a_w_o[j], a_b_o[j]
        elif kind == 1:
            n_heads = b_a_log.shape[1]
            w_in = b_in_w[j]
            main = w_in.shape[1] - n_heads
            w_dt = jnp.pad(w_in[:, main:], ((0, 0), (0, LANES - n_heads))).astype(BF16)
            zx, dt_raw = _linear(h, w_in[:, :main].astype(BF16), gain=norm_mix_w[i], side_w=w_dt, tn=1536,
                                 name="ssm_in")
            mix = _ssd_mixer(zx, dt_raw, b_conv_w[j], b_conv_b[j], b_dt_bias[j], b_a_log[j], b_d[j], b_norm_w[j],
                             batch, seq)
            w_o, b_o = b_out_w[j], None
        else:
            width = c_w_qkv.shape[2]
            n_rope = 2 * width // 3
            qkv = _linear(h, _pair_interleave(c_w_qkv[j], n_rope).astype(BF16), gain=norm_mix_w[i], rope=rope,
                          rope_cols=n_rope, out_dtype=F32, tn=1536, name="dilated_qkv")
            mix, w_o, b_o = _dilated_attention(qkv, batch, seq), c_w_o[j], None
        final_gain = final_norm_w if i == depth - 1 else None
        h = _mix_out_mlp(h, mix, w_o.astype(BF16), b_o, norm_mlp_w[i], mlp_w_up[i].astype(BF16),
                         mlp_w_down[i].astype(BF16), final_gain)
    return h.reshape(batch, seq, d_model).astype(x.dtype)
```

```python
import functools
import math

import jax
import jax.numpy as jnp
from jax import lax
from jax.experimental import pallas as pl
from jax.experimental.pallas import tpu as pltpu

F32 = jnp.float32
BF16 = jnp.bfloat16

N_MIXERS = 3
ATTN_BLOCK = 128
HEAD_DIM = 64
ROPE_THETA = 10000.0
NORM_EPS = 1e-5
A_Q_PER_KV = 8
A_WINDOW = 128
C_PATTERNS = ((128, 1), (512, 4), (2048, 16))
SSM_HEAD_DIM = 64
SSM_N_GROUPS = 8
SSM_D_STATE = 128
SSM_CONV = 4
SSM_CHUNK = 128

LANES = 128
LOG2_E = math.log2(math.e)
SUBLANES = 8
BF16_ROWS = 16
V7X_VMEM_LIMIT_CAP = 56 * 1024 * 1024


def _vmem_limit(nbytes):
    return int(min(max(nbytes, 16 * 1024 * 1024), V7X_VMEM_LIMIT_CAP))


def _row_tile(m):
    for t in (1024, 512, 256, 128):
        if m % t == 0:
            return t
    raise ValueError(f"token count {m} must be a multiple of 128")


def _rms(xf):
    return xf * lax.rsqrt(jnp.mean(xf * xf, axis=-1, keepdims=True) + NORM_EPS)


def _sigmoid(v):
    return 1.0 / (1.0 + jnp.exp2(v * (-LOG2_E)))


def _split_bf16(v, n):
    parts, r = [], v
    for _ in range(n):
        p = r.astype(BF16)
        parts.append(p)
        r = r - p.astype(F32)
    return parts


def _dot_split(parts, rhs):
    acc = None
    for p in parts:
        d = jnp.dot(p, rhs, preferred_element_type=F32)
        acc = d if acc is None else acc + d
    return acc


def _expand_heads(v, e_bf16):
    return _dot_split(_split_bf16(v, 2), e_bf16)


def _head_expander(n_heads, head_dim):
    rows = jnp.arange(LANES)[:, None]
    cols = jnp.arange(n_heads * head_dim)[None, :] // head_dim
    return (rows == cols).astype(BF16)


def _rope_table_kernel(pos_ref, inv_ref, sign_ref, cos_ref, sin_ref):
    ang = pos_ref[...].astype(F32) * inv_ref[...]
    cos_ref[...] = jnp.cos(ang)
    sin_ref[...] = jnp.sin(ang) * sign_ref[...]


def _rope_tables(pos_col):
    m = pos_col.shape[0]
    tm = _row_tile(m)
    half = HEAD_DIM // 2
    inv = ROPE_THETA ** (-jnp.arange(half, dtype=F32) / half)
    inv_l = jnp.tile(inv, LANES // half)[None, :]
    sign_l = jnp.where(jnp.arange(LANES) < LANES // 2, -1.0, 1.0).astype(F32)[None, :]
    row = pl.BlockSpec((tm, LANES), lambda i: (i, 0))
    const = pl.BlockSpec((1, LANES), lambda i: (0, 0))
    return pl.pallas_call(
        _rope_table_kernel,
        out_shape=(jax.ShapeDtypeStruct((m, LANES), F32), jax.ShapeDtypeStruct((m, LANES), F32)),
        grid=(m // tm,),
        in_specs=[pl.BlockSpec((tm, 1), lambda i: (i, 0)), const, const],
        out_specs=(row, row),
        compiler_params=pltpu.CompilerParams(dimension_semantics=("parallel",)),
        name="rope_tables",
    )(pos_col, inv_l, sign_l)


def _pair_interleave(w, n_rope):
    lead = w.shape[:-1]
    head = w[..., :n_rope].reshape(*lead, n_rope // LANES, 4, LANES // 4)
    head = jnp.stack([head[..., 0, :], head[..., 2, :], head[..., 1, :], head[..., 3, :]], axis=-2)
    return jnp.concatenate([head.reshape(*lead, n_rope), w[..., n_rope:]], axis=-1)


def _rope_lanes(yc, cos, sin_signed):
    return yc * cos + pltpu.roll(yc, LANES // 2, 1) * sin_signed


MXU_COLS = 256


def _linear_kernel(*refs, has_gain, has_bias, has_res, has_side, rope_cols, n_cols, prep_x):
    refs = list(refs)
    x_ref, w_ref = refs.pop(0), refs.pop(0)
    g_ref = refs.pop(0) if has_gain else None
    b_ref = refs.pop(0) if has_bias else None
    cos_ref, sin_ref = (refs.pop(0), refs.pop(0)) if rope_cols else (None, None)
    r_ref = refs.pop(0) if has_res else None
    ws_ref = refs.pop(0) if has_side else None
    o_ref = refs.pop(0)
    side_ref = refs.pop(0) if has_side else None
    j = pl.program_id(1)
    tn = o_ref.shape[1]

    def prepared():
        xf = x_ref[...].astype(F32)
        if has_gain:
            xf = _rms(xf) * g_ref[...]
        return xf.astype(BF16)

    if not prep_x:
        xb = x_ref[...]
    elif tn == n_cols:
        xb = prepared()
    else:
        xn_ref = refs.pop(0)

        @pl.when(j == 0)
        def _():
            xn_ref[...] = prepared()
            if has_side:
                side_ref[...] = jnp.dot(xn_ref[...], ws_ref[...], preferred_element_type=F32)
        xb = xn_ref[...]
    if has_side and (not prep_x or tn == n_cols):
        side_ref[...] = jnp.dot(xb, ws_ref[...], preferred_element_type=F32)

    tiled_rope = rope_cols and tn < n_cols
    if rope_cols:
        cos, sin = cos_ref[...], sin_ref[...]
        if tiled_rope:
            is_rope = j < rope_cols // tn
            cos, sin = jnp.where(is_rope, cos, 1.0), jnp.where(is_rope, sin, 0.0)

    step = MXU_COLS if tn % MXU_COLS == 0 else LANES
    for c0 in range(0, tn, step):
        acc = jnp.dot(xb, w_ref[:, c0:c0 + step], preferred_element_type=F32)
        if has_bias:
            acc = acc + b_ref[:, c0:c0 + step]
        if has_res:
            acc = acc + r_ref[:, c0:c0 + step]
        for l0 in range(0, step, LANES):
            yc = acc[:, l0:l0 + LANES]
            if rope_cols and (tiled_rope or c0 + l0 < rope_cols):
                yc = _rope_lanes(yc, cos, sin)
            o_ref[:, c0 + l0:c0 + l0 + LANES] = yc.astype(o_ref.dtype)


def _linear(x, w, *, gain=None, bias=None, rope=None, rope_cols=0, residual=None, side_w=None,
            out_dtype=BF16, tm=None, tn=None, name="linear"):
    m, k = x.shape
    n = w.shape[1]
    tm = _row_tile(m) if tm is None else tm
    tn = n if tn is None else tn
    assert m % tm == 0 and n % tn == 0 and tn % LANES == 0
    if rope_cols and rope_cols % tn:
        assert tn == n and rope_cols % LANES == 0
    prep_x = gain is not None or x.dtype != BF16
    row = lambda i, j: (i, 0)
    col = lambda i, j: (0, j)
    const = lambda i, j: (0, 0)
    w_spec = pl.BlockSpec((k, tn), col, pipeline_mode=pl.Buffered(1)) if tn == n else pl.BlockSpec((k, tn), col)
    args, specs = [x, w], [pl.BlockSpec((tm, k), row), w_spec]
    if gain is not None:
        args.append(gain.reshape(1, k).astype(F32))
        specs.append(pl.BlockSpec((1, k), const))
    if bias is not None:
        args.append(bias.reshape(1, n).astype(F32))
        specs.append(pl.BlockSpec((1, tn), col))
    if rope_cols:
        args += list(rope)
        specs += [pl.BlockSpec((tm, LANES), row)] * 2
    if residual is not None:
        args.append(residual)
        specs.append(pl.BlockSpec((tm, tn), lambda i, j: (i, j)))
    out_shape = jax.ShapeDtypeStruct((m, n), out_dtype)
    out_specs = pl.BlockSpec((tm, tn), lambda i, j: (i, j))
    if side_w is not None:
        args.append(side_w)
        specs.append(pl.BlockSpec((k, LANES), const))
        out_shape = (out_shape, jax.ShapeDtypeStruct((m, LANES), F32))
        out_specs = (out_specs, pl.BlockSpec((tm, LANES), row))
    out_bytes = jnp.dtype(out_dtype).itemsize
    est = (2 * tm * k * x.dtype.itemsize + 2 * k * tn * 2 + 2 * tm * tn * out_bytes
           + (2 * tm * tn * 4 if residual is not None else 0) + tm * k * 2 + 2 * tm * k * 4
           + 4 * tm * MXU_COLS * 4 + 6 * tm * LANES * 4)
    return pl.pallas_call(
        functools.partial(_linear_kernel, has_gain=gain is not None, has_bias=bias is not None,
                          has_res=residual is not None, has_side=side_w is not None, rope_cols=rope_cols,
                          n_cols=n, prep_x=prep_x),
        out_shape=out_shape,
        grid=(m // tm, n // tn),
        in_specs=specs,
        out_specs=out_specs,
        scratch_shapes=[pltpu.VMEM((tm, k), BF16)] if prep_x and tn < n else [],
        compiler_params=pltpu.CompilerParams(dimension_semantics=("parallel", "arbitrary"),
                                             vmem_limit_bytes=_vmem_limit(est)),
        name=name,
    )(*args)


MLP_ROWS = 512
MLP_FF_CHUNK = 512


def _mlp_kernel(*refs, has_bias, has_final):
    refs = list(refs)
    x_ref, a_ref, wo_ref = refs.pop(0), refs.pop(0), refs.pop(0)
    bo_ref = refs.pop(0) if has_bias else None
    g_ref, wu_ref, wd_ref = refs.pop(0), refs.pop(0), refs.pop(0)
    gf_ref = refs.pop(0) if has_final else None
    o_ref = refs.pop(0)
    x = x_ref[...] + jnp.dot(a_ref[...], wo_ref[...], preferred_element_type=F32)
    if has_bias:
        x = x + bo_ref[...]
    xn = (_rms(x) * g_ref[...]).astype(BF16)
    y = x
    ff = wu_ref.shape[1]
    step = MLP_FF_CHUNK if ff % MLP_FF_CHUNK == 0 else ff
    for c0 in range(0, ff, step):
        u = jnp.maximum(jnp.dot(xn, wu_ref[:, c0:c0 + step], preferred_element_type=F32), 0.0)
        y = y + jnp.dot((u * u).astype(BF16), wd_ref[c0:c0 + step, :], preferred_element_type=F32)
    if has_final:
        y = _rms(y) * gf_ref[...]
    o_ref[...] = y


def _mix_out_mlp(h, mix, w_o, b_o, gain, w_up, w_down, final_gain=None):
    m, d = h.shape
    k = mix.shape[1]
    ff = w_up.shape[1]
    tm = MLP_ROWS if m % MLP_ROWS == 0 else _row_tile(m)
    row = lambda i: (i, 0)
    const = lambda i: (0, 0)
    vec = pl.BlockSpec((1, d), const)
    resident = lambda shape: pl.BlockSpec(shape, const, pipeline_mode=pl.Buffered(1))
    args = [h, mix, w_o]
    specs = [pl.BlockSpec((tm, d), row), pl.BlockSpec((tm, k), row), resident((k, d))]
    if b_o is not None:
        args.append(b_o.reshape(1, d).astype(F32))
        specs.append(vec)
    args += [gain.reshape(1, d).astype(F32), w_up, w_down]
    specs += [vec, resident((d, ff)), resident((ff, d))]
    if final_gain is not None:
        args.append(final_gain.reshape(1, d).astype(F32))
        specs.append(vec)
    est = 4 * d * ff + 2 * k * d + 4 * tm * k + 4 * tm * d * 4 + 4 * tm * d * 4 + 3 * tm * MLP_FF_CHUNK * 4
    return pl.pallas_call(
        functools.partial(_mlp_kernel, has_bias=b_o is not None, has_final=final_gain is not None),
        out_shape=jax.ShapeDtypeStruct((m, d), F32),
        grid=(m // tm,),
        in_specs=specs,
        out_specs=pl.BlockSpec((tm, d), row),
        compiler_params=pltpu.CompilerParams(dimension_semantics=("parallel",),
                                             vmem_limit_bytes=_vmem_limit(est)),
        name="sqrelu_mlp",
    )(*args)


Q_SCALE = HEAD_DIM ** -0.5 * LOG2_E
ATTN_PIPE_DEPTH = 2


def _low_head_lanes(shape, interleaved):
    lane = lax.broadcasted_iota(jnp.int32, shape, 1)
    return (lane % HEAD_DIM) < HEAD_DIM // 2 if interleaved else lane < HEAD_DIM


def _stack_two_heads(t, interleaved=False):
    low = _low_head_lanes(t.shape, interleaved)
    zero = jnp.zeros_like(t)
    return jnp.concatenate([jnp.where(low, t, zero), jnp.where(low, zero, t)], axis=0)


def _init_band_consts(bias_ref, ones_ref, max_dist):
    r = lax.broadcasted_iota(jnp.int32, (ATTN_BLOCK, 2 * ATTN_BLOCK), 0)
    c = lax.broadcasted_iota(jnp.int32, (ATTN_BLOCK, 2 * ATTN_BLOCK), 1)
    dist = r + ATTN_BLOCK - c
    ok = (dist >= 0) & (dist <= max_dist)
    bias_ref[0] = jnp.where(ok & (c >= ATTN_BLOCK), 0.0, -jnp.inf)
    bias_ref[1] = jnp.where(ok, 0.0, -jnp.inf)
    low = _low_head_lanes((2 * ATTN_BLOCK, LANES), False)
    ones_ref[0:2 * ATTN_BLOCK, :] = jnp.where(low, 1.0, 0.0).astype(BF16)
    ones_ref[2 * ATTN_BLOCK:, :] = jnp.where(low, 0.0, 1.0).astype(BF16)


def _head_pieces(t, interleaved):
    low = _low_head_lanes(t.shape, interleaved)
    return jnp.where(low, t, 0.0).astype(BF16), jnp.where(low, 0.0, t).astype(BF16)


def _piece_cache(load, interleaved):
    cache = {}

    def get(key):
        if key not in cache:
            cache[key] = _head_pieces(load(key), interleaved)
        return cache[key]
    return get


def _stack_blocks(prev, cur):
    return jnp.concatenate([prev[0], cur[0], prev[1], cur[1]], axis=0)


def _scores(q2, k_stack):
    return lax.dot_general(q2, k_stack, (((1,), (1,)), ((), ())), preferred_element_type=F32)


def _probs(s, bias, sinks):
    kw = 2 * ATTN_BLOCK
    s0, s1 = s[:, :kw] + bias, s[:, kw:] + bias
    m0 = jnp.max(s0, axis=-1, keepdims=True)
    m1 = jnp.max(s1, axis=-1, keepdims=True)
    if sinks is not None:
        m0, m1 = jnp.maximum(m0, sinks[0]), jnp.maximum(m1, sinks[1])
    p2 = jnp.concatenate([jnp.exp2(s0 - m0), jnp.exp2(s1 - m1)], axis=1).astype(BF16)
    return p2, jnp.where(_low_head_lanes((s.shape[0], LANES), False), m0, m1)


def _weighted(p2, m_l, v_stack, ones_stack, sinks):
    od = jnp.dot(p2, jnp.concatenate([v_stack, ones_stack], axis=1), preferred_element_type=F32)
    o, den = od[:, :LANES], od[:, LANES:]
    if sinks is not None:
        den = den + jnp.exp2(jnp.where(_low_head_lanes(o.shape, False), sinks[0], sinks[1]) - m_l)
    return o / den, m_l + jnp.log2(den)


def _run_pipelined(n, first_stage, mid_stage, last_stage):
    ahead = {i: first_stage(i) for i in range(min(ATTN_PIPE_DEPTH, n))}
    for i in range(n):
        mid = mid_stage(i, ahead.pop(i))
        if i + ATTN_PIPE_DEPTH < n:
            ahead[i + ATTN_PIPE_DEPTH] = first_stage(i + ATTN_PIPE_DEPTH)
        last_stage(i, mid)


def _swa_kernel(sink_ref, q_ref, k_ref, v_ref, o_ref, kb_ref, vb_ref, bias_ref, ones_ref):
    seq = q_ref.shape[0]
    p = pl.program_id(1)
    pairs_per_kv = A_Q_PER_KV // 2
    rows = 2 * ATTN_BLOCK

    @pl.when(p == 0)
    def _():
        _init_band_consts(bias_ref, ones_ref, A_WINDOW - 1)

    def spread_kv_head(kv):
        k_low, v_low = _low_head_lanes((rows, LANES), True), _low_head_lanes((rows, LANES), False)
        k_own, v_own = (k_low, v_low) if kv == 0 else (jnp.logical_not(k_low), jnp.logical_not(v_low))
        k_shift = HEAD_DIM // 2 if kv == 0 else LANES - HEAD_DIM // 2
        for c in range(seq // rows):
            sl = slice(c * rows, (c + 1) * rows)
            kt, vt = k_ref[sl, :].astype(F32), v_ref[sl, :].astype(F32)
            kt = jnp.where(k_own, kt, pltpu.roll(kt, k_shift, 1))
            vt = jnp.where(v_own, vt, pltpu.roll(vt, HEAD_DIM, 1))
            for dst, t, low in ((kb_ref, kt, k_low), (vb_ref, vt, v_low)):
                dst[0, sl, :] = jnp.where(low, t, 0.0).astype(BF16)
                dst[1, sl, :] = jnp.where(low, 0.0, t).astype(BF16)

    for kv in range(2):
        pl.when(p == kv * pairs_per_kv)(functools.partial(spread_kv_head, kv))

    sinks = (sink_ref[2 * p] * LOG2_E, sink_ref[2 * p + 1] * LOG2_E)
    blk = lambda i: slice(i * ATTN_BLOCK, (i + 1) * ATTN_BLOCK)

    def stack(ref, i):
        if i == 0:
            return jnp.concatenate([ref[h, blk(0), :] for h in (0, 0, 1, 1)], axis=0)
        both = slice((i - 1) * ATTN_BLOCK, (i + 1) * ATTN_BLOCK)
        return jnp.concatenate([ref[0, both, :], ref[1, both, :]], axis=0)

    def score_stage(i):
        return _scores((q_ref[blk(i), :].astype(F32) * Q_SCALE).astype(BF16), stack(kb_ref, i))

    def prob_stage(i, s):
        return _probs(s, bias_ref[min(i, 1)], sinks)

    def out_stage(i, mid):
        o, _ = _weighted(*mid, stack(vb_ref, i), ones_ref[...], sinks)
        o_ref[blk(i), :] = o.astype(o_ref.dtype)

    _run_pipelined(seq // ATTN_BLOCK, score_stage, prob_stage, out_stage)


def _swa_attention(qkv, sinks, batch, seq):
    m = qkv.shape[0]
    q_dim = sinks.shape[0] * HEAD_DIM
    assert q_dim // HEAD_DIM // A_Q_PER_KV == 2 and seq % (2 * ATTN_BLOCK) == 0
    q3 = qkv.reshape(batch, seq, qkv.shape[1])
    kcol, vcol = q_dim // LANES, q_dim // LANES + 1
    slab = (None, seq, LANES)
    out = pl.pallas_call(
        _swa_kernel,
        out_shape=jax.ShapeDtypeStruct((batch, seq, q_dim), BF16),
        grid=(batch, q_dim // LANES),
        in_specs=[
            pl.BlockSpec(memory_space=pltpu.SMEM),
            pl.BlockSpec(slab, lambda b, p: (b, 0, p)),
            pl.BlockSpec(slab, lambda b, p: (b, 0, kcol)),
            pl.BlockSpec(slab, lambda b, p: (b, 0, vcol)),
        ],
        out_specs=pl.BlockSpec(slab, lambda b, p: (b, 0, p)),
        scratch_shapes=[pltpu.VMEM((2, seq, LANES), BF16), pltpu.VMEM((2, seq, LANES), BF16),
                        pltpu.VMEM((2, ATTN_BLOCK, 2 * ATTN_BLOCK), F32), pltpu.VMEM((4 * ATTN_BLOCK, LANES), BF16)],
        compiler_params=pltpu.CompilerParams(dimension_semantics=("parallel", "arbitrary"),
                                             vmem_limit_bytes=_vmem_limit(16 * seq * LANES * 2)),
        name="swa_sink_attention",
    )(sinks.astype(F32), q3, q3, q3)
    return out.reshape(m, q_dim)


def _dilated_kernel(*refs):
    n_groups = len(C_PATTERNS)
    in_refs, o_ref = refs[:3 * n_groups], refs[3 * n_groups]
    acc_o, acc_l, bias_ref, ones_ref = refs[3 * n_groups + 1:]
    seq = o_ref.shape[0]
    max_dist = C_PATTERNS[0][0] // C_PATTERNS[0][1]
    assert all(w // d == max_dist for w, d in C_PATTERNS)
    _init_band_consts(bias_ref, ones_ref, max_dist)

    blocks = []
    for g, (window, dil) in enumerate(C_PATTERNS):
        for r in range(dil):
            for i in range(seq // dil // ATTN_BLOCK):
                blocks.append((g, dil, r, i))

    def rows(dil, r, i):
        start = i * ATTN_BLOCK * dil + r
        return pl.ds(start, ATTN_BLOCK, stride=dil) if dil > 1 else pl.ds(start, ATTN_BLOCK)

    def loader(which):
        return lambda key: in_refs[3 * key[0] + which][rows(*key[1:]), :]

    k_pieces, v_pieces = _piece_cache(loader(1), True), _piece_cache(loader(2), False)

    def stack(pieces, g, dil, r, i):
        return _stack_blocks(pieces((g, dil, r, max(i - 1, 0))), pieces((g, dil, r, i)))

    def score_stage(n):
        g, dil, r, i = blocks[n]
        q2 = (in_refs[3 * g][rows(dil, r, i), :] * Q_SCALE).astype(BF16)
        return _scores(q2, stack(k_pieces, g, dil, r, i))

    def prob_stage(n, s):
        return _probs(s, bias_ref[min(blocks[n][3], 1)], None)

    def out_stage(n, mid):
        g, dil, r, i = blocks[n]
        cur = rows(dil, r, i)
        o, lse = _weighted(*mid, stack(v_pieces, g, dil, r, i), ones_ref[...], None)
        if g > 0:
            o_old, l_old = acc_o[cur, :], acc_l[cur, :]
            l_max = jnp.maximum(l_old, lse)
            e_old, e_new = jnp.exp2(l_old - l_max), jnp.exp2(lse - l_max)
            inv = 1.0 / (e_old + e_new)
            o = o_old * (e_old * inv) + o * (e_new * inv)
            lse = l_max + jnp.log2(e_old + e_new)
        acc_o[cur, :] = o
        if g < n_groups - 1:
            acc_l[cur, :] = lse

    _run_pipelined(len(blocks), score_stage, prob_stage, out_stage)
    o_ref[...] = acc_o[...].astype(o_ref.dtype)


def _dilated_attention(qkv, batch, seq):
    m, width = qkv.shape
    n_groups = len(C_PATTERNS)
    gw = width // (3 * n_groups)
    assert gw % LANES == 0 and all(seq % (d * ATTN_BLOCK) == 0 for _, d in C_PATTERNS)
    pairs = gw // LANES
    q3 = qkv.reshape(batch, seq, width)
    slab = lambda c: pl.BlockSpec((None, seq, LANES), lambda b, p: (b, 0, c * pairs + p))
    specs = []
    for g in range(n_groups):
        specs += [slab(g), slab(n_groups + g), slab(2 * n_groups + g)]
    out = pl.pallas_call(
        _dilated_kernel,
        out_shape=jax.ShapeDtypeStruct((batch, seq, gw), BF16),
        grid=(batch, pairs),
        in_specs=specs,
        out_specs=pl.BlockSpec((None, seq, LANES), lambda b, p: (b, 0, p)),
        scratch_shapes=[pltpu.VMEM((seq, LANES), F32), pltpu.VMEM((seq, LANES), F32),
                        pltpu.VMEM((2, ATTN_BLOCK, 2 * ATTN_BLOCK), F32), pltpu.VMEM((4 * ATTN_BLOCK, LANES), BF16)],
        compiler_params=pltpu.CompilerParams(
            dimension_semantics=("parallel", "parallel"),
            vmem_limit_bytes=_vmem_limit(2 * (3 * n_groups + 2) * seq * LANES * 4 + 8 * 1024 * 1024)),
        name="dilated_attention",
    )(*([q3] * (3 * n_groups)))
    return out.reshape(m, gw)


def _ssd_kernel(z_ref, x_ref, b_ref, c_ref, dt_ref, cw_ref, cb_ref, dtb_ref, alog_ref, dskip_ref, nw_ref, e_ref,
                o_ref, xpad_ref, state_ref, xs_ref, bm_ref, cm_ref):
    q = SSM_CHUNK
    d_inner = x_ref.shape[1]
    bc_dim = b_ref.shape[1]
    n_heads = d_inner // SSM_HEAD_DIM
    group_w = d_inner // SSM_N_GROUPS
    pad = BF16_ROWS

    @pl.when(pl.program_id(1) == 0)
    def _():
        xpad_ref[0:pad, :] = jnp.zeros((pad, xpad_ref.shape[1]), BF16)
        state_ref[...] = jnp.zeros_like(state_ref)

    xpad_ref[pad:pad + q, 0:d_inner] = x_ref[...]
    xpad_ref[pad:pad + q, d_inner:d_inner + bc_dim] = b_ref[...]
    xpad_ref[pad:pad + q, d_inner + bc_dim:] = c_ref[...]
    out_row = lax.broadcasted_iota(jnp.int32, (q, pad + q), 0)
    src_row = lax.broadcasted_iota(jnp.int32, (q, pad + q), 1)
    shifters = [jnp.where(src_row == pad + out_row - k, 1.0, 0.0).astype(BF16) for k in range(1, SSM_CONV)]
    conv_tile = MXU_COLS
    n_tiles = xpad_ref.shape[1] // conv_tile

    def conv_tiles(t_lo, t_hi):
        for t in range(t_lo, t_hi):
            sl = slice(t * conv_tile, (t + 1) * conv_tile)
            acc = cb_ref[:, sl] + cw_ref[SSM_CONV - 1:SSM_CONV, sl] * xpad_ref[pad:pad + q, sl].astype(F32)
            for k in range(1, SSM_CONV):
                shifted = jnp.dot(shifters[k - 1], xpad_ref[:, sl], preferred_element_type=F32)
                acc = acc + cw_ref[SSM_CONV - 1 - k:SSM_CONV - k, sl] * shifted
            y = acc * _sigmoid(acc)
            lo = t * conv_tile
            if lo < d_inner:
                xs_ref[:, lo:lo + conv_tile] = y
            elif lo < d_inner + bc_dim:
                bm_ref[:, lo - d_inner:lo - d_inner + conv_tile] = y
            else:
                cm_ref[:, lo - d_inner - bc_dim:lo - d_inner - bc_dim + conv_tile] = y.astype(BF16)

    head_lane = lax.broadcasted_iota(jnp.int32, (q, LANES), 1) < n_heads
    dtr = dt_ref[...] + dtb_ref[...]
    dt = jnp.maximum(dtr, 0.0) + jnp.log(1.0 + jnp.exp(-jnp.abs(dtr)))
    dt = jnp.where(head_lane, dt, 0.0)
    d_a = dt * (-jnp.exp(alog_ref[...]))
    rr = lax.broadcasted_iota(jnp.int32, (q, q), 0)
    cc = lax.broadcasted_iota(jnp.int32, (q, q), 1)
    tril = rr >= cc
    tri16 = jnp.where(tril, 1.0, 0.0).astype(BF16)
    d_a_parts = _split_bf16(d_a, 3)
    conv_tiles(0, n_tiles // 4)
    cs = None
    for part in d_a_parts:
        term = jnp.dot(tri16, part, preferred_element_type=F32)
        cs = term if cs is None else cs + term
    conv_tiles(n_tiles // 4, n_tiles // 2)
    cs2 = cs * LOG2_E
    cs2_t = cs2.T
    ecs = jnp.exp(cs)
    dec = jnp.exp(cs[q - 1:q, :] - cs)
    conv_tiles(n_tiles // 2, 3 * n_tiles // 4)
    e16 = e_ref[...]
    dt_e = _expand_heads(dt, e16)
    ecs_e = _expand_heads(ecs, e16)
    dec_e = _expand_heads(dec, e16)
    chunk_decay_e = ecs_e[q - 1:q, :]
    conv_tiles(3 * n_tiles // 4, n_tiles)
    xpad_ref[0:pad, :] = xpad_ref[q:q + pad, :]

    heads_per_group = n_heads // SSM_N_GROUPS
    lanes_of = lambda g: slice(g * group_w, (g + 1) * group_w)

    def state_stage(g):
        gs, ns = lanes_of(g), slice(g * SSM_D_STATE, (g + 1) * SSM_D_STATE)
        b_f = bm_ref[:, ns]
        c16 = cm_ref[:, ns]
        cb = lax.dot_general(c16, b_f.astype(BF16), (((1,), (1,)), ((), ())), preferred_element_type=F32)
        xdt = xs_ref[:, gs] * dt_e[:, gs]
        st_old = state_ref[:, gs]
        y_off = jnp.dot(c16, st_old.astype(BF16), preferred_element_type=F32) * ecs_e[:, gs]
        st_new = jnp.dot(b_f.T.astype(BF16), (xdt * dec_e[:, gs]).astype(BF16), preferred_element_type=F32)
        state_ref[:, gs] = st_old * chunk_decay_e[:, gs] + st_new
        return cb, y_off, xdt.astype(BF16)

    def diag_stage(g, carried):
        cb, y_off, xdt16 = carried
        y_parts = []
        for pr in range(heads_per_group // 2):
            h0 = g * heads_per_group + 2 * pr
            ms = []
            for hh in (h0, h0 + 1):
                diff = cs2[:, hh:hh + 1] - cs2_t[hh:hh + 1, :]
                ms.append(cb * jnp.exp2(jnp.where(tril, diff, -jnp.inf)))
            m2 = jnp.concatenate(ms, axis=1).astype(BF16)
            x_stack = _stack_two_heads(xdt16[:, pr * LANES:(pr + 1) * LANES])
            y_parts.append(jnp.dot(m2, x_stack, preferred_element_type=F32))
        return jnp.concatenate(y_parts, axis=1) + y_off

    def gate_stage(g, y):
        gs = lanes_of(g)
        y = y + dskip_ref[:, gs] * xs_ref[:, gs]
        zf = z_ref[:, gs].astype(F32)
        gated = y * (zf * _sigmoid(zf))
        o_ref[:, gs] = (_rms(gated) * nw_ref[:, gs]).astype(o_ref.dtype)

    _run_pipelined(SSM_N_GROUPS, state_stage, diag_stage, gate_stage)


def _ssd_mixer(zx, dt_raw, conv_w, conv_b, dt_bias, a_log, d_skip, norm_w, batch, seq):
    m = zx.shape[0]
    n_heads = a_log.shape[0]
    d_inner = n_heads * SSM_HEAD_DIM
    bc_dim = SSM_N_GROUPS * SSM_D_STATE
    conv_dim = d_inner + 2 * bc_dim
    assert zx.shape[1] == d_inner + conv_dim and seq % SSM_CHUNK == 0 and n_heads <= LANES
    assert d_inner % bc_dim == 0 and conv_dim % 512 == 0 and d_inner // SSM_N_GROUPS == 2 * LANES
    zx3 = zx.reshape(batch, seq, zx.shape[1])
    dt3 = dt_raw.reshape(batch, seq, LANES)
    pad_heads = lambda v: jnp.pad(v.astype(F32), (0, LANES - n_heads))[None, :]
    blk = lambda w: (None, SSM_CHUNK, w)
    const = lambda shape: pl.BlockSpec(shape, lambda b, c: (0,) * len(shape))
    r = d_inner // bc_dim
    out = pl.pallas_call(
        _ssd_kernel,
        out_shape=jax.ShapeDtypeStruct((batch, seq, d_inner), BF16),
        grid=(batch, seq // SSM_CHUNK),
        in_specs=[
            pl.BlockSpec(blk(d_inner), lambda b, c: (b, c, 0)),
            pl.BlockSpec(blk(d_inner), lambda b, c: (b, c, 1)),
            pl.BlockSpec(blk(bc_dim), lambda b, c: (b, c, 2 * r)),
            pl.BlockSpec(blk(bc_dim), lambda b, c: (b, c, 2 * r + 1)),
            pl.BlockSpec(blk(LANES), lambda b, c: (b, c, 0)),
            const((SSM_CONV, conv_dim)), const((1, conv_dim)), const((1, LANES)), const((1, LANES)),
            const((1, d_inner)), const((1, d_inner)), const((LANES, d_inner)),
        ],
        out_specs=pl.BlockSpec(blk(d_inner), lambda b, c: (b, c, 0)),
        scratch_shapes=[
            pltpu.VMEM((SSM_CHUNK + BF16_ROWS, conv_dim), BF16),
            pltpu.VMEM((SSM_D_STATE, d_inner), F32),
            pltpu.VMEM((SSM_CHUNK, d_inner), F32),
            pltpu.VMEM((SSM_CHUNK, bc_dim), F32),
            pltpu.VMEM((SSM_CHUNK, bc_dim), BF16),
        ],
        compiler_params=pltpu.CompilerParams(dimension_semantics=("parallel", "arbitrary"),
                                             vmem_limit_bytes=_vmem_limit(40 * 1024 * 1024)),
        name="ssd_mixer",
    )(zx3, zx3, zx3, zx3, dt3, conv_w.astype(F32), conv_b.astype(F32)[None, :], pad_heads(dt_bias),
      pad_heads(a_log), jnp.repeat(d_skip.astype(F32), SSM_HEAD_DIM)[None, :], norm_w.astype(F32)[None, :],
      _head_expander(n_heads, SSM_HEAD_DIM))
    return out.reshape(m, d_inner)


def kernel(x, positions, norm_mix_w, norm_mlp_w, a_w_qkv, a_b_qkv, a_sinks, a_w_o, a_b_o, b_in_w, b_conv_w, b_conv_b, b_dt_bias, b_a_log, b_d, b_norm_w, b_out_w, c_w_qkv, c_w_o, mlp_w_up, mlp_w_down, final_norm_w):
    batch, seq, d_model = x.shape
    m = batch * seq
    depth = norm_mix_w.shape[0]
    h = x.reshape(m, d_model).astype(F32)
    rope = _rope_tables(positions.reshape(m, 1))

    for i in range(depth):
        kind, j = i % N_MIXERS, i // N_MIXERS
        if kind == 0:
            q_dim = a_sinks.shape[1] * HEAD_DIM
            kv_dim = (a_w_qkv.shape[2] - q_dim) // 2
            n_rope = q_dim + kv_dim
            qkv = _linear(h, _pair_interleave(a_w_qkv[j], n_rope).astype(BF16), gain=norm_mix_w[i],
                          bias=_pair_interleave(a_b_qkv[j], n_rope), rope=rope, rope_cols=n_rope, name="swa_qkv")
            mix, w_o, b_o = _swa_attention(qkv, a_sinks[j], batch, seq), a_w_o[j], a_b_o[j]
        elif kind == 1:
            n_heads = b_a_log.shape[1]
            w_in = b_in_w[j]
            main = w_in.shape[1] - n_heads
            w_dt = jnp.pad(w_in[:, main:], ((0, 0), (0, LANES - n_heads))).astype(BF16)
            zx, dt_raw = _linear(h, w_in[:, :main].astype(BF16), gain=norm_mix_w[i], side_w=w_dt, tm=512,
                                 name="ssm_in")
            mix = _ssd_mixer(zx, dt_raw, b_conv_w[j], b_conv_b[j], b_dt_bias[j], b_a_log[j], b_d[j], b_norm_w[j],
                             batch, seq)
            w_o, b_o = b_out_w[j], None
        else:
            width = c_w_qkv.shape[2]
            n_rope = 2 * width // 3
            qkv = _linear(h, _pair_interleave(c_w_qkv[j], n_rope).astype(BF16), gain=norm_mix_w[i], rope=rope,
                          rope_cols=n_rope, out_dtype=F32, tm=256, name="dilated_qkv")
            mix, w_o, b_o = _dilated_attention(qkv, batch, seq), c_w_o[j], None
        final_gain = final_norm_w if i == depth - 1 else None
        h = _mix_out_mlp(h, mix, w_o.astype(BF16), b_o, norm_mlp_w[i], mlp_w_up[i].astype(BF16),
                         mlp_w_down[i].astype(BF16), final_gain)
    return h.reshape(batch, seq, d_model).astype(x.dtype)
```

```python
import functools
import math

import jax
import jax.numpy as jnp
from jax import lax
from jax.experimental import pallas as pl
from jax.experimental.pallas import tpu as pltpu

F32 = jnp.float32
BF16 = jnp.bfloat16

N_MIXERS = 3
ATTN_BLOCK = 128
HEAD_DIM = 64
ROPE_THETA = 10000.0
NORM_EPS = 1e-5
A_Q_PER_KV = 8
A_WINDOW = 128
C_PATTERNS = ((128, 1), (512, 4), (2048, 16))
SSM_HEAD_DIM = 64
SSM_N_GROUPS = 8
SSM_D_STATE = 128
SSM_CONV = 4
SSM_CHUNK = 128

LANES = 128
LOG2_E = math.log2(math.e)
SUBLANES = 8
BF16_ROWS = 16
V7X_VMEM_LIMIT_CAP = 56 * 1024 * 1024


def _vmem_limit(nbytes):
    return int(min(max(nbytes, 16 * 1024 * 1024), V7X_VMEM_LIMIT_CAP))


def _row_tile(m):
    for t in (1024, 512, 256, 128):
        if m % t == 0:
            return t
    raise ValueError(f"token count {m} must be a multiple of 128")


def _rms(xf):
    return xf * lax.rsqrt(jnp.mean(xf * xf, axis=-1, keepdims=True) + NORM_EPS)


def _sigmoid(v):
    return 1.0 / (1.0 + jnp.exp2(v * (-LOG2_E)))


def _split_bf16(v, n):
    parts, r = [], v
    for _ in range(n):
        p = r.astype(BF16)
        parts.append(p)
        r = r - p.astype(F32)
    return parts


def _dot_split(parts, rhs):
    acc = None
    for p in parts:
        d = jnp.dot(p, rhs, preferred_element_type=F32)
        acc = d if acc is None else acc + d
    return acc


def _expand_heads(v, e_bf16, terms=2):
    return _dot_split(_split_bf16(v, terms), e_bf16)


def _head_expander(n_heads, head_dim):
    rows = jnp.arange(LANES)[:, None]
    cols = jnp.arange(n_heads * head_dim)[None, :] // head_dim
    return (rows == cols).astype(BF16)


ROPE_PACK = LANES // (HEAD_DIM // 2)


def _rope_table_kernel(pos_ref, inv_ref, sign_ref, cos_ref, sin_ref):
    packed_rows = pos_ref.shape[0]
    width = LANES // ROPE_PACK
    quarter = lax.broadcasted_iota(jnp.int32, (packed_rows, LANES), 1) // width
    pos = pos_ref[...].astype(F32)
    packed_pos = pos[:, ROPE_PACK - 1:ROPE_PACK]
    for a in range(ROPE_PACK - 2, -1, -1):
        packed_pos = jnp.where(quarter == a, pos[:, a:a + 1], packed_pos)
    ang = packed_pos * inv_ref[...]
    for table, out_ref, signed in ((jnp.cos(ang), cos_ref, False), (jnp.sin(ang), sin_ref, True)):
        turned = [table] + [pltpu.roll(table, width * k, 1) for k in range(1, ROPE_PACK)]
        for a in range(ROPE_PACK):
            spread = turned[(ROPE_PACK - 1 - a) % ROPE_PACK]
            for ql in range(ROPE_PACK - 1):
                spread = jnp.where(quarter == ql, turned[(ql - a) % ROPE_PACK], spread)
            if signed:
                spread = spread * sign_ref[...]
            out_ref[pl.ds(a, packed_rows, stride=ROPE_PACK), :] = spread


def _rope_tables(pos_col):
    m = pos_col.shape[0]
    tm = _row_tile(m)
    half = HEAD_DIM // 2
    inv = ROPE_THETA ** (-jnp.arange(half, dtype=F32) / half)
    inv_l = jnp.tile(inv, LANES // half)[None, :]
    sign_l = jnp.where(jnp.arange(LANES) < LANES // 2, -1.0, 1.0).astype(F32)[None, :]
    row = pl.BlockSpec((tm, LANES), lambda i: (i, 0))
    const = pl.BlockSpec((1, LANES), lambda i: (0, 0))
    return pl.pallas_call(
        _rope_table_kernel,
        out_shape=(jax.ShapeDtypeStruct((m, LANES), F32), jax.ShapeDtypeStruct((m, LANES), F32)),
        grid=(m // tm,),
        in_specs=[pl.BlockSpec((tm // ROPE_PACK, ROPE_PACK), lambda i: (i, 0)), const, const],
        out_specs=(row, row),
        compiler_params=pltpu.CompilerParams(dimension_semantics=("parallel",)),
        name="rope_tables",
    )(pos_col.reshape(m // ROPE_PACK, ROPE_PACK), inv_l, sign_l)


def _pair_interleave(w, n_rope):
    lead = w.shape[:-1]
    head = w[..., :n_rope].reshape(*lead, n_rope // LANES, 4, LANES // 4)
    head = jnp.stack([head[..., 0, :], head[..., 2, :], head[..., 1, :], head[..., 3, :]], axis=-2)
    return jnp.concatenate([head.reshape(*lead, n_rope), w[..., n_rope:]], axis=-1)


def _rope_lanes(yc, cos, sin_signed):
    return yc * cos + pltpu.roll(yc, LANES // 2, 1) * sin_signed


MXU_COLS = 256


def _linear_kernel(*refs, has_gain, has_bias, has_res, has_side, rope_cols, q_cols, n_cols, prep_x):
    refs = list(refs)
    x_ref, w_ref = refs.pop(0), refs.pop(0)
    g_ref = refs.pop(0) if has_gain else None
    b_ref = refs.pop(0) if has_bias else None
    cos_ref, sin_ref = (refs.pop(0), refs.pop(0)) if rope_cols else (None, None)
    r_ref = refs.pop(0) if has_res else None
    ws_ref = refs.pop(0) if has_side else None
    o_ref = refs.pop(0)
    side_ref = refs.pop(0) if has_side else None
    j = pl.program_id(1)
    tn = o_ref.shape[1]

    def prepared():
        xf = x_ref[...].astype(F32)
        if has_gain:
            xf = _rms(xf) * g_ref[...]
        return xf.astype(BF16)

    if not prep_x:
        xb = x_ref[...]
    elif tn == n_cols:
        xb = prepared()
    else:
        xn_ref = refs.pop(0)

        @pl.when(j == 0)
        def _():
            xn_ref[...] = prepared()
            if has_side:
                side_ref[...] = jnp.dot(xn_ref[...], ws_ref[...], preferred_element_type=F32)
        xb = xn_ref[...]
    if has_side and (not prep_x or tn == n_cols):
        side_ref[...] = jnp.dot(xb, ws_ref[...], preferred_element_type=F32)

    tiled_rope = rope_cols and tn < n_cols
    if rope_cols:
        cos, sin = cos_ref[...], sin_ref[...]
        if tiled_rope:
            is_rope = j < rope_cols // tn
            cos, sin = jnp.where(is_rope, cos, 1.0), jnp.where(is_rope, sin, 0.0)

    if q_cols:
        assert rope_cols and not tiled_rope and q_cols <= rope_cols
        cos_q, sin_q = cos * Q_SCALE, sin * Q_SCALE

    step = MXU_COLS if tn % MXU_COLS == 0 else LANES
    for c0 in range(0, tn, step):
        acc = jnp.dot(xb, w_ref[:, c0:c0 + step], preferred_element_type=F32)
        if has_bias:
            acc = acc + b_ref[:, c0:c0 + step]
        if has_res:
            acc = acc + r_ref[:, c0:c0 + step]
        for l0 in range(0, step, LANES):
            yc = acc[:, l0:l0 + LANES]
            if rope_cols and (tiled_rope or c0 + l0 < rope_cols):
                yc = _rope_lanes(yc, cos_q, sin_q) if c0 + l0 < q_cols else _rope_lanes(yc, cos, sin)
            o_ref[:, c0 + l0:c0 + l0 + LANES] = yc.astype(o_ref.dtype)


def _linear(x, w, *, gain=None, bias=None, rope=None, rope_cols=0, q_cols=0, residual=None, side_w=None,
            out_dtype=BF16, tm=None, tn=None, name="linear"):
    m, k = x.shape
    n = w.shape[1]
    tm = _row_tile(m) if tm is None else tm
    tn = n if tn is None else tn
    assert m % tm == 0 and n % tn == 0 and tn % LANES == 0
    if rope_cols and rope_cols % tn:
        assert tn == n and rope_cols % LANES == 0
    prep_x = gain is not None or x.dtype != BF16
    row = lambda i, j: (i, 0)
    col = lambda i, j: (0, j)
    const = lambda i, j: (0, 0)
    w_spec = pl.BlockSpec((k, tn), col, pipeline_mode=pl.Buffered(1)) if tn == n else pl.BlockSpec((k, tn), col)
    args, specs = [x, w], [pl.BlockSpec((tm, k), row), w_spec]
    if gain is not None:
        args.append(gain.reshape(1, k).astype(F32))
        specs.append(pl.BlockSpec((1, k), const))
    if bias is not None:
        args.append(bias.reshape(1, n).astype(F32))
        specs.append(pl.BlockSpec((1, tn), col))
    if rope_cols:
        args += list(rope)
        specs += [pl.BlockSpec((tm, LANES), row)] * 2
    if residual is not None:
        args.append(residual)
        specs.append(pl.BlockSpec((tm, tn), lambda i, j: (i, j)))
    out_shape = jax.ShapeDtypeStruct((m, n), out_dtype)
    out_specs = pl.BlockSpec((tm, tn), lambda i, j: (i, j))
    if side_w is not None:
        args.append(side_w)
        specs.append(pl.BlockSpec((k, LANES), const))
        out_shape = (out_shape, jax.ShapeDtypeStruct((m, LANES), F32))
        out_specs = (out_specs, pl.BlockSpec((tm, LANES), row))
    out_bytes = jnp.dtype(out_dtype).itemsize
    est = (2 * tm * k * x.dtype.itemsize + 2 * k * tn * 2 + 2 * tm * tn * out_bytes
           + (2 * tm * tn * 4 if residual is not None else 0) + tm * k * 2 + 2 * tm * k * 4
           + 4 * tm * MXU_COLS * 4 + 6 * tm * LANES * 4)
    return pl.pallas_call(
        functools.partial(_linear_kernel, has_gain=gain is not None, has_bias=bias is not None,
                          has_res=residual is not None, has_side=side_w is not None, rope_cols=rope_cols,
                          q_cols=q_cols, n_cols=n, prep_x=prep_x),
        out_shape=out_shape,
        grid=(m // tm, n // tn),
        in_specs=specs,
        out_specs=out_specs,
        scratch_shapes=[pltpu.VMEM((tm, k), BF16)] if prep_x and tn < n else [],
        compiler_params=pltpu.CompilerParams(dimension_semantics=("parallel", "arbitrary"),
                                             vmem_limit_bytes=_vmem_limit(est)),
        name=name,
    )(*args)


MLP_ROWS = 512
MLP_FF_CHUNK = 512


def _mlp_kernel(*refs, has_bias, has_final):
    refs = list(refs)
    x_ref, a_ref, wo_ref = refs.pop(0), refs.pop(0), refs.pop(0)
    bo_ref = refs.pop(0) if has_bias else None
    g_ref, wu_ref, wd_ref = refs.pop(0), refs.pop(0), refs.pop(0)
    gf_ref = refs.pop(0) if has_final else None
    o_ref = refs.pop(0)
    x = x_ref[...] + jnp.dot(a_ref[...], wo_ref[...], preferred_element_type=F32)
    if has_bias:
        x = x + bo_ref[...]
    xn = (_rms(x) * g_ref[...]).astype(BF16)
    y = x
    ff = wu_ref.shape[1]
    step = MLP_FF_CHUNK if ff % MLP_FF_CHUNK == 0 else ff
    for c0 in range(0, ff, step):
        u = jnp.maximum(jnp.dot(xn, wu_ref[:, c0:c0 + step], preferred_element_type=F32), 0.0)
        y = y + jnp.dot((u * u).astype(BF16), wd_ref[c0:c0 + step, :], preferred_element_type=F32)
    if has_final:
        y = _rms(y) * gf_ref[...]
    o_ref[...] = y


def _mix_out_mlp(h, mix, w_o, b_o, gain, w_up, w_down, final_gain=None):
    m, d = h.shape
    k = mix.shape[1]
    ff = w_up.shape[1]
    tm = MLP_ROWS if m % MLP_ROWS == 0 else _row_tile(m)
    row = lambda i: (i, 0)
    const = lambda i: (0, 0)
    vec = pl.BlockSpec((1, d), const)
    resident = lambda shape: pl.BlockSpec(shape, const, pipeline_mode=pl.Buffered(1))
    args = [h, mix, w_o]
    specs = [pl.BlockSpec((tm, d), row), pl.BlockSpec((tm, k), row), resident((k, d))]
    if b_o is not None:
        args.append(b_o.reshape(1, d).astype(F32))
        specs.append(vec)
    args += [gain.reshape(1, d).astype(F32), w_up, w_down]
    specs += [vec, resident((d, ff)), resident((ff, d))]
    if final_gain is not None:
        args.append(final_gain.reshape(1, d).astype(F32))
        specs.append(vec)
    est = 4 * d * ff + 2 * k * d + 4 * tm * k + 4 * tm * d * 4 + 4 * tm * d * 4 + 3 * tm * MLP_FF_CHUNK * 4
    return pl.pallas_call(
        functools.partial(_mlp_kernel, has_bias=b_o is not None, has_final=final_gain is not None),
        out_shape=jax.ShapeDtypeStruct((m, d), F32),
        grid=(m // tm,),
        in_specs=specs,
        out_specs=pl.BlockSpec((tm, d), row),
        compiler_params=pltpu.CompilerParams(dimension_semantics=("parallel",),
                                             vmem_limit_bytes=_vmem_limit(est)),
        name="sqrelu_mlp",
    )(*args)


Q_SCALE = HEAD_DIM ** -0.5 * LOG2_E
ATTN_PIPE_DEPTH = 2


def _low_head_lanes(shape, interleaved):
    lane = lax.broadcasted_iota(jnp.int32, shape, 1)
    return (lane % HEAD_DIM) < HEAD_DIM // 2 if interleaved else lane < HEAD_DIM


def _stack_two_heads(t, interleaved=False):
    low = _low_head_lanes(t.shape, interleaved)
    zero = jnp.zeros_like(t)
    return jnp.concatenate([jnp.where(low, t, zero), jnp.where(low, zero, t)], axis=0)


def _init_band_consts(bias_ref, ones_ref, max_dist):
    r = lax.broadcasted_iota(jnp.int32, (ATTN_BLOCK, 2 * ATTN_BLOCK), 0)
    c = lax.broadcasted_iota(jnp.int32, (ATTN_BLOCK, 2 * ATTN_BLOCK), 1)
    dist = r + ATTN_BLOCK - c
    bias_ref[...] = jnp.where((dist >= 0) & (dist <= max_dist), 0.0, -jnp.inf)
    low = _low_head_lanes((2 * ATTN_BLOCK, LANES), False)
    ones_ref[0:2 * ATTN_BLOCK, :] = jnp.where(low, 1.0, 0.0).astype(BF16)
    ones_ref[2 * ATTN_BLOCK:, :] = jnp.where(low, 0.0, 1.0).astype(BF16)


def _head_pieces(t, interleaved):
    low = _low_head_lanes(t.shape, interleaved)
    return jnp.where(low, t, 0.0).astype(BF16), jnp.where(low, 0.0, t).astype(BF16)


def _piece_cache(load, interleaved):
    cache = {}

    def get(key):
        if key not in cache:
            cache[key] = _head_pieces(load(key), interleaved)
        return cache[key]
    return get


def _stack_blocks(prev, cur):
    return jnp.concatenate(list(cur) if prev is None else [prev[0], cur[0], prev[1], cur[1]], axis=0)


def _band_bias(bias_ref, first):
    return bias_ref[:, ATTN_BLOCK:] if first else bias_ref[...]


def _ones_stack(ones_ref, first):
    if not first:
        return ones_ref[...]
    half = ones_ref.shape[0] // 2
    return jnp.concatenate([ones_ref[half - ATTN_BLOCK:half, :], ones_ref[2 * half - ATTN_BLOCK:, :]], axis=0)


def _scores(q2, k_stack):
    return lax.dot_general(q2, k_stack, (((1,), (1,)), ((), ())), preferred_element_type=F32)


def _probs(s, bias, sinks):
    kw = s.shape[1] // 2
    s0, s1 = s[:, :kw] + bias, s[:, kw:] + bias
    m0 = jnp.max(s0, axis=-1, keepdims=True)
    m1 = jnp.max(s1, axis=-1, keepdims=True)
    if sinks is not None:
        m0, m1 = jnp.maximum(m0, sinks[0]), jnp.maximum(m1, sinks[1])
    p2 = jnp.concatenate([jnp.exp2(s0 - m0), jnp.exp2(s1 - m1)], axis=1).astype(BF16)
    return p2, jnp.where(_low_head_lanes((s.shape[0], LANES), False), m0, m1)


def _weighted(p2, m_l, v_stack, ones_stack, sinks):
    od = jnp.dot(p2, jnp.concatenate([v_stack, ones_stack], axis=1), preferred_element_type=F32)
    o, den = od[:, :LANES], od[:, LANES:]
    if sinks is not None:
        den = den + jnp.exp2(jnp.where(_low_head_lanes(o.shape, False), sinks[0], sinks[1]) - m_l)
    return o / den, m_l + jnp.log2(den)


def _run_pipelined(n, first_stage, mid_stage, last_stage):
    ahead = {i: first_stage(i) for i in range(min(ATTN_PIPE_DEPTH, n))}
    for i in range(n):
        mid = mid_stage(i, ahead.pop(i))
        if i + ATTN_PIPE_DEPTH < n:
            ahead[i + ATTN_PIPE_DEPTH] = first_stage(i + ATTN_PIPE_DEPTH)
        last_stage(i, mid)


def _swa_kernel(sink_ref, q_ref, k_ref, v_ref, o_ref, kb_ref, vb_ref, bias_ref, ones_ref):
    seq = q_ref.shape[0]
    p = pl.program_id(1)
    pairs_per_kv = A_Q_PER_KV // 2
    rows = 2 * ATTN_BLOCK

    @pl.when(p == 0)
    def _():
        _init_band_consts(bias_ref, ones_ref, A_WINDOW - 1)

    def spread_kv_head(kv):
        k_low, v_low = _low_head_lanes((rows, LANES), True), _low_head_lanes((rows, LANES), False)
        k_own, v_own = (k_low, v_low) if kv == 0 else (jnp.logical_not(k_low), jnp.logical_not(v_low))
        k_shift = HEAD_DIM // 2 if kv == 0 else LANES - HEAD_DIM // 2
        for c in range(seq // rows):
            sl = slice(c * rows, (c + 1) * rows)
            kt, vt = k_ref[sl, :].astype(F32), v_ref[sl, :].astype(F32)
            kt = jnp.where(k_own, kt, pltpu.roll(kt, k_shift, 1))
            vt = jnp.where(v_own, vt, pltpu.roll(vt, HEAD_DIM, 1))
            for dst, t, low in ((kb_ref, kt, k_low), (vb_ref, vt, v_low)):
                dst[0, sl, :] = jnp.where(low, t, 0.0).astype(BF16)
                dst[1, sl, :] = jnp.where(low, 0.0, t).astype(BF16)

    for kv in range(2):
        pl.when(p == kv * pairs_per_kv)(functools.partial(spread_kv_head, kv))

    sinks = (sink_ref[2 * p] * LOG2_E, sink_ref[2 * p + 1] * LOG2_E)
    blk = lambda i: slice(i * ATTN_BLOCK, (i + 1) * ATTN_BLOCK)

    def stack(ref, i):
        both = slice(max(i - 1, 0) * ATTN_BLOCK, (i + 1) * ATTN_BLOCK)
        return jnp.concatenate([ref[0, both, :], ref[1, both, :]], axis=0)

    def score_stage(i):
        return _scores(q_ref[blk(i), :], stack(kb_ref, i))

    def prob_stage(i, s):
        return _probs(s, _band_bias(bias_ref, i == 0), sinks)

    def out_stage(i, mid):
        o, _ = _weighted(*mid, stack(vb_ref, i), _ones_stack(ones_ref, i == 0), sinks)
        o_ref[blk(i), :] = o.astype(o_ref.dtype)

    _run_pipelined(seq // ATTN_BLOCK, score_stage, prob_stage, out_stage)


def _swa_attention(qkv, sinks, batch, seq):
    m = qkv.shape[0]
    q_dim = sinks.shape[0] * HEAD_DIM
    assert q_dim // HEAD_DIM // A_Q_PER_KV == 2 and seq % (2 * ATTN_BLOCK) == 0
    q3 = qkv.reshape(batch, seq, qkv.shape[1])
    kcol, vcol = q_dim // LANES, q_dim // LANES + 1
    slab = (None, seq, LANES)
    out = pl.pallas_call(
        _swa_kernel,
        out_shape=jax.ShapeDtypeStruct((batch, seq, q_dim), BF16),
        grid=(batch, q_dim // LANES),
        in_specs=[
            pl.BlockSpec(memory_space=pltpu.SMEM),
            pl.BlockSpec(slab, lambda b, p: (b, 0, p)),
            pl.BlockSpec(slab, lambda b, p: (b, 0, kcol)),
            pl.BlockSpec(slab, lambda b, p: (b, 0, vcol)),
        ],
        out_specs=pl.BlockSpec(slab, lambda b, p: (b, 0, p)),
        scratch_shapes=[pltpu.VMEM((2, seq, LANES), BF16), pltpu.VMEM((2, seq, LANES), BF16),
                        pltpu.VMEM((ATTN_BLOCK, 2 * ATTN_BLOCK), F32), pltpu.VMEM((4 * ATTN_BLOCK, LANES), BF16)],
        compiler_params=pltpu.CompilerParams(dimension_semantics=("parallel", "arbitrary"),
                                             vmem_limit_bytes=_vmem_limit(16 * seq * LANES * 2)),
        name="swa_sink_attention",
    )(sinks.astype(F32), q3, q3, q3)
    return out.reshape(m, q_dim)


def _dilated_kernel(*refs):
    n_groups = len(C_PATTERNS)
    in_refs, o_ref = refs[:3 * n_groups], refs[3 * n_groups]
    acc_o, acc_l, bias_ref, ones_ref = refs[3 * n_groups + 1:]
    seq = o_ref.shape[0]
    max_dist = C_PATTERNS[0][0] // C_PATTERNS[0][1]
    assert all(w // d == max_dist for w, d in C_PATTERNS)
    _init_band_consts(bias_ref, ones_ref, max_dist)

    blocks = []
    for g, (window, dil) in enumerate(C_PATTERNS):
        for r in range(dil):
            for i in range(seq // dil // ATTN_BLOCK):
                blocks.append((g, dil, r, i))

    def rows(dil, r, i):
        start = i * ATTN_BLOCK * dil + r
        return pl.ds(start, ATTN_BLOCK, stride=dil) if dil > 1 else pl.ds(start, ATTN_BLOCK)

    def loader(which):
        return lambda key: in_refs[3 * key[0] + which][rows(*key[1:]), :]

    k_pieces, v_pieces = _piece_cache(loader(1), True), _piece_cache(loader(2), False)

    def stack(pieces, g, dil, r, i):
        return _stack_blocks(pieces((g, dil, r, i - 1)) if i else None, pieces((g, dil, r, i)))

    def score_stage(n):
        g, dil, r, i = blocks[n]
        q2 = in_refs[3 * g][rows(dil, r, i), :].astype(BF16)
        return _scores(q2, stack(k_pieces, g, dil, r, i))

    def prob_stage(n, s):
        return _probs(s, _band_bias(bias_ref, blocks[n][3] == 0), None)

    def out_stage(n, mid):
        g, dil, r, i = blocks[n]
        cur = rows(dil, r, i)
        o, lse = _weighted(*mid, stack(v_pieces, g, dil, r, i), _ones_stack(ones_ref, i == 0), None)
        if g > 0:
            o_old, l_old = acc_o[cur, :], acc_l[cur, :]
            l_max = jnp.maximum(l_old, lse)
            e_old, e_new = jnp.exp2(l_old - l_max), jnp.exp2(lse - l_max)
            inv = 1.0 / (e_old + e_new)
            o = o_old * (e_old * inv) + o * (e_new * inv)
            lse = l_max + jnp.log2(e_old + e_new)
        acc_o[cur, :] = o
        if g < n_groups - 1:
            acc_l[cur, :] = lse

    _run_pipelined(len(blocks), score_stage, prob_stage, out_stage)
    o_ref[...] = acc_o[...].astype(o_ref.dtype)


def _dilated_attention(qkv, batch, seq):
    m, width = qkv.shape
    n_groups = len(C_PATTERNS)
    gw = width // (3 * n_groups)
    assert gw % LANES == 0 and all(seq % (d * ATTN_BLOCK) == 0 for _, d in C_PATTERNS)
    pairs = gw // LANES
    q3 = qkv.reshape(batch, seq, width)
    slab = lambda c: pl.BlockSpec((None, seq, LANES), lambda b, p: (b, 0, c * pairs + p))
    specs = []
    for g in range(n_groups):
        specs += [slab(g), slab(n_groups + g), slab(2 * n_groups + g)]
    out = pl.pallas_call(
        _dilated_kernel,
        out_shape=jax.ShapeDtypeStruct((batch, seq, gw), BF16),
        grid=(batch, pairs),
        in_specs=specs,
        out_specs=pl.BlockSpec((None, seq, LANES), lambda b, p: (b, 0, p)),
        scratch_shapes=[pltpu.VMEM((seq, LANES), F32), pltpu.VMEM((seq, LANES), F32),
                        pltpu.VMEM((ATTN_BLOCK, 2 * ATTN_BLOCK), F32), pltpu.VMEM((4 * ATTN_BLOCK, LANES), BF16)],
        compiler_params=pltpu.CompilerParams(
            dimension_semantics=("parallel", "parallel"),
            vmem_limit_bytes=_vmem_limit(2 * (3 * n_groups + 2) * seq * LANES * 4 + 8 * 1024 * 1024)),
        name="dilated_attention",
    )(*([q3] * (3 * n_groups)))
    return out.reshape(m, gw)


def _ssd_kernel(z_ref, x_ref, b_ref, c_ref, dt_ref, cw_ref, cb_ref, dtb_ref, alog_ref, dskip_ref, nw_ref, e_ref,
                o_ref, xpad_ref, state_ref, xs_ref, bm_ref, cm_ref):
    q = SSM_CHUNK
    d_inner = x_ref.shape[1]
    bc_dim = b_ref.shape[1]
    n_heads = d_inner // SSM_HEAD_DIM
    group_w = d_inner // SSM_N_GROUPS
    pad = BF16_ROWS

    @pl.when(pl.program_id(1) == 0)
    def _():
        xpad_ref[0:pad, :] = jnp.zeros((pad, xpad_ref.shape[1]), BF16)
        state_ref[...] = jnp.zeros_like(state_ref)

    xpad_ref[pad:pad + q, 0:d_inner] = x_ref[...]
    xpad_ref[pad:pad + q, d_inner:d_inner + bc_dim] = b_ref[...]
    xpad_ref[pad:pad + q, d_inner + bc_dim:] = c_ref[...]
    out_row = lax.broadcasted_iota(jnp.int32, (q, pad + q), 0)
    src_row = lax.broadcasted_iota(jnp.int32, (q, pad + q), 1)
    shifters = [jnp.where(src_row == pad + out_row - k, 1.0, 0.0).astype(BF16) for k in range(1, SSM_CONV)]
    conv_tile = MXU_COLS
    n_tiles = xpad_ref.shape[1] // conv_tile

    def conv_tiles(t_lo, t_hi):
        for t in range(t_lo, t_hi):
            sl = slice(t * conv_tile, (t + 1) * conv_tile)
            acc = cb_ref[:, sl] + cw_ref[SSM_CONV - 1:SSM_CONV, sl] * xpad_ref[pad:pad + q, sl].astype(F32)
            for k in range(1, SSM_CONV):
                shifted = jnp.dot(shifters[k - 1], xpad_ref[:, sl], preferred_element_type=F32)
                acc = acc + cw_ref[SSM_CONV - 1 - k:SSM_CONV - k, sl] * shifted
            y = acc * _sigmoid(acc)
            lo = t * conv_tile
            if lo < d_inner:
                xs_ref[:, lo:lo + conv_tile] = y
            elif lo < d_inner + bc_dim:
                bm_ref[:, lo - d_inner:lo - d_inner + conv_tile] = y
            else:
                cm_ref[:, lo - d_inner - bc_dim:lo - d_inner - bc_dim + conv_tile] = y.astype(BF16)

    head_lane = lax.broadcasted_iota(jnp.int32, (q, LANES), 1) < n_heads
    dtr = dt_ref[...] + dtb_ref[...]
    dt = jnp.maximum(dtr, 0.0) + jnp.log(1.0 + jnp.exp(-jnp.abs(dtr)))
    dt = jnp.where(head_lane, dt, 0.0)
    d_a = dt * (-jnp.exp(alog_ref[...]))
    rr = lax.broadcasted_iota(jnp.int32, (q, q), 0)
    cc = lax.broadcasted_iota(jnp.int32, (q, q), 1)
    tril = rr >= cc
    tri16 = jnp.where(tril, 1.0, 0.0).astype(BF16)
    d_a_parts = _split_bf16(d_a, 3)
    conv_tiles(0, n_tiles // 4)
    cs = None
    for part in d_a_parts:
        term = jnp.dot(tri16, part, preferred_element_type=F32)
        cs = term if cs is None else cs + term
    conv_tiles(n_tiles // 4, n_tiles // 2)
    cs2 = cs * LOG2_E
    cs2_t = cs2.T
    ecs = jnp.exp(cs)
    dec = jnp.exp(cs[q - 1:q, :] - cs)
    conv_tiles(n_tiles // 2, 3 * n_tiles // 4)
    e16 = e_ref[...]
    dt_e = _expand_heads(dt, e16, terms=1)
    ecs_e = _expand_heads(ecs, e16)
    dec_e = _expand_heads(dec, e16, terms=1)
    chunk_decay_e = ecs_e[q - 1:q, :]
    conv_tiles(3 * n_tiles // 4, n_tiles)
    xpad_ref[0:pad, :] = xpad_ref[q:q + pad, :]

    heads_per_group = n_heads // SSM_N_GROUPS
    lanes_of = lambda g: slice(g * group_w, (g + 1) * group_w)

    def state_stage(g):
        gs, ns = lanes_of(g), slice(g * SSM_D_STATE, (g + 1) * SSM_D_STATE)
        b_f = bm_ref[:, ns]
        c16 = cm_ref[:, ns]
        cb = lax.dot_general(c16, b_f.astype(BF16), (((1,), (1,)), ((), ())), preferred_element_type=F32)
        xdt = xs_ref[:, gs] * dt_e[:, gs]
        st_old = state_ref[:, gs]
        y_off = jnp.dot(c16, st_old.astype(BF16), preferred_element_type=F32) * ecs_e[:, gs]
        st_new = jnp.dot(b_f.T.astype(BF16), (xdt * dec_e[:, gs]).astype(BF16), preferred_element_type=F32)
        state_ref[:, gs] = st_old * chunk_decay_e[:, gs] + st_new
        return cb, y_off, xdt.astype(BF16)

    def diag_stage(g, carried):
        cb, y_off, xdt16 = carried
        y_parts = []
        for pr in range(heads_per_group // 2):
            h0 = g * heads_per_group + 2 * pr
            ms = []
            for hh in (h0, h0 + 1):
                diff = cs2[:, hh:hh + 1] - cs2_t[hh:hh + 1, :]
                ms.append(cb * jnp.exp2(jnp.where(tril, diff, -jnp.inf)))
            m2 = jnp.concatenate(ms, axis=1).astype(BF16)
            x_stack = _stack_two_heads(xdt16[:, pr * LANES:(pr + 1) * LANES])
            y_parts.append(jnp.dot(m2, x_stack, preferred_element_type=F32))
        return jnp.concatenate(y_parts, axis=1) + y_off

    def gate_stage(g, y):
        gs = lanes_of(g)
        y = y + dskip_ref[:, gs] * xs_ref[:, gs]
        zf = z_ref[:, gs].astype(F32)
        gated = y * (zf * _sigmoid(zf))
        o_ref[:, gs] = (_rms(gated) * nw_ref[:, gs]).astype(o_ref.dtype)

    _run_pipelined(SSM_N_GROUPS, state_stage, diag_stage, gate_stage)


def _ssd_mixer(zx, dt_raw, conv_w, conv_b, dt_bias, a_log, d_skip, norm_w, batch, seq):
    m = zx.shape[0]
    n_heads = a_log.shape[0]
    d_inner = n_heads * SSM_HEAD_DIM
    bc_dim = SSM_N_GROUPS * SSM_D_STATE
    conv_dim = d_inner + 2 * bc_dim
    assert zx.shape[1] == d_inner + conv_dim and seq % SSM_CHUNK == 0 and n_heads <= LANES
    assert d_inner % bc_dim == 0 and conv_dim % 512 == 0 and d_inner // SSM_N_GROUPS == 2 * LANES
    zx3 = zx.reshape(batch, seq, zx.shape[1])
    dt3 = dt_raw.reshape(batch, seq, LANES)
    pad_heads = lambda v: jnp.pad(v.astype(F32), (0, LANES - n_heads))[None, :]
    blk = lambda w: (None, SSM_CHUNK, w)
    const = lambda shape: pl.BlockSpec(shape, lambda b, c: (0,) * len(shape))
    r = d_inner // bc_dim
    out = pl.pallas_call(
        _ssd_kernel,
        out_shape=jax.ShapeDtypeStruct((batch, seq, d_inner), BF16),
        grid=(batch, seq // SSM_CHUNK),
        in_specs=[
            pl.BlockSpec(blk(d_inner), lambda b, c: (b, c, 0)),
            pl.BlockSpec(blk(d_inner), lambda b, c: (b, c, 1)),
            pl.BlockSpec(blk(bc_dim), lambda b, c: (b, c, 2 * r)),
            pl.BlockSpec(blk(bc_dim), lambda b, c: (b, c, 2 * r + 1)),
            pl.BlockSpec(blk(LANES), lambda b, c: (b, c, 0)),
            const((SSM_CONV, conv_dim)), const((1, conv_dim)), const((1, LANES)), const((1, LANES)),
            const((1, d_inner)), const((1, d_inner)), const((LANES, d_inner)),
        ],
        out_specs=pl.BlockSpec(blk(d_inner), lambda b, c: (b, c, 0)),
        scratch_shapes=[
            pltpu.VMEM((SSM_CHUNK + BF16_ROWS, conv_dim), BF16),
            pltpu.VMEM((SSM_D_STATE, d_inner), F32),
            pltpu.VMEM((SSM_CHUNK, d_inner), F32),
            pltpu.VMEM((SSM_CHUNK, bc_dim), F32),
            pltpu.VMEM((SSM_CHUNK, bc_dim), BF16),
        ],
        compiler_params=pltpu.CompilerParams(dimension_semantics=("parallel", "arbitrary"),
                                             vmem_limit_bytes=_vmem_limit(40 * 1024 * 1024)),
        name="ssd_mixer",
    )(zx3, zx3, zx3, zx3, dt3, conv_w.astype(F32), conv_b.astype(F32)[None, :], pad_heads(dt_bias),
      pad_heads(a_log), jnp.repeat(d_skip.astype(F32), SSM_HEAD_DIM)[None, :], norm_w.astype(F32)[None, :],
      _head_expander(n_heads, SSM_HEAD_DIM))
    return out.reshape(m, d_inner)


def kernel(x, positions, norm_mix_w, norm_mlp_w, a_w_qkv, a_b_qkv, a_sinks, a_w_o, a_b_o, b_in_w, b_conv_w, b_conv_b, b_dt_bias, b_a_log, b_d, b_norm_w, b_out_w, c_w_qkv, c_w_o, mlp_w_up, mlp_w_down, final_norm_w):
    batch, seq, d_model = x.shape
    m = batch * seq
    depth = norm_mix_w.shape[0]
    h = x.reshape(m, d_model).astype(F32)
    rope = _rope_tables(positions.reshape(m, 1))

    for i in range(depth):
        kind, j = i % N_MIXERS, i // N_MIXERS
        if kind == 0:
            q_dim = a_sinks.shape[1] * HEAD_DIM
            kv_dim = (a_w_qkv.shape[2] - q_dim) // 2
            n_rope = q_dim + kv_dim
            qkv = _linear(h, _pair_interleave(a_w_qkv[j], n_rope).astype(BF16), gain=norm_mix_w[i],
                          bias=_pair_interleave(a_b_qkv[j], n_rope), rope=rope, rope_cols=n_rope, q_cols=q_dim,
                          name="swa_qkv")
            mix, w_o, b_o = _swa_attention(qkv, a_sinks[j], batch, seq), a_w_o[j], a_b_o[j]
        elif kind == 1:
            n_heads = b_a_log.shape[1]
            w_in = b_in_w[j]
            main = w_in.shape[1] - n_heads
            w_dt = jnp.pad(w_in[:, main:], ((0, 0), (0, LANES - n_heads))).astype(BF16)
            zx, dt_raw = _linear(h, w_in[:, :main].astype(BF16), gain=norm_mix_w[i], side_w=w_dt, tm=512,
                                 name="ssm_in")
            mix = _ssd_mixer(zx, dt_raw, b_conv_w[j], b_conv_b[j], b_dt_bias[j], b_a_log[j], b_d[j], b_norm_w[j],
                             batch, seq)
            w_o, b_o = b_out_w[j], None
        else:
            width = c_w_qkv.shape[2]
            n_rope = 2 * width // 3
            qkv = _linear(h, _pair_interleave(c_w_qkv[j], n_rope).astype(BF16), gain=norm_mix_w[i], rope=rope,
                          rope_cols=n_rope, q_cols=width // 3, out_dtype=F32, tm=256, name="dilated_qkv")
            mix, w_o, b_o = _dilated_attention(qkv, batch, seq), c_w_o[j], None
        final_gain = final_norm_w if i == depth - 1 else None
        h = _mix_out_mlp(h, mix, w_o.astype(BF16), b_o, norm_mlp_w[i], mlp_w_up[i].astype(BF16),
                         mlp_w_down[i].astype(BF16), final_gain)
    return h.reshape(batch, seq, d_model).astype(x.dtype)
```

```python
import functools
import math

import jax
import jax.numpy as jnp
from jax import lax
from jax.experimental import pallas as pl
from jax.experimental.pallas import tpu as pltpu

F32 = jnp.float32
BF16 = jnp.bfloat16

N_MIXERS = 3
ATTN_BLOCK = 128
HEAD_DIM = 64
ROPE_THETA = 10000.0
NORM_EPS = 1e-5
A_Q_PER_KV = 8
A_WINDOW = 128
C_PATTERNS = ((128, 1), (512, 4), (2048, 16))
SSM_HEAD_DIM = 64
SSM_N_GROUPS = 8
SSM_D_STATE = 128
SSM_CONV = 4
SSM_CHUNK = 128

LANES = 128
LOG2_E = math.log2(math.e)
SUBLANES = 8
BF16_ROWS = 16
V7X_VMEM_LIMIT_CAP = 56 * 1024 * 1024


def _vmem_limit(nbytes):
    return int(min(max(nbytes, 16 * 1024 * 1024), V7X_VMEM_LIMIT_CAP))


def _row_tile(m):
    for t in (1024, 512, 256, 128):
        if m % t == 0:
            return t
    raise ValueError(f"token count {m} must be a multiple of 128")


def _rms(xf):
    return xf * lax.rsqrt(jnp.mean(xf * xf, axis=-1, keepdims=True) + NORM_EPS)


def _sigmoid(v):
    return 1.0 / (1.0 + jnp.exp2(v * (-LOG2_E)))


def _split_bf16(v, n):
    parts, r = [], v
    for _ in range(n):
        p = r.astype(BF16)
        parts.append(p)
        r = r - p.astype(F32)
    return parts


def _dot_split(parts, rhs):
    acc = None
    for p in parts:
        d = jnp.dot(p, rhs, preferred_element_type=F32)
        acc = d if acc is None else acc + d
    return acc


def _expand_heads(v, e_bf16, terms=2):
    return _dot_split(_split_bf16(v, terms), e_bf16)


def _head_expander(n_heads, head_dim):
    rows = jnp.arange(LANES)[:, None]
    cols = jnp.arange(n_heads * head_dim)[None, :] // head_dim
    return (rows == cols).astype(BF16)


ROPE_PACK = LANES // (HEAD_DIM // 2)


def _rope_table_kernel(pos_ref, inv_ref, sign_ref, cos_ref, sin_ref):
    packed_rows = pos_ref.shape[0]
    width = LANES // ROPE_PACK
    quarter = lax.broadcasted_iota(jnp.int32, (packed_rows, LANES), 1) // width
    pos = pos_ref[...].astype(F32)
    packed_pos = pos[:, ROPE_PACK - 1:ROPE_PACK]
    for a in range(ROPE_PACK - 2, -1, -1):
        packed_pos = jnp.where(quarter == a, pos[:, a:a + 1], packed_pos)
    ang = packed_pos * inv_ref[...]
    for table, out_ref, signed in ((jnp.cos(ang), cos_ref, False), (jnp.sin(ang), sin_ref, True)):
        turned = [table] + [pltpu.roll(table, width * k, 1) for k in range(1, ROPE_PACK)]
        for a in range(ROPE_PACK):
            spread = turned[(ROPE_PACK - 1 - a) % ROPE_PACK]
            for ql in range(ROPE_PACK - 1):
                spread = jnp.where(quarter == ql, turned[(ql - a) % ROPE_PACK], spread)
            if signed:
                spread = spread * sign_ref[...]
            out_ref[pl.ds(a, packed_rows, stride=ROPE_PACK), :] = spread


def _rope_tables(pos_col):
    m = pos_col.shape[0]
    tm = _row_tile(m)
    half = HEAD_DIM // 2
    inv = ROPE_THETA ** (-jnp.arange(half, dtype=F32) / half)
    inv_l = jnp.tile(inv, LANES // half)[None, :]
    sign_l = jnp.where(jnp.arange(LANES) < LANES // 2, -1.0, 1.0).astype(F32)[None, :]
    row = pl.BlockSpec((tm, LANES), lambda i: (i, 0))
    const = pl.BlockSpec((1, LANES), lambda i: (0, 0))
    return pl.pallas_call(
        _rope_table_kernel,
        out_shape=(jax.ShapeDtypeStruct((m, LANES), F32), jax.ShapeDtypeStruct((m, LANES), F32)),
        grid=(m // tm,),
        in_specs=[pl.BlockSpec((tm // ROPE_PACK, ROPE_PACK), lambda i: (i, 0)), const, const],
        out_specs=(row, row),
        compiler_params=pltpu.CompilerParams(dimension_semantics=("parallel",)),
        name="rope_tables",
    )(pos_col.reshape(m // ROPE_PACK, ROPE_PACK), inv_l, sign_l)


def _pair_interleave(w, n_rope):
    lead = w.shape[:-1]
    head = w[..., :n_rope].reshape(*lead, n_rope // LANES, 4, LANES // 4)
    head = jnp.stack([head[..., 0, :], head[..., 2, :], head[..., 1, :], head[..., 3, :]], axis=-2)
    return jnp.concatenate([head.reshape(*lead, n_rope), w[..., n_rope:]], axis=-1)


def _rope_lanes(yc, cos, sin_signed):
    return yc * cos + pltpu.roll(yc, LANES // 2, 1) * sin_signed


MXU_COLS = 256


def _linear_kernel(*refs, has_gain, has_bias, has_res, has_side, rope_cols, q_cols, n_cols, prep_x):
    refs = list(refs)
    x_ref, w_ref = refs.pop(0), refs.pop(0)
    g_ref = refs.pop(0) if has_gain else None
    b_ref = refs.pop(0) if has_bias else None
    cos_ref, sin_ref = (refs.pop(0), refs.pop(0)) if rope_cols else (None, None)
    r_ref = refs.pop(0) if has_res else None
    ws_ref = refs.pop(0) if has_side else None
    o_ref = refs.pop(0)
    side_ref = refs.pop(0) if has_side else None
    j = pl.program_id(1)
    tn = o_ref.shape[1]

    def prepared():
        xf = x_ref[...].astype(F32)
        if has_gain:
            xf = _rms(xf) * g_ref[...]
        return xf.astype(BF16)

    if not prep_x:
        xb = x_ref[...]
    elif tn == n_cols:
        xb = prepared()
    else:
        xn_ref = refs.pop(0)

        @pl.when(j == 0)
        def _():
            xn_ref[...] = prepared()
            if has_side:
                side_ref[...] = jnp.dot(xn_ref[...], ws_ref[...], preferred_element_type=F32)
        xb = xn_ref[...]
    if has_side and (not prep_x or tn == n_cols):
        side_ref[...] = jnp.dot(xb, ws_ref[...], preferred_element_type=F32)

    tiled_rope = rope_cols and tn < n_cols
    if rope_cols:
        cos, sin = cos_ref[...], sin_ref[...]
        if tiled_rope:
            is_rope = j < rope_cols // tn
            cos, sin = jnp.where(is_rope, cos, 1.0), jnp.where(is_rope, sin, 0.0)

    if q_cols:
        assert rope_cols and not tiled_rope and q_cols <= rope_cols
        cos_q, sin_q = cos * Q_SCALE, sin * Q_SCALE

    step = MXU_COLS if tn % MXU_COLS == 0 else LANES
    for c0 in range(0, tn, step):
        acc = jnp.dot(xb, w_ref[:, c0:c0 + step], preferred_element_type=F32)
        if has_bias:
            acc = acc + b_ref[:, c0:c0 + step]
        if has_res:
            acc = acc + r_ref[:, c0:c0 + step]
        for l0 in range(0, step, LANES):
            yc = acc[:, l0:l0 + LANES]
            if rope_cols and (tiled_rope or c0 + l0 < rope_cols):
                yc = _rope_lanes(yc, cos_q, sin_q) if c0 + l0 < q_cols else _rope_lanes(yc, cos, sin)
            o_ref[:, c0 + l0:c0 + l0 + LANES] = yc.astype(o_ref.dtype)


def _linear(x, w, *, gain=None, bias=None, rope=None, rope_cols=0, q_cols=0, residual=None, side_w=None,
            out_dtype=BF16, tm=None, tn=None, name="linear"):
    m, k = x.shape
    n = w.shape[1]
    tm = _row_tile(m) if tm is None else tm
    tn = n if tn is None else tn
    assert m % tm == 0 and n % tn == 0 and tn % LANES == 0
    if rope_cols and rope_cols % tn:
        assert tn == n and rope_cols % LANES == 0
    prep_x = gain is not None or x.dtype != BF16
    row = lambda i, j: (i, 0)
    col = lambda i, j: (0, j)
    const = lambda i, j: (0, 0)
    w_spec = pl.BlockSpec((k, tn), col, pipeline_mode=pl.Buffered(1)) if tn == n else pl.BlockSpec((k, tn), col)
    args, specs = [x, w], [pl.BlockSpec((tm, k), row), w_spec]
    if gain is not None:
        args.append(gain.reshape(1, k).astype(F32))
        specs.append(pl.BlockSpec((1, k), const))
    if bias is not None:
        args.append(bias.reshape(1, n).astype(F32))
        specs.append(pl.BlockSpec((1, tn), col))
    if rope_cols:
        args += list(rope)
        specs += [pl.BlockSpec((tm, LANES), row)] * 2
    if residual is not None:
        args.append(residual)
        specs.append(pl.BlockSpec((tm, tn), lambda i, j: (i, j)))
    out_shape = jax.ShapeDtypeStruct((m, n), out_dtype)
    out_specs = pl.BlockSpec((tm, tn), lambda i, j: (i, j))
    if side_w is not None:
        args.append(side_w)
        specs.append(pl.BlockSpec((k, LANES), const))
        out_shape = (out_shape, jax.ShapeDtypeStruct((m, LANES), F32))
        out_specs = (out_specs, pl.BlockSpec((tm, LANES), row))
    out_bytes = jnp.dtype(out_dtype).itemsize
    est = (2 * tm * k * x.dtype.itemsize + 2 * k * tn * 2 + 2 * tm * tn * out_bytes
           + (2 * tm * tn * 4 if residual is not None else 0) + tm * k * 2 + 2 * tm * k * 4
           + 4 * tm * MXU_COLS * 4 + 6 * tm * LANES * 4)
    return pl.pallas_call(
        functools.partial(_linear_kernel, has_gain=gain is not None, has_bias=bias is not None,
                          has_res=residual is not None, has_side=side_w is not None, rope_cols=rope_cols,
                          q_cols=q_cols, n_cols=n, prep_x=prep_x),
        out_shape=out_shape,
        grid=(m // tm, n // tn),
        in_specs=specs,
        out_specs=out_specs,
        scratch_shapes=[pltpu.VMEM((tm, k), BF16)] if prep_x and tn < n else [],
        compiler_params=pltpu.CompilerParams(dimension_semantics=("parallel", "arbitrary"),
                                             vmem_limit_bytes=_vmem_limit(est)),
        name=name,
    )(*args)


MLP_ROWS = 1024
MLP_FF_CHUNK = 512


def _mlp_kernel(*refs, has_bias, has_final):
    refs = list(refs)
    x_ref, a_ref, wo_ref = refs.pop(0), refs.pop(0), refs.pop(0)
    bo_ref = refs.pop(0) if has_bias else None
    g_ref, wu_ref, wd_ref = refs.pop(0), refs.pop(0), refs.pop(0)
    gf_ref = refs.pop(0) if has_final else None
    o_ref = refs.pop(0)
    x = x_ref[...] + jnp.dot(a_ref[...], wo_ref[...], preferred_element_type=F32)
    if has_bias:
        x = x + bo_ref[...]
    xn = (_rms(x) * g_ref[...]).astype(BF16)
    y = x
    ff = wu_ref.shape[1]
    step = MLP_FF_CHUNK if ff % MLP_FF_CHUNK == 0 else ff
    for c0 in range(0, ff, step):
        u = jnp.maximum(jnp.dot(xn, wu_ref[:, c0:c0 + step], preferred_element_type=F32), 0.0)
        y = y + jnp.dot((u * u).astype(BF16), wd_ref[c0:c0 + step, :], preferred_element_type=F32)
    if has_final:
        y = _rms(y) * gf_ref[...]
    o_ref[...] = y


def _mix_out_mlp(h, mix, w_o, b_o, gain, w_up, w_down, final_gain=None):
    m, d = h.shape
    k = mix.shape[1]
    ff = w_up.shape[1]
    tm = MLP_ROWS if m % MLP_ROWS == 0 else _row_tile(m)
    row = lambda i: (i, 0)
    const = lambda i: (0, 0)
    vec = pl.BlockSpec((1, d), const)
    resident = lambda shape: pl.BlockSpec(shape, const, pipeline_mode=pl.Buffered(1))
    args = [h, mix, w_o]
    specs = [pl.BlockSpec((tm, d), row), pl.BlockSpec((tm, k), row), resident((k, d))]
    if b_o is not None:
        args.append(b_o.reshape(1, d).astype(F32))
        specs.append(vec)
    args += [gain.reshape(1, d).astype(F32), w_up, w_down]
    specs += [vec, resident((d, ff)), resident((ff, d))]
    if final_gain is not None:
        args.append(final_gain.reshape(1, d).astype(F32))
        specs.append(vec)
    est = 4 * d * ff + 2 * k * d + 4 * tm * k + 4 * tm * d * 4 + 4 * tm * d * 4 + 3 * tm * MLP_FF_CHUNK * 4
    return pl.pallas_call(
        functools.partial(_mlp_kernel, has_bias=b_o is not None, has_final=final_gain is not None),
        out_shape=jax.ShapeDtypeStruct((m, d), F32),
        grid=(m // tm,),
        in_specs=specs,
        out_specs=pl.BlockSpec((tm, d), row),
        compiler_params=pltpu.CompilerParams(dimension_semantics=("parallel",),
                                             vmem_limit_bytes=_vmem_limit(est)),
        name="sqrelu_mlp",
    )(*args)


Q_SCALE = HEAD_DIM ** -0.5 * LOG2_E
LAST_STAGE_LAG = 1
ATTN_PIPE_DEPTH = 2


def _low_head_lanes(shape, interleaved):
    lane = lax.broadcasted_iota(jnp.int32, shape, 1)
    return (lane % HEAD_DIM) < HEAD_DIM // 2 if interleaved else lane < HEAD_DIM


def _stack_two_heads(t, interleaved=False):
    low = _low_head_lanes(t.shape, interleaved)
    zero = jnp.zeros_like(t)
    return jnp.concatenate([jnp.where(low, t, zero), jnp.where(low, zero, t)], axis=0)


def _init_band_consts(bias_ref, ones_ref, max_dist):
    r = lax.broadcasted_iota(jnp.int32, (ATTN_BLOCK, 2 * ATTN_BLOCK), 0)
    c = lax.broadcasted_iota(jnp.int32, (ATTN_BLOCK, 2 * ATTN_BLOCK), 1)
    dist = r + ATTN_BLOCK - c
    bias_ref[...] = jnp.where((dist >= 0) & (dist <= max_dist), 0.0, -jnp.inf)
    low = _low_head_lanes((2 * ATTN_BLOCK, LANES), False)
    ones_ref[0:2 * ATTN_BLOCK, :] = jnp.where(low, 1.0, 0.0).astype(BF16)
    ones_ref[2 * ATTN_BLOCK:, :] = jnp.where(low, 0.0, 1.0).astype(BF16)


def _head_pieces(t, interleaved):
    low = _low_head_lanes(t.shape, interleaved)
    return jnp.where(low, t, 0.0).astype(BF16), jnp.where(low, 0.0, t).astype(BF16)


def _piece_cache(load, interleaved):
    cache = {}

    def get(key):
        if key not in cache:
            cache[key] = _head_pieces(load(key), interleaved)
        return cache[key]
    return get


def _stack_blocks(prev, cur):
    return jnp.concatenate(list(cur) if prev is None else [prev[0], cur[0], prev[1], cur[1]], axis=0)


def _band_bias(bias_ref, first):
    return bias_ref[:, ATTN_BLOCK:] if first else bias_ref[...]


def _ones_stack(ones_ref, first):
    if not first:
        return ones_ref[...]
    half = ones_ref.shape[0] // 2
    return jnp.concatenate([ones_ref[half - ATTN_BLOCK:half, :], ones_ref[2 * half - ATTN_BLOCK:, :]], axis=0)


def _scores(q2, k_stack):
    return lax.dot_general(q2, k_stack, (((1,), (1,)), ((), ())), preferred_element_type=F32)


def _probs(s, bias, sinks):
    kw = s.shape[1] // 2
    s0, s1 = s[:, :kw] + bias, s[:, kw:] + bias
    m0 = jnp.max(s0, axis=-1, keepdims=True)
    m1 = jnp.max(s1, axis=-1, keepdims=True)
    if sinks is not None:
        m0, m1 = jnp.maximum(m0, sinks[0]), jnp.maximum(m1, sinks[1])
    p2 = jnp.concatenate([jnp.exp2(s0 - m0), jnp.exp2(s1 - m1)], axis=1).astype(BF16)
    return p2, jnp.where(_low_head_lanes((s.shape[0], LANES), False), m0, m1)


def _weighted(p2, m_l, v_stack, ones_stack, sinks):
    od = jnp.dot(p2, jnp.concatenate([v_stack, ones_stack], axis=1), preferred_element_type=F32)
    o, den = od[:, :LANES], od[:, LANES:]
    if sinks is not None:
        den = den + jnp.exp2(jnp.where(_low_head_lanes(o.shape, False), sinks[0], sinks[1]) - m_l)
    return o / den, m_l + jnp.log2(den)


def _run_pipelined(n, first_stage, mid_stage, last_stage):
    ahead = {i: first_stage(i) for i in range(min(ATTN_PIPE_DEPTH, n))}
    mids = {}
    for i in range(n + LAST_STAGE_LAG):
        if i < n:
            mids[i] = mid_stage(i, ahead.pop(i))
            if i + ATTN_PIPE_DEPTH < n:
                ahead[i + ATTN_PIPE_DEPTH] = first_stage(i + ATTN_PIPE_DEPTH)
        if i >= LAST_STAGE_LAG:
            last_stage(i - LAST_STAGE_LAG, mids.pop(i - LAST_STAGE_LAG))


def _swa_kernel(sink_ref, q_ref, k_ref, v_ref, o_ref, kb_ref, vb_ref, bias_ref, ones_ref):
    seq = q_ref.shape[0]
    p = pl.program_id(1)
    pairs_per_kv = A_Q_PER_KV // 2
    rows = 2 * ATTN_BLOCK

    @pl.when(p == 0)
    def _():
        _init_band_consts(bias_ref, ones_ref, A_WINDOW - 1)

    def spread_kv_head(kv):
        k_low, v_low = _low_head_lanes((rows, LANES), True), _low_head_lanes((rows, LANES), False)
        k_own, v_own = (k_low, v_low) if kv == 0 else (jnp.logical_not(k_low), jnp.logical_not(v_low))
        k_shift = HEAD_DIM // 2 if kv == 0 else LANES - HEAD_DIM // 2
        for c in range(seq // rows):
            sl = slice(c * rows, (c + 1) * rows)
            kt, vt = k_ref[sl, :].astype(F32), v_ref[sl, :].astype(F32)
            kt = jnp.where(k_own, kt, pltpu.roll(kt, k_shift, 1))
            vt = jnp.where(v_own, vt, pltpu.roll(vt, HEAD_DIM, 1))
            for dst, t, low in ((kb_ref, kt, k_low), (vb_ref, vt, v_low)):
                dst[0, sl, :] = jnp.where(low, t, 0.0).astype(BF16)
                dst[1, sl, :] = jnp.where(low, 0.0, t).astype(BF16)

    for kv in range(2):
        pl.when(p == kv * pairs_per_kv)(functools.partial(spread_kv_head, kv))

    sinks = (sink_ref[2 * p] * LOG2_E, sink_ref[2 * p + 1] * LOG2_E)
    blk = lambda i: slice(i * ATTN_BLOCK, (i + 1) * ATTN_BLOCK)

    def stack(ref, i):
        both = slice(max(i - 1, 0) * ATTN_BLOCK, (i + 1) * ATTN_BLOCK)
        return jnp.concatenate([ref[0, both, :], ref[1, both, :]], axis=0)

    def score_stage(i):
        return _scores(q_ref[blk(i), :], stack(kb_ref, i))

    def prob_stage(i, s):
        return _probs(s, _band_bias(bias_ref, i == 0), sinks)

    def out_stage(i, mid):
        o, _ = _weighted(*mid, stack(vb_ref, i), _ones_stack(ones_ref, i == 0), sinks)
        o_ref[blk(i), :] = o.astype(o_ref.dtype)

    _run_pipelined(seq // ATTN_BLOCK, score_stage, prob_stage, out_stage)


def _swa_attention(qkv, sinks, batch, seq):
    m = qkv.shape[0]
    q_dim = sinks.shape[0] * HEAD_DIM
    assert q_dim // HEAD_DIM // A_Q_PER_KV == 2 and seq % (2 * ATTN_BLOCK) == 0
    q3 = qkv.reshape(batch, seq, qkv.shape[1])
    kcol, vcol = q_dim // LANES, q_dim // LANES + 1
    slab = (None, seq, LANES)
    out = pl.pallas_call(
        _swa_kernel,
        out_shape=jax.ShapeDtypeStruct((batch, seq, q_dim), BF16),
        grid=(batch, q_dim // LANES),
        in_specs=[
            pl.BlockSpec(memory_space=pltpu.SMEM),
            pl.BlockSpec(slab, lambda b, p: (b, 0, p)),
            pl.BlockSpec(slab, lambda b, p: (b, 0, kcol)),
            pl.BlockSpec(slab, lambda b, p: (b, 0, vcol)),
        ],
        out_specs=pl.BlockSpec(slab, lambda b, p: (b, 0, p)),
        scratch_shapes=[pltpu.VMEM((2, seq, LANES), BF16), pltpu.VMEM((2, seq, LANES), BF16),
                        pltpu.VMEM((ATTN_BLOCK, 2 * ATTN_BLOCK), F32), pltpu.VMEM((4 * ATTN_BLOCK, LANES), BF16)],
        compiler_params=pltpu.CompilerParams(dimension_semantics=("parallel", "arbitrary"),
                                             vmem_limit_bytes=_vmem_limit(16 * seq * LANES * 2)),
        name="swa_sink_attention",
    )(sinks.astype(F32), q3, q3, q3)
    return out.reshape(m, q_dim)


def _dilated_kernel(*refs):
    n_groups = len(C_PATTERNS)
    in_refs, o_ref = refs[:3 * n_groups], refs[3 * n_groups]
    acc_o, acc_l, bias_ref, ones_ref = refs[3 * n_groups + 1:]
    seq = o_ref.shape[0]
    max_dist = C_PATTERNS[0][0] // C_PATTERNS[0][1]
    assert all(w // d == max_dist for w, d in C_PATTERNS)
    _init_band_consts(bias_ref, ones_ref, max_dist)

    order = sorted(range(n_groups), key=lambda g: -C_PATTERNS[g][1])
    blocks = []
    for g in order:
        dil = C_PATTERNS[g][1]
        for r in range(dil):
            for i in range(seq // dil // ATTN_BLOCK):
                blocks.append((g, dil, r, i))

    def rows(dil, r, i):
        start = i * ATTN_BLOCK * dil + r
        return pl.ds(start, ATTN_BLOCK, stride=dil) if dil > 1 else pl.ds(start, ATTN_BLOCK)

    def loader(which):
        return lambda key: in_refs[3 * key[0] + which][rows(*key[1:]), :]

    k_pieces, v_pieces = _piece_cache(loader(1), True), _piece_cache(loader(2), False)

    def stack(pieces, g, dil, r, i):
        return _stack_blocks(pieces((g, dil, r, i - 1)) if i else None, pieces((g, dil, r, i)))

    def score_stage(n):
        g, dil, r, i = blocks[n]
        q2 = in_refs[3 * g][rows(dil, r, i), :].astype(BF16)
        return _scores(q2, stack(k_pieces, g, dil, r, i))

    def prob_stage(n, s):
        return _probs(s, _band_bias(bias_ref, blocks[n][3] == 0), None)

    def out_stage(n, mid):
        g, dil, r, i = blocks[n]
        cur = rows(dil, r, i)
        o, lse = _weighted(*mid, stack(v_pieces, g, dil, r, i), _ones_stack(ones_ref, i == 0), None)
        if g != order[0]:
            o_old, l_old = acc_o[cur, :], acc_l[cur, :]
            l_max = jnp.maximum(l_old, lse)
            e_old, e_new = jnp.exp2(l_old - l_max), jnp.exp2(lse - l_max)
            inv = 1.0 / (e_old + e_new)
            o = o_old * (e_old * inv) + o * (e_new * inv)
            lse = l_max + jnp.log2(e_old + e_new)
        acc_o[cur, :] = o
        if g != order[-1]:
            acc_l[cur, :] = lse

    _run_pipelined(len(blocks), score_stage, prob_stage, out_stage)
    o_ref[...] = acc_o[...].astype(o_ref.dtype)


def _dilated_attention(qkv, batch, seq):
    m, width = qkv.shape
    n_groups = len(C_PATTERNS)
    gw = width // (3 * n_groups)
    assert gw % LANES == 0 and all(seq % (d * ATTN_BLOCK) == 0 for _, d in C_PATTERNS)
    pairs = gw // LANES
    q3 = qkv.reshape(batch, seq, width)
    slab = lambda c: pl.BlockSpec((None, seq, LANES), lambda b, p: (b, 0, c * pairs + p))
    specs = []
    for g in range(n_groups):
        specs += [slab(g), slab(n_groups + g), slab(2 * n_groups + g)]
    out = pl.pallas_call(
        _dilated_kernel,
        out_shape=jax.ShapeDtypeStruct((batch, seq, gw), BF16),
        grid=(batch, pairs),
        in_specs=specs,
        out_specs=pl.BlockSpec((None, seq, LANES), lambda b, p: (b, 0, p)),
        scratch_shapes=[pltpu.VMEM((seq, LANES), F32), pltpu.VMEM((seq, LANES), F32),
                        pltpu.VMEM((ATTN_BLOCK, 2 * ATTN_BLOCK), F32), pltpu.VMEM((4 * ATTN_BLOCK, LANES), BF16)],
        compiler_params=pltpu.CompilerParams(
            dimension_semantics=("parallel", "parallel"),
            vmem_limit_bytes=_vmem_limit(2 * (3 * n_groups + 2) * seq * LANES * 4 + 8 * 1024 * 1024)),
        name="dilated_attention",
    )(*([q3] * (3 * n_groups)))
    return out.reshape(m, gw)


def _ssd_kernel(z_ref, x_ref, b_ref, c_ref, dt_ref, cw_ref, cb_ref, dtb_ref, alog_ref, dskip_ref, nw_ref, e_ref,
                o_ref, xpad_ref, state_ref, xs_ref, bm_ref, cm_ref):
    q = SSM_CHUNK
    d_inner = x_ref.shape[1]
    bc_dim = b_ref.shape[1]
    n_heads = d_inner // SSM_HEAD_DIM
    group_w = d_inner // SSM_N_GROUPS
    pad = BF16_ROWS

    @pl.when(pl.program_id(1) == 0)
    def _():
        xpad_ref[0:pad, :] = jnp.zeros((pad, xpad_ref.shape[1]), BF16)
        state_ref[...] = jnp.zeros_like(state_ref)

    xpad_ref[pad:pad + q, 0:d_inner] = x_ref[...]
    xpad_ref[pad:pad + q, d_inner:d_inner + bc_dim] = b_ref[...]
    xpad_ref[pad:pad + q, d_inner + bc_dim:] = c_ref[...]
    out_row = lax.broadcasted_iota(jnp.int32, (q, pad + q), 0)
    src_row = lax.broadcasted_iota(jnp.int32, (q, pad + q), 1)
    shifters = [jnp.where(src_row == pad + out_row - k, 1.0, 0.0).astype(BF16) for k in range(1, SSM_CONV)]
    conv_tile = MXU_COLS
    n_tiles = xpad_ref.shape[1] // conv_tile

    def conv_tiles(t_lo, t_hi):
        for t in range(t_lo, t_hi):
            sl = slice(t * conv_tile, (t + 1) * conv_tile)
            acc = cb_ref[:, sl] + cw_ref[SSM_CONV - 1:SSM_CONV, sl] * xpad_ref[pad:pad + q, sl].astype(F32)
            for k in range(1, SSM_CONV):
                shifted = jnp.dot(shifters[k - 1], xpad_ref[:, sl], preferred_element_type=F32)
                acc = acc + cw_ref[SSM_CONV - 1 - k:SSM_CONV - k, sl] * shifted
            y = acc * _sigmoid(acc)
            lo = t * conv_tile
            if lo < d_inner:
                xs_ref[:, lo:lo + conv_tile] = y
            elif lo < d_inner + bc_dim:
                bm_ref[:, lo - d_inner:lo - d_inner + conv_tile] = y
            else:
                cm_ref[:, lo - d_inner - bc_dim:lo - d_inner - bc_dim + conv_tile] = y.astype(BF16)

    head_lane = lax.broadcasted_iota(jnp.int32, (q, LANES), 1) < n_heads
    dtr = dt_ref[...] + dtb_ref[...]
    dt = jnp.maximum(dtr, 0.0) + jnp.log(1.0 + jnp.exp(-jnp.abs(dtr)))
    dt = jnp.where(head_lane, dt, 0.0)
    d_a = dt * (-jnp.exp(alog_ref[...]))
    rr = lax.broadcasted_iota(jnp.int32, (q, q), 0)
    cc = lax.broadcasted_iota(jnp.int32, (q, q), 1)
    tril = rr >= cc
    tri16 = jnp.where(tril, 1.0, 0.0).astype(BF16)
    d_a_parts = _split_bf16(d_a, 3)
    conv_tiles(0, n_tiles // 4)
    cs = None
    for part in d_a_parts:
        term = jnp.dot(tri16, part, preferred_element_type=F32)
        cs = term if cs is None else cs + term
    conv_tiles(n_tiles // 4, n_tiles // 2)
    cs2 = cs * LOG2_E
    cs2_t = cs2.T
    ecs = jnp.exp(cs)
    dec = jnp.exp(cs[q - 1:q, :] - cs)
    conv_tiles(n_tiles // 2, 3 * n_tiles // 4)
    e16 = e_ref[...]
    dt_e = _expand_heads(dt, e16, terms=1)
    ecs_e = _expand_heads(ecs, e16)
    dec_e = _expand_heads(dec, e16, terms=1)
    chunk_decay_e = ecs_e[q - 1:q, :]
    conv_tiles(3 * n_tiles // 4, n_tiles)
    xpad_ref[0:pad, :] = xpad_ref[q:q + pad, :]

    heads_per_group = n_heads // SSM_N_GROUPS
    lanes_of = lambda g: slice(g * group_w, (g + 1) * group_w)

    def state_stage(g):
        gs, ns = lanes_of(g), slice(g * SSM_D_STATE, (g + 1) * SSM_D_STATE)
        b_f = bm_ref[:, ns]
        c16 = cm_ref[:, ns]
        cb = lax.dot_general(c16, b_f.astype(BF16), (((1,), (1,)), ((), ())), preferred_element_type=F32)
        xdt = xs_ref[:, gs] * dt_e[:, gs]
        st_old = state_ref[:, gs]
        y_off = jnp.dot(c16, st_old.astype(BF16), preferred_element_type=F32) * ecs_e[:, gs]
        st_new = jnp.dot(b_f.T.astype(BF16), (xdt * dec_e[:, gs]).astype(BF16), preferred_element_type=F32)
        state_ref[:, gs] = st_old * chunk_decay_e[:, gs] + st_new
        return cb, y_off, xdt.astype(BF16)

    def diag_stage(g, carried):
        cb, y_off, xdt16 = carried
        y_parts = []
        for pr in range(heads_per_group // 2):
            h0 = g * heads_per_group + 2 * pr
            ms = []
            for hh in (h0, h0 + 1):
                diff = cs2[:, hh:hh + 1] - cs2_t[hh:hh + 1, :]
                ms.append(cb * jnp.exp2(jnp.where(tril, diff, -jnp.inf)))
            m2 = jnp.concatenate(ms, axis=1).astype(BF16)
            x_stack = _stack_two_heads(xdt16[:, pr * LANES:(pr + 1) * LANES])
            y_parts.append(jnp.dot(m2, x_stack, preferred_element_type=F32))
        return jnp.concatenate(y_parts, axis=1) + y_off

    def gate_stage(g, y):
        gs = lanes_of(g)
        y = y + dskip_ref[:, gs] * xs_ref[:, gs]
        zf = z_ref[:, gs].astype(F32)
        gated = y * (zf * _sigmoid(zf))
        o_ref[:, gs] = (_rms(gated) * nw_ref[:, gs]).astype(o_ref.dtype)

    _run_pipelined(SSM_N_GROUPS, state_stage, diag_stage, gate_stage)


def _ssd_mixer(zx, dt_raw, conv_w, conv_b, dt_bias, a_log, d_skip, norm_w, batch, seq):
    m = zx.shape[0]
    n_heads = a_log.shape[0]
    d_inner = n_heads * SSM_HEAD_DIM
    bc_dim = SSM_N_GROUPS * SSM_D_STATE
    conv_dim = d_inner + 2 * bc_dim
    assert zx.shape[1] == d_inner + conv_dim and seq % SSM_CHUNK == 0 and n_heads <= LANES
    assert d_inner % bc_dim == 0 and conv_dim % 512 == 0 and d_inner // SSM_N_GROUPS == 2 * LANES
    zx3 = zx.reshape(batch, seq, zx.shape[1])
    dt3 = dt_raw.reshape(batch, seq, LANES)
    pad_heads = lambda v: jnp.pad(v.astype(F32), (0, LANES - n_heads))[None, :]
    blk = lambda w: (None, SSM_CHUNK, w)
    const = lambda shape: pl.BlockSpec(shape, lambda b, c: (0,) * len(shape))
    r = d_inner // bc_dim
    out = pl.pallas_call(
        _ssd_kernel,
        out_shape=jax.ShapeDtypeStruct((batch, seq, d_inner), BF16),
        grid=(batch, seq // SSM_CHUNK),
        in_specs=[
            pl.BlockSpec(blk(d_inner), lambda b, c: (b, c, 0)),
            pl.BlockSpec(blk(d_inner), lambda b, c: (b, c, 1)),
            pl.BlockSpec(blk(bc_dim), lambda b, c: (b, c, 2 * r)),
            pl.BlockSpec(blk(bc_dim), lambda b, c: (b, c, 2 * r + 1)),
            pl.BlockSpec(blk(LANES), lambda b, c: (b, c, 0)),
            const((SSM_CONV, conv_dim)), const((1, conv_dim)), const((1, LANES)), const((1, LANES)),
            const((1, d_inner)), const((1, d_inner)), const((LANES, d_inner)),
        ],
        out_specs=pl.BlockSpec(blk(d_inner), lambda b, c: (b, c, 0)),
        scratch_shapes=[
            pltpu.VMEM((SSM_CHUNK + BF16_ROWS, conv_dim), BF16),
            pltpu.VMEM((SSM_D_STATE, d_inner), F32),
            pltpu.VMEM((SSM_CHUNK, d_inner), F32),
            pltpu.VMEM((SSM_CHUNK, bc_dim), F32),
            pltpu.VMEM((SSM_CHUNK, bc_dim), BF16),
        ],
        compiler_params=pltpu.CompilerParams(dimension_semantics=("parallel", "arbitrary"),
                                             vmem_limit_bytes=_vmem_limit(40 * 1024 * 1024)),
        name="ssd_mixer",
    )(zx3, zx3, zx3, zx3, dt3, conv_w.astype(F32), conv_b.astype(F32)[None, :], pad_heads(dt_bias),
      pad_heads(a_log), jnp.repeat(d_skip.astype(F32), SSM_HEAD_DIM)[None, :], norm_w.astype(F32)[None, :],
      _head_expander(n_heads, SSM_HEAD_DIM))
    return out.reshape(m, d_inner)


def kernel(x, positions, norm_mix_w, norm_mlp_w, a_w_qkv, a_b_qkv, a_sinks, a_w_o, a_b_o, b_in_w, b_conv_w, b_conv_b, b_dt_bias, b_a_log, b_d, b_norm_w, b_out_w, c_w_qkv, c_w_o, mlp_w_up, mlp_w_down, final_norm_w):
    batch, seq, d_model = x.shape
    m = batch * seq
    depth = norm_mix_w.shape[0]
    h = x.reshape(m, d_model).astype(F32)
    rope = _rope_tables(positions.reshape(m, 1))

    for i in range(depth):
        kind, j = i % N_MIXERS, i // N_MIXERS
        if kind == 0:
            q_dim = a_sinks.shape[1] * HEAD_DIM
            kv_dim = (a_w_qkv.shape[2] - q_dim) // 2
            n_rope = q_dim + kv_dim
            qkv = _linear(h, _pair_interleave(a_w_qkv[j], n_rope).astype(BF16), gain=norm_mix_w[i],
                          bias=_pair_interleave(a_b_qkv[j], n_rope), rope=rope, rope_cols=n_rope, q_cols=q_dim,
                          name="swa_qkv")
            mix, w_o, b_o = _swa_attention(qkv, a_sinks[j], batch, seq), a_w_o[j], a_b_o[j]
        elif kind == 1:
            n_heads = b_a_log.shape[1]
            w_in = b_in_w[j]
            main = w_in.shape[1] - n_heads
            w_dt = jnp.pad(w_in[:, main:], ((0, 0), (0, LANES - n_heads))).astype(BF16)
            zx, dt_raw = _linear(h, w_in[:, :main].astype(BF16), gain=norm_mix_w[i], side_w=w_dt, tm=512,
                                 name="ssm_in")
            mix = _ssd_mixer(zx, dt_raw, b_conv_w[j], b_conv_b[j], b_dt_bias[j], b_a_log[j], b_d[j], b_norm_w[j],
                             batch, seq)
            w_o, b_o = b_out_w[j], None
        else:
            width = c_w_qkv.shape[2]
            n_rope = 2 * width // 3
            qkv = _linear(h, _pair_interleave(c_w_qkv[j], n_rope).astype(BF16), gain=norm_mix_w[i], rope=rope,
                          rope_cols=n_rope, q_cols=width // 3, out_dtype=F32, tm=256, name="dilated_qkv")
            mix, w_o, b_o = _dilated_attention(qkv, batch, seq), c_w_o[j], None
        final_gain = final_norm_w if i == depth - 1 else None
        h = _mix_out_mlp(h, mix, w_o.astype(BF16), b_o, norm_mlp_w[i], mlp_w_up[i].astype(BF16),
                         mlp_w_down[i].astype(BF16), final_gain)
    return h.reshape(batch, seq, d_model).astype(x.dtype)
```

```python
import functools
import math

import jax
import jax.numpy as jnp
from jax import lax
from jax.experimental import pallas as pl
from jax.experimental.pallas import tpu as pltpu

F32 = jnp.float32
BF16 = jnp.bfloat16

N_MIXERS = 3
ATTN_BLOCK = 128
HEAD_DIM = 64
ROPE_THETA = 10000.0
NORM_EPS = 1e-5
A_Q_PER_KV = 8
A_WINDOW = 128
C_PATTERNS = ((128, 1), (512, 4), (2048, 16))
SSM_HEAD_DIM = 64
SSM_N_GROUPS = 8
SSM_D_STATE = 128
SSM_CONV = 4
SSM_CHUNK = 128

LANES = 128
LOG2_E = math.log2(math.e)
SUBLANES = 8
V7X_VMEM_LIMIT_CAP = 56 * 1024 * 1024


def _vmem_limit(nbytes):
    return int(min(max(nbytes, 16 * 1024 * 1024), V7X_VMEM_LIMIT_CAP))


def _row_tile(m):
    for t in (1024, 512, 256, 128):
        if m % t == 0:
            return t
    raise ValueError(f"token count {m} must be a multiple of 128")


def _rms(xf):
    return xf * lax.rsqrt(jnp.mean(xf * xf, axis=-1, keepdims=True) + NORM_EPS)


def _sigmoid(v):
    return 1.0 / (1.0 + jnp.exp2(v * (-LOG2_E)))


def _split_bf16(v, n):
    parts, r = [], v
    for _ in range(n):
        p = r.astype(BF16)
        parts.append(p)
        r = r - p.astype(F32)
    return parts


def _dot_split(parts, rhs):
    acc = None
    for p in parts:
        d = jnp.dot(p, rhs, preferred_element_type=F32)
        acc = d if acc is None else acc + d
    return acc


def _expand_heads(v, e_bf16, terms=2):
    return _dot_split(_split_bf16(v, terms), e_bf16)


def _head_expander(n_heads, head_dim):
    rows = jnp.arange(LANES)[:, None]
    cols = jnp.arange(n_heads * head_dim)[None, :] // head_dim
    return (rows == cols).astype(BF16)


ROPE_PACK = LANES // (HEAD_DIM // 2)


def _rope_table_kernel(pos_ref, inv_ref, sign_ref, cos_ref, sin_ref):
    packed_rows = pos_ref.shape[0]
    width = LANES // ROPE_PACK
    quarter = lax.broadcasted_iota(jnp.int32, (packed_rows, LANES), 1) // width
    pos = pos_ref[...].astype(F32)
    packed_pos = pos[:, ROPE_PACK - 1:ROPE_PACK]
    for a in range(ROPE_PACK - 2, -1, -1):
        packed_pos = jnp.where(quarter == a, pos[:, a:a + 1], packed_pos)
    ang = packed_pos * inv_ref[...]
    for table, out_ref, signed in ((jnp.cos(ang), cos_ref, False), (jnp.sin(ang), sin_ref, True)):
        turned = [table] + [pltpu.roll(table, width * k, 1) for k in range(1, ROPE_PACK)]
        for a in range(ROPE_PACK):
            spread = turned[(ROPE_PACK - 1 - a) % ROPE_PACK]
            for ql in range(ROPE_PACK - 1):
                spread = jnp.where(quarter == ql, turned[(ql - a) % ROPE_PACK], spread)
            if signed:
                spread = spread * sign_ref[...]
            out_ref[pl.ds(a, packed_rows, stride=ROPE_PACK), :] = spread


def _rope_tables(pos_col):
    m = pos_col.shape[0]
    tm = _row_tile(m)
    half = HEAD_DIM // 2
    inv = ROPE_THETA ** (-jnp.arange(half, dtype=F32) / half)
    inv_l = jnp.tile(inv, LANES // half)[None, :]
    sign_l = jnp.where(jnp.arange(LANES) < LANES // 2, -1.0, 1.0).astype(F32)[None, :]
    row = pl.BlockSpec((tm, LANES), lambda i: (i, 0))
    const = pl.BlockSpec((1, LANES), lambda i: (0, 0))
    return pl.pallas_call(
        _rope_table_kernel,
        out_shape=(jax.ShapeDtypeStruct((m, LANES), F32), jax.ShapeDtypeStruct((m, LANES), F32)),
        grid=(m // tm,),
        in_specs=[pl.BlockSpec((tm // ROPE_PACK, ROPE_PACK), lambda i: (i, 0)), const, const],
        out_specs=(row, row),
        compiler_params=pltpu.CompilerParams(dimension_semantics=("parallel",)),
        name="rope_tables",
    )(pos_col.reshape(m // ROPE_PACK, ROPE_PACK), inv_l, sign_l)


def _pair_interleave(w, n_rope):
    lead = w.shape[:-1]
    head = w[..., :n_rope].reshape(*lead, n_rope // LANES, 4, LANES // 4)
    head = jnp.stack([head[..., 0, :], head[..., 2, :], head[..., 1, :], head[..., 3, :]], axis=-2)
    return jnp.concatenate([head.reshape(*lead, n_rope), w[..., n_rope:]], axis=-1)


def _rope_lanes(yc, cos, sin_signed):
    return yc * cos + pltpu.roll(yc, LANES // 2, 1) * sin_signed


MXU_COLS = 256


def _linear_kernel(*refs, has_gain, has_bias, has_res, has_side, rope_cols, q_cols, n_cols, prep_x):
    refs = list(refs)
    x_ref, w_ref = refs.pop(0), refs.pop(0)
    g_ref = refs.pop(0) if has_gain else None
    b_ref = refs.pop(0) if has_bias else None
    cos_ref, sin_ref = (refs.pop(0), refs.pop(0)) if rope_cols else (None, None)
    r_ref = refs.pop(0) if has_res else None
    ws_ref = refs.pop(0) if has_side else None
    o_ref = refs.pop(0)
    side_ref = refs.pop(0) if has_side else None
    j = pl.program_id(1)
    tn = o_ref.shape[1]

    def prepared():
        xf = x_ref[...].astype(F32)
        if has_gain:
            xf = _rms(xf) * g_ref[...]
        return xf.astype(BF16)

    if not prep_x:
        xb = x_ref[...]
    elif tn == n_cols:
        xb = prepared()
    else:
        xn_ref = refs.pop(0)

        @pl.when(j == 0)
        def _():
            xn_ref[...] = prepared()
            if has_side:
                side_ref[...] = jnp.dot(xn_ref[...], ws_ref[...], preferred_element_type=F32)
        xb = xn_ref[...]
    if has_side and (not prep_x or tn == n_cols):
        side_ref[...] = jnp.dot(xb, ws_ref[...], preferred_element_type=F32)

    tiled_rope = rope_cols and tn < n_cols
    if rope_cols:
        cos, sin = cos_ref[...], sin_ref[...]
        if tiled_rope:
            is_rope = j < rope_cols // tn
            cos, sin = jnp.where(is_rope, cos, 1.0), jnp.where(is_rope, sin, 0.0)

    if q_cols:
        assert rope_cols and not tiled_rope and q_cols <= rope_cols
        cos_q, sin_q = cos * Q_SCALE, sin * Q_SCALE

    step = MXU_COLS if tn % MXU_COLS == 0 else LANES
    for c0 in range(0, tn, step):
        acc = jnp.dot(xb, w_ref[:, c0:c0 + step], preferred_element_type=F32)
        if has_bias:
            acc = acc + b_ref[:, c0:c0 + step]
        if has_res:
            acc = acc + r_ref[:, c0:c0 + step]
        for l0 in range(0, step, LANES):
            yc = acc[:, l0:l0 + LANES]
            if rope_cols and (tiled_rope or c0 + l0 < rope_cols):
                yc = _rope_lanes(yc, cos_q, sin_q) if c0 + l0 < q_cols else _rope_lanes(yc, cos, sin)
            o_ref[:, c0 + l0:c0 + l0 + LANES] = yc.astype(o_ref.dtype)


def _linear(x, w, *, gain=None, bias=None, rope=None, rope_cols=0, q_cols=0, residual=None, side_w=None,
            out_dtype=BF16, tm=None, tn=None, name="linear"):
    m, k = x.shape
    n = w.shape[1]
    tm = _row_tile(m) if tm is None else tm
    tn = n if tn is None else tn
    assert m % tm == 0 and n % tn == 0 and tn % LANES == 0
    if rope_cols and rope_cols % tn:
        assert tn == n and rope_cols % LANES == 0
    prep_x = gain is not None or x.dtype != BF16
    row = lambda i, j: (i, 0)
    col = lambda i, j: (0, j)
    const = lambda i, j: (0, 0)
    w_spec = pl.BlockSpec((k, tn), col, pipeline_mode=pl.Buffered(1)) if tn == n else pl.BlockSpec((k, tn), col)
    args, specs = [x, w], [pl.BlockSpec((tm, k), row), w_spec]
    if gain is not None:
        args.append(gain.reshape(1, k).astype(F32))
        specs.append(pl.BlockSpec((1, k), const))
    if bias is not None:
        args.append(bias.reshape(1, n).astype(F32))
        specs.append(pl.BlockSpec((1, tn), col))
    if rope_cols:
        args += list(rope)
        specs += [pl.BlockSpec((tm, LANES), row)] * 2
    if residual is not None:
        args.append(residual)
        specs.append(pl.BlockSpec((tm, tn), lambda i, j: (i, j)))
    out_shape = jax.ShapeDtypeStruct((m, n), out_dtype)
    out_specs = pl.BlockSpec((tm, tn), lambda i, j: (i, j))
    if side_w is not None:
        args.append(side_w)
        specs.append(pl.BlockSpec((k, LANES), const))
        out_shape = (out_shape, jax.ShapeDtypeStruct((m, LANES), F32))
        out_specs = (out_specs, pl.BlockSpec((tm, LANES), row))
    out_bytes = jnp.dtype(out_dtype).itemsize
    est = (2 * tm * k * x.dtype.itemsize + 2 * k * tn * 2 + 2 * tm * tn * out_bytes
           + (2 * tm * tn * 4 if residual is not None else 0) + tm * k * 2 + 2 * tm * k * 4
           + 4 * tm * MXU_COLS * 4 + 6 * tm * LANES * 4)
    return pl.pallas_call(
        functools.partial(_linear_kernel, has_gain=gain is not None, has_bias=bias is not None,
                          has_res=residual is not None, has_side=side_w is not None, rope_cols=rope_cols,
                          q_cols=q_cols, n_cols=n, prep_x=prep_x),
        out_shape=out_shape,
        grid=(m // tm, n // tn),
        in_specs=specs,
        out_specs=out_specs,
        scratch_shapes=[pltpu.VMEM((tm, k), BF16)] if prep_x and tn < n else [],
        compiler_params=pltpu.CompilerParams(dimension_semantics=("parallel", "arbitrary"),
                                             vmem_limit_bytes=_vmem_limit(est)),
        name=name,
    )(*args)


MLP_ROWS = 1024
MLP_FF_CHUNK = 512


def _mlp_kernel(*refs, has_bias, has_final):
    refs = list(refs)
    x_ref, a_ref, wo_ref = refs.pop(0), refs.pop(0), refs.pop(0)
    bo_ref = refs.pop(0) if has_bias else None
    g_ref, wu_ref, wd_ref = refs.pop(0), refs.pop(0), refs.pop(0)
    gf_ref = refs.pop(0) if has_final else None
    o_ref = refs.pop(0)
    x = x_ref[...] + jnp.dot(a_ref[...], wo_ref[...], preferred_element_type=F32)
    if has_bias:
        x = x + bo_ref[...]
    xn = (_rms(x) * g_ref[...]).astype(BF16)
    y = x
    ff = wu_ref.shape[1]
    step = MLP_FF_CHUNK if ff % MLP_FF_CHUNK == 0 else ff
    for c0 in range(0, ff, step):
        u = jnp.maximum(jnp.dot(xn, wu_ref[:, c0:c0 + step], preferred_element_type=F32), 0.0)
        y = y + jnp.dot((u * u).astype(BF16), wd_ref[c0:c0 + step, :], preferred_element_type=F32)
    if has_final:
        y = _rms(y) * gf_ref[...]
    o_ref[...] = y


def _mix_out_mlp(h, mix, w_o, b_o, gain, w_up, w_down, final_gain=None):
    m, d = h.shape
    k = mix.shape[1]
    ff = w_up.shape[1]
    tm = MLP_ROWS if m % MLP_ROWS == 0 else _row_tile(m)
    row = lambda i: (i, 0)
    const = lambda i: (0, 0)
    vec = pl.BlockSpec((1, d), const)
    resident = lambda shape: pl.BlockSpec(shape, const, pipeline_mode=pl.Buffered(1))
    args = [h, mix, w_o]
    specs = [pl.BlockSpec((tm, d), row), pl.BlockSpec((tm, k), row), resident((k, d))]
    if b_o is not None:
        args.append(b_o.reshape(1, d).astype(F32))
        specs.append(vec)
    args += [gain.reshape(1, d).astype(F32), w_up, w_down]
    specs += [vec, resident((d, ff)), resident((ff, d))]
    if final_gain is not None:
        args.append(final_gain.reshape(1, d).astype(F32))
        specs.append(vec)
    est = 4 * d * ff + 2 * k * d + 4 * tm * k + 4 * tm * d * 4 + 4 * tm * d * 4 + 3 * tm * MLP_FF_CHUNK * 4
    return pl.pallas_call(
        functools.partial(_mlp_kernel, has_bias=b_o is not None, has_final=final_gain is not None),
        out_shape=jax.ShapeDtypeStruct((m, d), F32),
        grid=(m // tm,),
        in_specs=specs,
        out_specs=pl.BlockSpec((tm, d), row),
        compiler_params=pltpu.CompilerParams(dimension_semantics=("parallel",),
                                             vmem_limit_bytes=_vmem_limit(est)),
        name="sqrelu_mlp",
    )(*args)


Q_SCALE = HEAD_DIM ** -0.5 * LOG2_E
LAST_STAGE_LAG = 1
ATTN_PIPE_DEPTH = 2


def _low_head_lanes(shape, interleaved):
    lane = lax.broadcasted_iota(jnp.int32, shape, 1)
    return (lane % HEAD_DIM) < HEAD_DIM // 2 if interleaved else lane < HEAD_DIM


def _stack_two_heads(t, interleaved=False):
    low = _low_head_lanes(t.shape, interleaved)
    zero = jnp.zeros_like(t)
    return jnp.concatenate([jnp.where(low, t, zero), jnp.where(low, zero, t)], axis=0)


def _init_band_consts(bias_ref, ones_ref, max_dist):
    r = lax.broadcasted_iota(jnp.int32, (ATTN_BLOCK, 2 * ATTN_BLOCK), 0)
    c = lax.broadcasted_iota(jnp.int32, (ATTN_BLOCK, 2 * ATTN_BLOCK), 1)
    dist = r + ATTN_BLOCK - c
    bias_ref[...] = jnp.where((dist >= 0) & (dist <= max_dist), 0.0, -jnp.inf)
    low = _low_head_lanes((2 * ATTN_BLOCK, LANES), False)
    ones_ref[0:2 * ATTN_BLOCK, :] = jnp.where(low, 1.0, 0.0).astype(BF16)
    ones_ref[2 * ATTN_BLOCK:, :] = jnp.where(low, 0.0, 1.0).astype(BF16)


def _head_pieces(t, interleaved):
    low = _low_head_lanes(t.shape, interleaved)
    return jnp.where(low, t, 0.0).astype(BF16), jnp.where(low, 0.0, t).astype(BF16)


def _piece_cache(load, interleaved):
    cache = {}

    def get(key):
        if key not in cache:
            cache[key] = _head_pieces(load(key), interleaved)
        return cache[key]
    return get


def _stack_blocks(prev, cur):
    return jnp.concatenate(list(cur) if prev is None else [prev[0], cur[0], prev[1], cur[1]], axis=0)


def _band_bias(bias_ref, first):
    return bias_ref[:, ATTN_BLOCK:] if first else bias_ref[...]


def _ones_stack(ones_ref, first):
    if not first:
        return ones_ref[...]
    half = ones_ref.shape[0] // 2
    return jnp.concatenate([ones_ref[half - ATTN_BLOCK:half, :], ones_ref[2 * half - ATTN_BLOCK:, :]], axis=0)


def _scores(q2, k_stack):
    return lax.dot_general(q2, k_stack, (((1,), (1,)), ((), ())), preferred_element_type=F32)


def _probs(s, bias, sinks):
    kw = s.shape[1] // 2
    s0, s1 = s[:, :kw] + bias, s[:, kw:] + bias
    m0 = jnp.max(s0, axis=-1, keepdims=True)
    m1 = jnp.max(s1, axis=-1, keepdims=True)
    if sinks is not None:
        m0, m1 = jnp.maximum(m0, sinks[0]), jnp.maximum(m1, sinks[1])
    p2 = jnp.concatenate([jnp.exp2(s0 - m0), jnp.exp2(s1 - m1)], axis=1).astype(BF16)
    return p2, jnp.where(_low_head_lanes((s.shape[0], LANES), False), m0, m1)


def _weighted(p2, m_l, v_stack, ones_stack, sinks):
    od = jnp.dot(p2, jnp.concatenate([v_stack, ones_stack], axis=1), preferred_element_type=F32)
    o, den = od[:, :LANES], od[:, LANES:]
    if sinks is not None:
        den = den + jnp.exp2(jnp.where(_low_head_lanes(o.shape, False), sinks[0], sinks[1]) - m_l)
    return o / den, m_l + jnp.log2(den)


def _run_pipelined(n, first_stage, mid_stage, last_stage):
    ahead = {i: first_stage(i) for i in range(min(ATTN_PIPE_DEPTH, n))}
    mids = {}
    for i in range(n + LAST_STAGE_LAG):
        if i < n:
            mids[i] = mid_stage(i, ahead.pop(i))
            if i + ATTN_PIPE_DEPTH < n:
                ahead[i + ATTN_PIPE_DEPTH] = first_stage(i + ATTN_PIPE_DEPTH)
        if i >= LAST_STAGE_LAG:
            last_stage(i - LAST_STAGE_LAG, mids.pop(i - LAST_STAGE_LAG))


def _swa_kernel(sink_ref, q_ref, k_ref, v_ref, o_ref, kb_ref, vb_ref, bias_ref, ones_ref):
    seq = q_ref.shape[0]
    p = pl.program_id(1)
    pairs_per_kv = A_Q_PER_KV // 2
    rows = 2 * ATTN_BLOCK

    @pl.when(p == 0)
    def _():
        _init_band_consts(bias_ref, ones_ref, A_WINDOW - 1)

    def spread_kv_head(kv):
        k_low, v_low = _low_head_lanes((rows, LANES), True), _low_head_lanes((rows, LANES), False)
        k_own, v_own = (k_low, v_low) if kv == 0 else (jnp.logical_not(k_low), jnp.logical_not(v_low))
        k_shift = HEAD_DIM // 2 if kv == 0 else LANES - HEAD_DIM // 2
        for c in range(seq // rows):
            sl = slice(c * rows, (c + 1) * rows)
            kt, vt = k_ref[sl, :].astype(F32), v_ref[sl, :].astype(F32)
            kt = jnp.where(k_own, kt, pltpu.roll(kt, k_shift, 1))
            vt = jnp.where(v_own, vt, pltpu.roll(vt, HEAD_DIM, 1))
            for dst, t, low in ((kb_ref, kt, k_low), (vb_ref, vt, v_low)):
                dst[0, sl, :] = jnp.where(low, t, 0.0).astype(BF16)
                dst[1, sl, :] = jnp.where(low, 0.0, t).astype(BF16)

    for kv in range(2):
        pl.when(p == kv * pairs_per_kv)(functools.partial(spread_kv_head, kv))

    sinks = (sink_ref[2 * p] * LOG2_E, sink_ref[2 * p + 1] * LOG2_E)
    blk = lambda i: slice(i * ATTN_BLOCK, (i + 1) * ATTN_BLOCK)

    def stack(ref, i):
        both = slice(max(i - 1, 0) * ATTN_BLOCK, (i + 1) * ATTN_BLOCK)
        return jnp.concatenate([ref[0, both, :], ref[1, both, :]], axis=0)

    def score_stage(i):
        return _scores(q_ref[blk(i), :], stack(kb_ref, i))

    def prob_stage(i, s):
        return _probs(s, _band_bias(bias_ref, i == 0), sinks)

    def out_stage(i, mid):
        o, _ = _weighted(*mid, stack(vb_ref, i), _ones_stack(ones_ref, i == 0), sinks)
        o_ref[blk(i), :] = o.astype(o_ref.dtype)

    _run_pipelined(seq // ATTN_BLOCK, score_stage, prob_stage, out_stage)


def _swa_attention(qkv, sinks, batch, seq):
    m = qkv.shape[0]
    q_dim = sinks.shape[0] * HEAD_DIM
    assert q_dim // HEAD_DIM // A_Q_PER_KV == 2 and seq % (2 * ATTN_BLOCK) == 0
    q3 = qkv.reshape(batch, seq, qkv.shape[1])
    kcol, vcol = q_dim // LANES, q_dim // LANES + 1
    slab = (None, seq, LANES)
    out = pl.pallas_call(
        _swa_kernel,
        out_shape=jax.ShapeDtypeStruct((batch, seq, q_dim), BF16),
        grid=(batch, q_dim // LANES),
        in_specs=[
            pl.BlockSpec(memory_space=pltpu.SMEM),
            pl.BlockSpec(slab, lambda b, p: (b, 0, p)),
            pl.BlockSpec(slab, lambda b, p: (b, 0, kcol)),
            pl.BlockSpec(slab, lambda b, p: (b, 0, vcol)),
        ],
        out_specs=pl.BlockSpec(slab, lambda b, p: (b, 0, p)),
        scratch_shapes=[pltpu.VMEM((2, seq, LANES), BF16), pltpu.VMEM((2, seq, LANES), BF16),
                        pltpu.VMEM((ATTN_BLOCK, 2 * ATTN_BLOCK), F32), pltpu.VMEM((4 * ATTN_BLOCK, LANES), BF16)],
        compiler_params=pltpu.CompilerParams(dimension_semantics=("parallel", "arbitrary"),
                                             vmem_limit_bytes=_vmem_limit(16 * seq * LANES * 2)),
        name="swa_sink_attention",
    )(sinks.astype(F32), q3, q3, q3)
    return out.reshape(m, q_dim)


def _dilated_kernel(*refs):
    n_groups = len(C_PATTERNS)
    in_refs, o_ref = refs[:3 * n_groups], refs[3 * n_groups]
    acc_o, acc_l, bias_ref, ones_ref = refs[3 * n_groups + 1:]
    seq = o_ref.shape[0]
    max_dist = C_PATTERNS[0][0] // C_PATTERNS[0][1]
    assert all(w // d == max_dist for w, d in C_PATTERNS)
    _init_band_consts(bias_ref, ones_ref, max_dist)

    order = sorted(range(n_groups), key=lambda g: -C_PATTERNS[g][1])
    blocks = []
    for g in order:
        dil = C_PATTERNS[g][1]
        for r in range(dil):
            for i in range(seq // dil // ATTN_BLOCK):
                blocks.append((g, dil, r, i))

    def rows(dil, r, i):
        start = i * ATTN_BLOCK * dil + r
        return pl.ds(start, ATTN_BLOCK, stride=dil) if dil > 1 else pl.ds(start, ATTN_BLOCK)

    def loader(which):
        return lambda key: in_refs[3 * key[0] + which][rows(*key[1:]), :]

    k_pieces, v_pieces = _piece_cache(loader(1), True), _piece_cache(loader(2), False)

    def stack(pieces, g, dil, r, i):
        return _stack_blocks(pieces((g, dil, r, i - 1)) if i else None, pieces((g, dil, r, i)))

    def score_stage(n):
        g, dil, r, i = blocks[n]
        q2 = in_refs[3 * g][rows(dil, r, i), :].astype(BF16)
        return _scores(q2, stack(k_pieces, g, dil, r, i))

    def prob_stage(n, s):
        return _probs(s, _band_bias(bias_ref, blocks[n][3] == 0), None)

    def out_stage(n, mid):
        g, dil, r, i = blocks[n]
        cur = rows(dil, r, i)
        o, lse = _weighted(*mid, stack(v_pieces, g, dil, r, i), _ones_stack(ones_ref, i == 0), None)
        if g != order[0]:
            o_old, l_old = acc_o[cur, :], acc_l[cur, :]
            l_max = jnp.maximum(l_old, lse)
            e_old, e_new = jnp.exp2(l_old - l_max), jnp.exp2(lse - l_max)
            inv = 1.0 / (e_old + e_new)
            o = o_old * (e_old * inv) + o * (e_new * inv)
            lse = l_max + jnp.log2(e_old + e_new)
        acc_o[cur, :] = o
        if g != order[-1]:
            acc_l[cur, :] = lse

    _run_pipelined(len(blocks), score_stage, prob_stage, out_stage)
    o_ref[...] = acc_o[...].astype(o_ref.dtype)


def _dilated_attention(qkv, batch, seq):
    m, width = qkv.shape
    n_groups = len(C_PATTERNS)
    gw = width // (3 * n_groups)
    assert gw % LANES == 0 and all(seq % (d * ATTN_BLOCK) == 0 for _, d in C_PATTERNS)
    pairs = gw // LANES
    q3 = qkv.reshape(batch, seq, width)
    slab = lambda c: pl.BlockSpec((None, seq, LANES), lambda b, p: (b, 0, c * pairs + p))
    specs = []
    for g in range(n_groups):
        specs += [slab(g), slab(n_groups + g), slab(2 * n_groups + g)]
    out = pl.pallas_call(
        _dilated_kernel,
        out_shape=jax.ShapeDtypeStruct((batch, seq, gw), BF16),
        grid=(batch, pairs),
        in_specs=specs,
        out_specs=pl.BlockSpec((None, seq, LANES), lambda b, p: (b, 0, p)),
        scratch_shapes=[pltpu.VMEM((seq, LANES), F32), pltpu.VMEM((seq, LANES), F32),
                        pltpu.VMEM((ATTN_BLOCK, 2 * ATTN_BLOCK), F32), pltpu.VMEM((4 * ATTN_BLOCK, LANES), BF16)],
        compiler_params=pltpu.CompilerParams(
            dimension_semantics=("parallel", "parallel"),
            vmem_limit_bytes=_vmem_limit(2 * (3 * n_groups + 2) * seq * LANES * 4 + 8 * 1024 * 1024)),
        name="dilated_attention",
    )(*([q3] * (3 * n_groups)))
    return out.reshape(m, gw)


def _ssd_kernel(z_ref, x_ref, b_ref, c_ref, dt_ref, cw_ref, cb_ref, dtb_ref, alog_ref, dskip_ref, nw_ref, e_ref,
                o_ref, xpad_ref, state_ref, conv_ref):
    q = SSM_CHUNK
    d_inner = x_ref.shape[1]
    bc_dim = b_ref.shape[1]
    n_heads = d_inner // SSM_HEAD_DIM
    group_w = d_inner // SSM_N_GROUPS
    pad = SUBLANES
    n_slabs = xpad_ref.shape[0]
    x_slabs, bc_slabs = d_inner // LANES, bc_dim // LANES

    @pl.when(pl.program_id(1) == 0)
    def _():
        xpad_ref[:, 0:pad, :] = jnp.zeros((n_slabs, pad, LANES), F32)
        state_ref[...] = jnp.zeros_like(state_ref)

    def conv_slabs(s_lo, s_hi):
        for s in range(s_lo, s_hi):
            src, s0 = (x_ref, s) if s < x_slabs else (b_ref, s - x_slabs) if s < x_slabs + bc_slabs else (
                c_ref, s - x_slabs - bc_slabs)
            xpad_ref[s, pad:pad + q, :] = src[:, s0 * LANES:(s0 + 1) * LANES].astype(F32)
            lanes = slice(s * LANES, (s + 1) * LANES)
            for parity in range(2):
                acc = cb_ref[:, lanes]
                for k in range(SSM_CONV):
                    tap = xpad_ref[s, pl.ds(pad - k + parity, q // 2, stride=2), :]
                    acc = acc + cw_ref[SSM_CONV - 1 - k:SSM_CONV - k, lanes] * tap
                conv_ref[s, pl.ds(parity, q // 2, stride=2), :] = acc * _sigmoid(acc)

    head_lane = lax.broadcasted_iota(jnp.int32, (q, LANES), 1) < n_heads
    dtr = dt_ref[...] + dtb_ref[...]
    dt = jnp.maximum(dtr, 0.0) + jnp.log(1.0 + jnp.exp(-jnp.abs(dtr)))
    dt = jnp.where(head_lane, dt, 0.0)
    d_a = dt * (-jnp.exp(alog_ref[...]))
    rr = lax.broadcasted_iota(jnp.int32, (q, q), 0)
    cc = lax.broadcasted_iota(jnp.int32, (q, q), 1)
    tril = rr >= cc
    tri16 = jnp.where(tril, 1.0, 0.0).astype(BF16)
    d_a_parts = _split_bf16(d_a, 3)
    conv_slabs(0, n_slabs // 4)
    cs = None
    for part in d_a_parts:
        term = jnp.dot(tri16, part, preferred_element_type=F32)
        cs = term if cs is None else cs + term
    conv_slabs(n_slabs // 4, n_slabs // 2)
    cs2 = cs * LOG2_E
    cs2_t = cs2.T
    ecs = jnp.exp(cs)
    dec = jnp.exp(cs[q - 1:q, :] - cs)
    conv_slabs(n_slabs // 2, 3 * n_slabs // 4)
    e16 = e_ref[...]
    dt_e = _expand_heads(dt, e16, terms=1)
    ecs_e = _expand_heads(ecs, e16)
    dec_e = _expand_heads(dec, e16, terms=1)
    chunk_decay_e = ecs_e[q - 1:q, :]
    conv_slabs(3 * n_slabs // 4, n_slabs)
    xpad_ref[:, 0:pad, :] = xpad_ref[:, q:q + pad, :]

    heads_per_group = n_heads // SSM_N_GROUPS
    lanes_of = lambda g: slice(g * group_w, (g + 1) * group_w)
    slabs_per_group = group_w // LANES
    xs_of = lambda g: jnp.concatenate(
        [conv_ref[g * slabs_per_group + j] for j in range(slabs_per_group)], axis=1)

    def state_stage(g):
        gs = lanes_of(g)
        b_f = conv_ref[x_slabs + g]
        c16 = conv_ref[x_slabs + bc_slabs + g].astype(BF16)
        cb = lax.dot_general(c16, b_f.astype(BF16), (((1,), (1,)), ((), ())), preferred_element_type=F32)
        xdt = xs_of(g) * dt_e[:, gs]
        st_old = state_ref[:, gs]
        y_off = jnp.dot(c16, st_old.astype(BF16), preferred_element_type=F32) * ecs_e[:, gs]
        st_new = jnp.dot(b_f.T.astype(BF16), (xdt * dec_e[:, gs]).astype(BF16), preferred_element_type=F32)
        state_ref[:, gs] = st_old * chunk_decay_e[:, gs] + st_new
        return cb, y_off, xdt.astype(BF16)

    def diag_stage(g, carried):
        cb, y_off, xdt16 = carried
        y_parts = []
        for pr in range(heads_per_group // 2):
            h0 = g * heads_per_group + 2 * pr
            ms = []
            for hh in (h0, h0 + 1):
                diff = cs2[:, hh:hh + 1] - cs2_t[hh:hh + 1, :]
                ms.append(cb * jnp.exp2(jnp.where(tril, diff, -jnp.inf)))
            m2 = jnp.concatenate(ms, axis=1).astype(BF16)
            x_stack = _stack_two_heads(xdt16[:, pr * LANES:(pr + 1) * LANES])
            y_parts.append(jnp.dot(m2, x_stack, preferred_element_type=F32))
        return jnp.concatenate(y_parts, axis=1) + y_off

    def gate_stage(g, y):
        gs = lanes_of(g)
        y = y + dskip_ref[:, gs] * xs_of(g)
        zf = z_ref[:, gs].astype(F32)
        gated = y * (zf * _sigmoid(zf))
        o_ref[:, gs] = (_rms(gated) * nw_ref[:, gs]).astype(o_ref.dtype)

    _run_pipelined(SSM_N_GROUPS, state_stage, diag_stage, gate_stage)


def _ssd_mixer(zx, dt_raw, conv_w, conv_b, dt_bias, a_log, d_skip, norm_w, batch, seq):
    m = zx.shape[0]
    n_heads = a_log.shape[0]
    d_inner = n_heads * SSM_HEAD_DIM
    bc_dim = SSM_N_GROUPS * SSM_D_STATE
    conv_dim = d_inner + 2 * bc_dim
    assert zx.shape[1] == d_inner + conv_dim and seq % SSM_CHUNK == 0 and n_heads <= LANES
    assert d_inner % bc_dim == 0 and bc_dim % LANES == 0 and d_inner // SSM_N_GROUPS == 2 * LANES
    zx3 = zx.reshape(batch, seq, zx.shape[1])
    dt3 = dt_raw.reshape(batch, seq, LANES)
    pad_heads = lambda v: jnp.pad(v.astype(F32), (0, LANES - n_heads))[None, :]
    blk = lambda w: (None, SSM_CHUNK, w)
    const = lambda shape: pl.BlockSpec(shape, lambda b, c: (0,) * len(shape))
    r = d_inner // bc_dim
    out = pl.pallas_call(
        _ssd_kernel,
        out_shape=jax.ShapeDtypeStruct((batch, seq, d_inner), BF16),
        grid=(batch, seq // SSM_CHUNK),
        in_specs=[
            pl.BlockSpec(blk(d_inner), lambda b, c: (b, c, 0)),
            pl.BlockSpec(blk(d_inner), lambda b, c: (b, c, 1)),
            pl.BlockSpec(blk(bc_dim), lambda b, c: (b, c, 2 * r)),
            pl.BlockSpec(blk(bc_dim), lambda b, c: (b, c, 2 * r + 1)),
            pl.BlockSpec(blk(LANES), lambda b, c: (b, c, 0)),
            const((SSM_CONV, conv_dim)), const((1, conv_dim)), const((1, LANES)), const((1, LANES)),
            const((1, d_inner)), const((1, d_inner)), const((LANES, d_inner)),
        ],
        out_specs=pl.BlockSpec(blk(d_inner), lambda b, c: (b, c, 0)),
        scratch_shapes=[
            pltpu.VMEM((conv_dim // LANES, SSM_CHUNK + SUBLANES, LANES), F32),
            pltpu.VMEM((SSM_D_STATE, d_inner), F32),
            pltpu.VMEM((conv_dim // LANES, SSM_CHUNK, LANES), F32),
        ],
        compiler_params=pltpu.CompilerParams(dimension_semantics=("parallel", "arbitrary"),
                                             vmem_limit_bytes=_vmem_limit(40 * 1024 * 1024)),
        name="ssd_mixer",
    )(zx3, zx3, zx3, zx3, dt3, conv_w.astype(F32), conv_b.astype(F32)[None, :], pad_heads(dt_bias),
      pad_heads(a_log), jnp.repeat(d_skip.astype(F32), SSM_HEAD_DIM)[None, :], norm_w.astype(F32)[None, :],
      _head_expander(n_heads, SSM_HEAD_DIM))
    return out.reshape(m, d_inner)


def kernel(x, positions, norm_mix_w, norm_mlp_w, a_w_qkv, a_b_qkv, a_sinks, a_w_o, a_b_o, b_in_w, b_conv_w, b_conv_b, b_dt_bias, b_a_log, b_d, b_norm_w, b_out_w, c_w_qkv, c_w_o, mlp_w_up, mlp_w_down, final_norm_w):
    batch, seq, d_model = x.shape
    m = batch * seq
    depth = norm_mix_w.shape[0]
    h = x.reshape(m, d_model).astype(F32)
    rope = _rope_tables(positions.reshape(m, 1))

    for i in range(depth):
        kind, j = i % N_MIXERS, i // N_MIXERS
        if kind == 0:
            q_dim = a_sinks.shape[1] * HEAD_DIM
            kv_dim = (a_w_qkv.shape[2] - q_dim) // 2
            n_rope = q_dim + kv_dim
            qkv = _linear(h, _pair_interleave(a_w_qkv[j], n_rope).astype(BF16), gain=norm_mix_w[i],
                          bias=_pair_interleave(a_b_qkv[j], n_rope), rope=rope, rope_cols=n_rope, q_cols=q_dim,
                          name="swa_qkv")
            mix, w_o, b_o = _swa_attention(qkv, a_sinks[j], batch, seq), a_w_o[j], a_b_o[j]
        elif kind == 1:
            n_heads = b_a_log.shape[1]
            w_in = b_in_w[j]
            main = w_in.shape[1] - n_heads
            w_dt = jnp.pad(w_in[:, main:], ((0, 0), (0, LANES - n_heads))).astype(BF16)
            zx, dt_raw = _linear(h, w_in[:, :main].astype(BF16), gain=norm_mix_w[i], side_w=w_dt, tm=512,
                                 name="ssm_in")
            mix = _ssd_mixer(zx, dt_raw, b_conv_w[j], b_conv_b[j], b_dt_bias[j], b_a_log[j], b_d[j], b_norm_w[j],
                             batch, seq)
            w_o, b_o = b_out_w[j], None
        else:
            width = c_w_qkv.shape[2]
            n_rope = 2 * width // 3
            qkv = _linear(h, _pair_interleave(c_w_qkv[j], n_rope).astype(BF16), gain=norm_mix_w[i], rope=rope,
                          rope_cols=n_rope, q_cols=width // 3, out_dtype=F32, tm=256, name="dilated_qkv")
            mix, w_o, b_o = _dilated_attention(qkv, batch, seq), c_w_o[j], None
        final_gain = final_norm_w if i == depth - 1 else None
        h = _mix_out_mlp(h, mix, w_o.astype(BF16), b_o, norm_mlp_w[i], mlp_w_up[i].astype(BF16),
                         mlp_w_down[i].astype(BF16), final_gain)
    return h.reshape(batch, seq, d_model).astype(x.dtype)
```

```python
import functools
import math

import jax
import jax.numpy as jnp
from jax import lax
from jax.experimental import pallas as pl
from jax.experimental.pallas import tpu as pltpu

F32 = jnp.float32
BF16 = jnp.bfloat16

N_MIXERS = 3
ATTN_BLOCK = 128
HEAD_DIM = 64
ROPE_THETA = 10000.0
NORM_EPS = 1e-5
A_Q_PER_KV = 8
A_WINDOW = 128
C_PATTERNS = ((128, 1), (512, 4), (2048, 16))
SSM_HEAD_DIM = 64
SSM_N_GROUPS = 8
SSM_D_STATE = 128
SSM_CONV = 4
SSM_CHUNK = 128

LANES = 128
LOG2_E = math.log2(math.e)
SUBLANES = 8
V7X_VMEM_LIMIT_CAP = 56 * 1024 * 1024


def _vmem_limit(nbytes):
    return int(min(max(nbytes, 16 * 1024 * 1024), V7X_VMEM_LIMIT_CAP))


def _row_tile(m):
    for t in (1024, 512, 256, 128):
        if m % t == 0:
            return t
    raise ValueError(f"token count {m} must be a multiple of 128")


def _rms(xf):
    return xf * lax.rsqrt(jnp.mean(xf * xf, axis=-1, keepdims=True) + NORM_EPS)


def _sigmoid(v):
    return 1.0 / (1.0 + jnp.exp2(v * (-LOG2_E)))


def _split_bf16(v, n):
    parts, r = [], v
    for _ in range(n):
        p = r.astype(BF16)
        parts.append(p)
        r = r - p.astype(F32)
    return parts


def _dot_split(parts, rhs):
    acc = None
    for p in parts:
        d = jnp.dot(p, rhs, preferred_element_type=F32)
        acc = d if acc is None else acc + d
    return acc


def _expand_heads(v, e_bf16, terms=2):
    return _dot_split(_split_bf16(v, terms), e_bf16)


def _head_expander(n_heads, head_dim):
    rows = jnp.arange(LANES)[:, None]
    cols = jnp.arange(n_heads * head_dim)[None, :] // head_dim
    return (rows == cols).astype(BF16)


ROPE_PACK = LANES // (HEAD_DIM // 2)


def _rope_table_kernel(pos_ref, inv_ref, sign_ref, cos_ref, sin_ref):
    packed_rows = pos_ref.shape[0]
    width = LANES // ROPE_PACK
    quarter = lax.broadcasted_iota(jnp.int32, (packed_rows, LANES), 1) // width
    pos = pos_ref[...].astype(F32)
    packed_pos = pos[:, ROPE_PACK - 1:ROPE_PACK]
    for a in range(ROPE_PACK - 2, -1, -1):
        packed_pos = jnp.where(quarter == a, pos[:, a:a + 1], packed_pos)
    ang = packed_pos * inv_ref[...]
    for table, out_ref, signed in ((jnp.cos(ang), cos_ref, False), (jnp.sin(ang), sin_ref, True)):
        turned = [table] + [pltpu.roll(table, width * k, 1) for k in range(1, ROPE_PACK)]
        for a in range(ROPE_PACK):
            spread = turned[(ROPE_PACK - 1 - a) % ROPE_PACK]
            for ql in range(ROPE_PACK - 1):
                spread = jnp.where(quarter == ql, turned[(ql - a) % ROPE_PACK], spread)
            if signed:
                spread = spread * sign_ref[...]
            out_ref[pl.ds(a, packed_rows, stride=ROPE_PACK), :] = spread


def _rope_tables(pos_col):
    m = pos_col.shape[0]
    tm = _row_tile(m)
    half = HEAD_DIM // 2
    inv = ROPE_THETA ** (-jnp.arange(half, dtype=F32) / half)
    inv_l = jnp.tile(inv, LANES // half)[None, :]
    sign_l = jnp.where(jnp.arange(LANES) < LANES // 2, -1.0, 1.0).astype(F32)[None, :]
    row = pl.BlockSpec((tm, LANES), lambda i: (i, 0))
    const = pl.BlockSpec((1, LANES), lambda i: (0, 0))
    return pl.pallas_call(
        _rope_table_kernel,
        out_shape=(jax.ShapeDtypeStruct((m, LANES), F32), jax.ShapeDtypeStruct((m, LANES), F32)),
        grid=(m // tm,),
        in_specs=[pl.BlockSpec((tm // ROPE_PACK, ROPE_PACK), lambda i: (i, 0)), const, const],
        out_specs=(row, row),
        compiler_params=pltpu.CompilerParams(dimension_semantics=("parallel",)),
        name="rope_tables",
    )(pos_col.reshape(m // ROPE_PACK, ROPE_PACK), inv_l, sign_l)


def _pair_interleave(w, n_rope):
    lead = w.shape[:-1]
    head = w[..., :n_rope].reshape(*lead, n_rope // LANES, 4, LANES // 4)
    head = jnp.stack([head[..., 0, :], head[..., 2, :], head[..., 1, :], head[..., 3, :]], axis=-2)
    return jnp.concatenate([head.reshape(*lead, n_rope), w[..., n_rope:]], axis=-1)


def _rope_lanes(yc, cos, sin_signed):
    return yc * cos + pltpu.roll(yc, LANES // 2, 1) * sin_signed


MXU_COLS = 256


def _linear_kernel(*refs, has_gain, has_bias, has_res, has_side, rope_cols, q_cols, n_cols, prep_x):
    refs = list(refs)
    x_ref, w_ref = refs.pop(0), refs.pop(0)
    g_ref = refs.pop(0) if has_gain else None
    b_ref = refs.pop(0) if has_bias else None
    cos_ref, sin_ref = (refs.pop(0), refs.pop(0)) if rope_cols else (None, None)
    r_ref = refs.pop(0) if has_res else None
    ws_ref = refs.pop(0) if has_side else None
    o_ref = refs.pop(0)
    side_ref = refs.pop(0) if has_side else None
    j = pl.program_id(1)
    tn = o_ref.shape[1]

    def prepared():
        xf = x_ref[...].astype(F32)
        if has_gain:
            xf = _rms(xf) * g_ref[...]
        return xf.astype(BF16)

    if not prep_x:
        xb = x_ref[...]
    elif tn == n_cols:
        xb = prepared()
    else:
        xn_ref = refs.pop(0)

        @pl.when(j == 0)
        def _():
            xn_ref[...] = prepared()
            if has_side:
                side_ref[...] = jnp.dot(xn_ref[...], ws_ref[...], preferred_element_type=F32)
        xb = xn_ref[...]
    if has_side and (not prep_x or tn == n_cols):
        side_ref[...] = jnp.dot(xb, ws_ref[...], preferred_element_type=F32)

    tiled_rope = rope_cols and tn < n_cols
    if rope_cols:
        cos, sin = cos_ref[...], sin_ref[...]
        if tiled_rope:
            is_rope = j < rope_cols // tn
            cos, sin = jnp.where(is_rope, cos, 1.0), jnp.where(is_rope, sin, 0.0)

    if q_cols:
        assert rope_cols and not tiled_rope and q_cols <= rope_cols
        cos_q, sin_q = cos * Q_SCALE, sin * Q_SCALE

    step = MXU_COLS if tn % MXU_COLS == 0 else LANES
    for c0 in range(0, tn, step):
        acc = jnp.dot(xb, w_ref[:, c0:c0 + step], preferred_element_type=F32)
        if has_bias:
            acc = acc + b_ref[:, c0:c0 + step]
        if has_res:
            acc = acc + r_ref[:, c0:c0 + step]
        for l0 in range(0, step, LANES):
            yc = acc[:, l0:l0 + LANES]
            if rope_cols and (tiled_rope or c0 + l0 < rope_cols):
                yc = _rope_lanes(yc, cos_q, sin_q) if c0 + l0 < q_cols else _rope_lanes(yc, cos, sin)
            o_ref[:, c0 + l0:c0 + l0 + LANES] = yc.astype(o_ref.dtype)


def _linear(x, w, *, gain=None, bias=None, rope=None, rope_cols=0, q_cols=0, residual=None, side_w=None,
            out_dtype=BF16, tm=None, tn=None, name="linear"):
    m, k = x.shape
    n = w.shape[1]
    tm = _row_tile(m) if tm is None else tm
    tn = n if tn is None else tn
    assert m % tm == 0 and n % tn == 0 and tn % LANES == 0
    if rope_cols and rope_cols % tn:
        assert tn == n and rope_cols % LANES == 0
    prep_x = gain is not None or x.dtype != BF16
    row = lambda i, j: (i, 0)
    col = lambda i, j: (0, j)
    const = lambda i, j: (0, 0)
    w_spec = pl.BlockSpec((k, tn), col, pipeline_mode=pl.Buffered(1)) if tn == n else pl.BlockSpec((k, tn), col)
    args, specs = [x, w], [pl.BlockSpec((tm, k), row), w_spec]
    if gain is not None:
        args.append(gain.reshape(1, k).astype(F32))
        specs.append(pl.BlockSpec((1, k), const))
    if bias is not None:
        args.append(bias.reshape(1, n).astype(F32))
        specs.append(pl.BlockSpec((1, tn), col))
    if rope_cols:
        args += list(rope)
        specs += [pl.BlockSpec((tm, LANES), row)] * 2
    if residual is not None:
        args.append(residual)
        specs.append(pl.BlockSpec((tm, tn), lambda i, j: (i, j)))
    out_shape = jax.ShapeDtypeStruct((m, n), out_dtype)
    out_specs = pl.BlockSpec((tm, tn), lambda i, j: (i, j))
    if side_w is not None:
        args.append(side_w)
        specs.append(pl.BlockSpec((k, LANES), const))
        out_shape = (out_shape, jax.ShapeDtypeStruct((m, LANES), F32))
        out_specs = (out_specs, pl.BlockSpec((tm, LANES), row))
    out_bytes = jnp.dtype(out_dtype).itemsize
    est = (2 * tm * k * x.dtype.itemsize + 2 * k * tn * 2 + 2 * tm * tn * out_bytes
           + (2 * tm * tn * 4 if residual is not None else 0) + tm * k * 2 + 2 * tm * k * 4
           + 4 * tm * MXU_COLS * 4 + 6 * tm * LANES * 4)
    return pl.pallas_call(
        functools.partial(_linear_kernel, has_gain=gain is not None, has_bias=bias is not None,
                          has_res=residual is not None, has_side=side_w is not None, rope_cols=rope_cols,
                          q_cols=q_cols, n_cols=n, prep_x=prep_x),
        out_shape=out_shape,
        grid=(m // tm, n // tn),
        in_specs=specs,
        out_specs=out_specs,
        scratch_shapes=[pltpu.VMEM((tm, k), BF16)] if prep_x and tn < n else [],
        compiler_params=pltpu.CompilerParams(dimension_semantics=("parallel", "arbitrary"),
                                             vmem_limit_bytes=_vmem_limit(est)),
        name=name,
    )(*args)


MLP_ROWS = 1024
MLP_FF_CHUNK = 512


def _mlp_kernel(*refs, has_bias, has_final):
    refs = list(refs)
    x_ref, a_ref, wo_ref = refs.pop(0), refs.pop(0), refs.pop(0)
    bo_ref = refs.pop(0) if has_bias else None
    g_ref, wu_ref, wd_ref = refs.pop(0), refs.pop(0), refs.pop(0)
    gf_ref = refs.pop(0) if has_final else None
    o_ref = refs.pop(0)
    x = x_ref[...] + jnp.dot(a_ref[...], wo_ref[...], preferred_element_type=F32)
    if has_bias:
        x = x + bo_ref[...]
    xn = (_rms(x) * g_ref[...]).astype(BF16)
    y = x
    ff = wu_ref.shape[1]
    step = MLP_FF_CHUNK if ff % MLP_FF_CHUNK == 0 else ff
    for c0 in range(0, ff, step):
        u = jnp.maximum(jnp.dot(xn, wu_ref[:, c0:c0 + step], preferred_element_type=F32), 0.0)
        y = y + jnp.dot((u * u).astype(BF16), wd_ref[c0:c0 + step, :], preferred_element_type=F32)
    if has_final:
        y = _rms(y) * gf_ref[...]
    o_ref[...] = y


def _mix_out_mlp(h, mix, w_o, b_o, gain, w_up, w_down, final_gain=None):
    m, d = h.shape
    k = mix.shape[1]
    ff = w_up.shape[1]
    tm = MLP_ROWS if m % MLP_ROWS == 0 else _row_tile(m)
    row = lambda i: (i, 0)
    const = lambda i: (0, 0)
    vec = pl.BlockSpec((1, d), const)
    resident = lambda shape: pl.BlockSpec(shape, const, pipeline_mode=pl.Buffered(1))
    args = [h, mix, w_o]
    specs = [pl.BlockSpec((tm, d), row), pl.BlockSpec((tm, k), row), resident((k, d))]
    if b_o is not None:
        args.append(b_o.reshape(1, d).astype(F32))
        specs.append(vec)
    args += [gain.reshape(1, d).astype(F32), w_up, w_down]
    specs += [vec, resident((d, ff)), resident((ff, d))]
    if final_gain is not None:
        args.append(final_gain.reshape(1, d).astype(F32))
        specs.append(vec)
    est = 4 * d * ff + 2 * k * d + 4 * tm * k + 4 * tm * d * 4 + 4 * tm * d * 4 + 3 * tm * MLP_FF_CHUNK * 4
    return pl.pallas_call(
        functools.partial(_mlp_kernel, has_bias=b_o is not None, has_final=final_gain is not None),
        out_shape=jax.ShapeDtypeStruct((m, d), F32),
        grid=(m // tm,),
        in_specs=specs,
        out_specs=pl.BlockSpec((tm, d), row),
        compiler_params=pltpu.CompilerParams(dimension_semantics=("parallel",),
                                             vmem_limit_bytes=_vmem_limit(est)),
        name="sqrelu_mlp",
    )(*args)


Q_SCALE = HEAD_DIM ** -0.5 * LOG2_E
STAGE_STRIDE = 4
LAST_STAGE_LAG = 1
ATTN_PIPE_DEPTH = 2


def _low_head_lanes(shape, interleaved):
    lane = lax.broadcasted_iota(jnp.int32, shape, 1)
    return (lane % HEAD_DIM) < HEAD_DIM // 2 if interleaved else lane < HEAD_DIM


def _stack_two_heads(t, interleaved=False):
    low = _low_head_lanes(t.shape, interleaved)
    zero = jnp.zeros_like(t)
    return jnp.concatenate([jnp.where(low, t, zero), jnp.where(low, zero, t)], axis=0)


def _init_band_consts(bias_ref, ones_ref, max_dist):
    r = lax.broadcasted_iota(jnp.int32, (ATTN_BLOCK, 2 * ATTN_BLOCK), 0)
    c = lax.broadcasted_iota(jnp.int32, (ATTN_BLOCK, 2 * ATTN_BLOCK), 1)
    dist = r + ATTN_BLOCK - c
    bias_ref[...] = jnp.where((dist >= 0) & (dist <= max_dist), 0.0, -jnp.inf)
    low = _low_head_lanes((2 * ATTN_BLOCK, LANES), False)
    ones_ref[0:2 * ATTN_BLOCK, :] = jnp.where(low, 1.0, 0.0).astype(BF16)
    ones_ref[2 * ATTN_BLOCK:, :] = jnp.where(low, 0.0, 1.0).astype(BF16)


def _head_pieces(t, interleaved):
    low = _low_head_lanes(t.shape, interleaved)
    return jnp.where(low, t, 0.0).astype(BF16), jnp.where(low, 0.0, t).astype(BF16)


def _piece_cache(load, interleaved):
    cache = {}

    def get(key):
        if key not in cache:
            cache[key] = _head_pieces(load(key), interleaved)
        return cache[key]
    return get


def _stack_blocks(prev, cur):
    return jnp.concatenate(list(cur) if prev is None else [prev[0], cur[0], prev[1], cur[1]], axis=0)


def _band_bias(bias_ref, first):
    return bias_ref[:, ATTN_BLOCK:] if first else bias_ref[...]


def _ones_stack(ones_ref, first):
    if not first:
        return ones_ref[...]
    half = ones_ref.shape[0] // 2
    return jnp.concatenate([ones_ref[half - ATTN_BLOCK:half, :], ones_ref[2 * half - ATTN_BLOCK:, :]], axis=0)


def _scores(q2, k_stack):
    return lax.dot_general(q2, k_stack, (((1,), (1,)), ((), ())), preferred_element_type=F32)


def _probs(s, bias, sinks):
    kw = s.shape[1] // 2
    s0, s1 = s[:, :kw] + bias, s[:, kw:] + bias
    m0 = jnp.max(s0, axis=-1, keepdims=True)
    m1 = jnp.max(s1, axis=-1, keepdims=True)
    if sinks is not None:
        m0, m1 = jnp.maximum(m0, sinks[0]), jnp.maximum(m1, sinks[1])
    p2 = jnp.concatenate([jnp.exp2(s0 - m0), jnp.exp2(s1 - m1)], axis=1).astype(BF16)
    return p2, jnp.where(_low_head_lanes((s.shape[0], LANES), False), m0, m1)


def _weighted(p2, m_l, v_stack, ones_stack, sinks):
    od = jnp.dot(p2, jnp.concatenate([v_stack, ones_stack], axis=1), preferred_element_type=F32)
    o, den = od[:, :LANES], od[:, LANES:]
    if sinks is not None:
        den = den + jnp.exp2(jnp.where(_low_head_lanes(o.shape, False), sinks[0], sinks[1]) - m_l)
    return o / den, m_l + jnp.log2(den)


def _run_pipelined(n, first_stage, mid_stage, last_stage):
    ahead = {i: first_stage(i) for i in range(min(ATTN_PIPE_DEPTH, n))}
    mids = {}
    for i in range(n + LAST_STAGE_LAG):
        if i < n:
            mids[i] = mid_stage(i, ahead.pop(i))
            if i + ATTN_PIPE_DEPTH < n:
                ahead[i + ATTN_PIPE_DEPTH] = first_stage(i + ATTN_PIPE_DEPTH)
        if i >= LAST_STAGE_LAG:
            last_stage(i - LAST_STAGE_LAG, mids.pop(i - LAST_STAGE_LAG))


def _swa_kernel(sink_ref, q_ref, k_ref, v_ref, o_ref, kb_ref, vb_ref, bias_ref, ones_ref):
    seq = q_ref.shape[0]
    p = pl.program_id(1)
    pairs_per_kv = A_Q_PER_KV // 2
    rows = 2 * ATTN_BLOCK

    @pl.when(p == 0)
    def _():
        _init_band_consts(bias_ref, ones_ref, A_WINDOW - 1)

    def spread_kv_head(kv):
        k_low, v_low = _low_head_lanes((rows, LANES), True), _low_head_lanes((rows, LANES), False)
        k_own, v_own = (k_low, v_low) if kv == 0 else (jnp.logical_not(k_low), jnp.logical_not(v_low))
        k_shift = HEAD_DIM // 2 if kv == 0 else LANES - HEAD_DIM // 2
        for c in range(seq // rows):
            sl = slice(c * rows, (c + 1) * rows)
            kt, vt = k_ref[sl, :].astype(F32), v_ref[sl, :].astype(F32)
            kt = jnp.where(k_own, kt, pltpu.roll(kt, k_shift, 1))
            vt = jnp.where(v_own, vt, pltpu.roll(vt, HEAD_DIM, 1))
            for dst, t, low in ((kb_ref, kt, k_low), (vb_ref, vt, v_low)):
                dst[0, sl, :] = jnp.where(low, t, 0.0).astype(BF16)
                dst[1, sl, :] = jnp.where(low, 0.0, t).astype(BF16)

    for kv in range(2):
        pl.when(p == kv * pairs_per_kv)(functools.partial(spread_kv_head, kv))

    sinks = (sink_ref[2 * p] * LOG2_E, sink_ref[2 * p + 1] * LOG2_E)
    blk = lambda i: slice(i * ATTN_BLOCK, (i + 1) * ATTN_BLOCK)

    def stack(ref, i):
        both = slice(max(i - 1, 0) * ATTN_BLOCK, (i + 1) * ATTN_BLOCK)
        return jnp.concatenate([ref[0, both, :], ref[1, both, :]], axis=0)

    def score_stage(i):
        return _scores(q_ref[blk(i), :], stack(kb_ref, i))

    def prob_stage(i, s):
        return _probs(s, _band_bias(bias_ref, i == 0), sinks)

    def out_stage(i, mid):
        o, _ = _weighted(*mid, stack(vb_ref, i), _ones_stack(ones_ref, i == 0), sinks)
        o_ref[blk(i), :] = o.astype(o_ref.dtype)

    _run_pipelined(seq // ATTN_BLOCK, score_stage, prob_stage, out_stage)


def _swa_attention(qkv, sinks, batch, seq):
    m = qkv.shape[0]
    q_dim = sinks.shape[0] * HEAD_DIM
    assert q_dim // HEAD_DIM // A_Q_PER_KV == 2 and seq % (2 * ATTN_BLOCK) == 0
    q3 = qkv.reshape(batch, seq, qkv.shape[1])
    kcol, vcol = q_dim // LANES, q_dim // LANES + 1
    slab = (None, seq, LANES)
    out = pl.pallas_call(
        _swa_kernel,
        out_shape=jax.ShapeDtypeStruct((batch, seq, q_dim), BF16),
        grid=(batch, q_dim // LANES),
        in_specs=[
            pl.BlockSpec(memory_space=pltpu.SMEM),
            pl.BlockSpec(slab, lambda b, p: (b, 0, p)),
            pl.BlockSpec(slab, lambda b, p: (b, 0, kcol)),
            pl.BlockSpec(slab, lambda b, p: (b, 0, vcol)),
        ],
        out_specs=pl.BlockSpec(slab, lambda b, p: (b, 0, p)),
        scratch_shapes=[pltpu.VMEM((2, seq, LANES), BF16), pltpu.VMEM((2, seq, LANES), BF16),
                        pltpu.VMEM((ATTN_BLOCK, 2 * ATTN_BLOCK), F32), pltpu.VMEM((4 * ATTN_BLOCK, LANES), BF16)],
        compiler_params=pltpu.CompilerParams(dimension_semantics=("parallel", "arbitrary"),
                                             vmem_limit_bytes=_vmem_limit(16 * seq * LANES * 2)),
        name="swa_sink_attention",
    )(sinks.astype(F32), q3, q3, q3)
    return out.reshape(m, q_dim)


def _dilated_kernel(*refs):
    n_groups = len(C_PATTERNS)
    in_refs, o_ref = refs[:3 * n_groups], refs[3 * n_groups]
    acc_o, acc_l, bias_ref, ones_ref, stage_ref = refs[3 * n_groups + 1:]
    seq = o_ref.shape[0]
    max_dist = C_PATTERNS[0][0] // C_PATTERNS[0][1]
    assert all(w // d == max_dist for w, d in C_PATTERNS)
    _init_band_consts(bias_ref, ones_ref, max_dist)

    order = sorted(range(n_groups), key=lambda g: -C_PATTERNS[g][1])
    blocks = []
    for g in order:
        dil = C_PATTERNS[g][1]
        for r in range(dil):
            for i in range(seq // dil // ATTN_BLOCK):
                blocks.append((g, dil, r, i))

    def rows(dil, r, i):
        start = i * ATTN_BLOCK * dil + r
        return pl.ds(start, ATTN_BLOCK, stride=dil) if dil > 1 else pl.ds(start, ATTN_BLOCK)

    staged = {g for g in range(n_groups) if C_PATTERNS[g][1] % (4 * STAGE_STRIDE) == 0}
    assert len(staged) <= 1
    for g in staged:
        for which in range(3):
            for a in range(STAGE_STRIDE):
                stage_ref[which, a] = in_refs[3 * g + which][pl.ds(a, seq // STAGE_STRIDE, stride=STAGE_STRIDE), :]

    def read(which, g, dil, r, i):
        if g not in staged:
            return in_refs[3 * g + which][rows(dil, r, i), :]
        inner = dil // STAGE_STRIDE
        return stage_ref[which, r % STAGE_STRIDE,
                         pl.ds(i * ATTN_BLOCK * inner + r // STAGE_STRIDE, ATTN_BLOCK, stride=inner), :]

    def loader(which):
        return lambda key: read(which, *key)

    k_pieces, v_pieces = _piece_cache(loader(1), True), _piece_cache(loader(2), False)

    def stack(pieces, g, dil, r, i):
        return _stack_blocks(pieces((g, dil, r, i - 1)) if i else None, pieces((g, dil, r, i)))

    def score_stage(n):
        g, dil, r, i = blocks[n]
        q2 = read(0, g, dil, r, i).astype(BF16)
        return _scores(q2, stack(k_pieces, g, dil, r, i))

    def prob_stage(n, s):
        return _probs(s, _band_bias(bias_ref, blocks[n][3] == 0), None)

    def out_stage(n, mid):
        g, dil, r, i = blocks[n]
        cur = rows(dil, r, i)
        o, lse = _weighted(*mid, stack(v_pieces, g, dil, r, i), _ones_stack(ones_ref, i == 0), None)
        if g != order[0]:
            o_old, l_old = acc_o[cur, :], acc_l[cur, :]
            l_max = jnp.maximum(l_old, lse)
            e_old, e_new = jnp.exp2(l_old - l_max), jnp.exp2(lse - l_max)
            inv = 1.0 / (e_old + e_new)
            o = o_old * (e_old * inv) + o * (e_new * inv)
            lse = l_max + jnp.log2(e_old + e_new)
        acc_o[cur, :] = o
        if g != order[-1]:
            acc_l[cur, :] = lse

    _run_pipelined(len(blocks), score_stage, prob_stage, out_stage)
    o_ref[...] = acc_o[...].astype(o_ref.dtype)


def _dilated_attention(qkv, batch, seq):
    m, width = qkv.shape
    n_groups = len(C_PATTERNS)
    gw = width // (3 * n_groups)
    assert gw % LANES == 0 and all(seq % (d * ATTN_BLOCK) == 0 for _, d in C_PATTERNS)
    pairs = gw // LANES
    q3 = qkv.reshape(batch, seq, width)
    slab = lambda c: pl.BlockSpec((None, seq, LANES), lambda b, p: (b, 0, c * pairs + p))
    specs = []
    for g in range(n_groups):
        specs += [slab(g), slab(n_groups + g), slab(2 * n_groups + g)]
    out = pl.pallas_call(
        _dilated_kernel,
        out_shape=jax.ShapeDtypeStruct((batch, seq, gw), BF16),
        grid=(batch, pairs),
        in_specs=specs,
        out_specs=pl.BlockSpec((None, seq, LANES), lambda b, p: (b, 0, p)),
        scratch_shapes=[pltpu.VMEM((seq, LANES), F32), pltpu.VMEM((seq, LANES), F32),
                        pltpu.VMEM((ATTN_BLOCK, 2 * ATTN_BLOCK), F32), pltpu.VMEM((4 * ATTN_BLOCK, LANES), BF16),
                        pltpu.VMEM((3, STAGE_STRIDE, seq // STAGE_STRIDE, LANES), F32)],
        compiler_params=pltpu.CompilerParams(
            dimension_semantics=("parallel", "parallel"),
            vmem_limit_bytes=_vmem_limit(2 * (3 * n_groups + 2) * seq * LANES * 4 + 8 * 1024 * 1024)),
        name="dilated_attention",
    )(*([q3] * (3 * n_groups)))
    return out.reshape(m, gw)


def _ssd_kernel(z_ref, x_ref, b_ref, c_ref, dt_ref, cw_ref, cb_ref, dtb_ref, alog_ref, dskip_ref, nw_ref, e_ref,
                o_ref, xpad_ref, state_ref, conv_ref):
    q = SSM_CHUNK
    d_inner = x_ref.shape[1]
    bc_dim = b_ref.shape[1]
    n_heads = d_inner // SSM_HEAD_DIM
    group_w = d_inner // SSM_N_GROUPS
    pad = SUBLANES
    n_slabs = xpad_ref.shape[0]
    x_slabs, bc_slabs = d_inner // LANES, bc_dim // LANES

    @pl.when(pl.program_id(1) == 0)
    def _():
        xpad_ref[:, 0:pad, :] = jnp.zeros((n_slabs, pad, LANES), F32)
        state_ref[...] = jnp.zeros_like(state_ref)

    def conv_slabs(s_lo, s_hi):
        for s in range(s_lo, s_hi):
            src, s0 = (x_ref, s) if s < x_slabs else (b_ref, s - x_slabs) if s < x_slabs + bc_slabs else (
                c_ref, s - x_slabs - bc_slabs)
            xpad_ref[s, pad:pad + q, :] = src[:, s0 * LANES:(s0 + 1) * LANES].astype(F32)
            lanes = slice(s * LANES, (s + 1) * LANES)
            for parity in range(2):
                acc = cb_ref[:, lanes]
                for k in range(SSM_CONV):
                    tap = xpad_ref[s, pl.ds(pad - k + parity, q // 2, stride=2), :]
                    acc = acc + cw_ref[SSM_CONV - 1 - k:SSM_CONV - k, lanes] * tap
                conv_ref[s, pl.ds(parity, q // 2, stride=2), :] = acc * _sigmoid(acc)

    head_lane = lax.broadcasted_iota(jnp.int32, (q, LANES), 1) < n_heads
    dtr = dt_ref[...] + dtb_ref[...]
    dt = jnp.maximum(dtr, 0.0) + jnp.log(1.0 + jnp.exp(-jnp.abs(dtr)))
    dt = jnp.where(head_lane, dt, 0.0)
    d_a = dt * (-jnp.exp(alog_ref[...]))
    rr = lax.broadcasted_iota(jnp.int32, (q, q), 0)
    cc = lax.broadcasted_iota(jnp.int32, (q, q), 1)
    tril = rr >= cc
    tri16 = jnp.where(tril, 1.0, 0.0).astype(BF16)
    d_a_parts = _split_bf16(d_a, 3)
    conv_slabs(0, n_slabs // 4)
    cs = None
    for part in d_a_parts:
        term = jnp.dot(tri16, part, preferred_element_type=F32)
        cs = term if cs is None else cs + term
    conv_slabs(n_slabs // 4, n_slabs // 2)
    cs2 = cs * LOG2_E
    cs2_t = cs2.T
    ecs = jnp.exp(cs)
    dec = jnp.exp(cs[q - 1:q, :] - cs)
    conv_slabs(n_slabs // 2, 3 * n_slabs // 4)
    e16 = e_ref[...]
    dt_e = _expand_heads(dt, e16, terms=1)
    ecs_e = _expand_heads(ecs, e16)
    dec_e = _expand_heads(dec, e16, terms=1)
    chunk_decay_e = ecs_e[q - 1:q, :]
    conv_slabs(3 * n_slabs // 4, n_slabs)
    xpad_ref[:, 0:pad, :] = xpad_ref[:, q:q + pad, :]

    heads_per_group = n_heads // SSM_N_GROUPS
    lanes_of = lambda g: slice(g * group_w, (g + 1) * group_w)
    slabs_per_group = group_w // LANES
    xs_of = lambda g: jnp.concatenate(
        [conv_ref[g * slabs_per_group + j] for j in range(slabs_per_group)], axis=1)

    def state_stage(g):
        gs = lanes_of(g)
        b_f = conv_ref[x_slabs + g]
        c16 = conv_ref[x_slabs + bc_slabs + g].astype(BF16)
        cb = lax.dot_general(c16, b_f.astype(BF16), (((1,), (1,)), ((), ())), preferred_element_type=F32)
        xdt = xs_of(g) * dt_e[:, gs]
        st_old = state_ref[:, gs]
        y_off = jnp.dot(c16, st_old.astype(BF16), preferred_element_type=F32) * ecs_e[:, gs]
        st_new = jnp.dot(b_f.T.astype(BF16), (xdt * dec_e[:, gs]).astype(BF16), preferred_element_type=F32)
        state_ref[:, gs] = st_old * chunk_decay_e[:, gs] + st_new
        return cb, y_off, xdt.astype(BF16)

    def diag_stage(g, carried):
        cb, y_off, xdt16 = carried
        y_parts = []
        for pr in range(heads_per_group // 2):
            h0 = g * heads_per_group + 2 * pr
            ms = []
            for hh in (h0, h0 + 1):
                diff = cs2[:, hh:hh + 1] - cs2_t[hh:hh + 1, :]
                ms.append(cb * jnp.exp2(jnp.where(tril, diff, -jnp.inf)))
            m2 = jnp.concatenate(ms, axis=1).astype(BF16)
            x_stack = _stack_two_heads(xdt16[:, pr * LANES:(pr + 1) * LANES])
            y_parts.append(jnp.dot(m2, x_stack, preferred_element_type=F32))
        return jnp.concatenate(y_parts, axis=1) + y_off

    def gate_stage(g, y):
        gs = lanes_of(g)
        y = y + dskip_ref[:, gs] * xs_of(g)
        zf = z_ref[:, gs].astype(F32)
        gated = y * (zf * _sigmoid(zf))
        o_ref[:, gs] = (_rms(gated) * nw_ref[:, gs]).astype(o_ref.dtype)

    _run_pipelined(SSM_N_GROUPS, state_stage, diag_stage, gate_stage)


def _ssd_mixer(zx, dt_raw, conv_w, conv_b, dt_bias, a_log, d_skip, norm_w, batch, seq):
    m = zx.shape[0]
    n_heads = a_log.shape[0]
    d_inner = n_heads * SSM_HEAD_DIM
    bc_dim = SSM_N_GROUPS * SSM_D_STATE
    conv_dim = d_inner + 2 * bc_dim
    assert zx.shape[1] == d_inner + conv_dim and seq % SSM_CHUNK == 0 and n_heads <= LANES
    assert d_inner % bc_dim == 0 and bc_dim % LANES == 0 and d_inner // SSM_N_GROUPS == 2 * LANES
    zx3 = zx.reshape(batch, seq, zx.shape[1])
    dt3 = dt_raw.reshape(batch, seq, LANES)
    pad_heads = lambda v: jnp.pad(v.astype(F32), (0, LANES - n_heads))[None, :]
    blk = lambda w: (None, SSM_CHUNK, w)
    const = lambda shape: pl.BlockSpec(shape, lambda b, c: (0,) * len(shape))
    r = d_inner // bc_dim
    out = pl.pallas_call(
        _ssd_kernel,
        out_shape=jax.ShapeDtypeStruct((batch, seq, d_inner), BF16),
        grid=(batch, seq // SSM_CHUNK),
        in_specs=[
            pl.BlockSpec(blk(d_inner), lambda b, c: (b, c, 0)),
            pl.BlockSpec(blk(d_inner), lambda b, c: (b, c, 1)),
            pl.BlockSpec(blk(bc_dim), lambda b, c: (b, c, 2 * r)),
            pl.BlockSpec(blk(bc_dim), lambda b, c: (b, c, 2 * r + 1)),
            pl.BlockSpec(blk(LANES), lambda b, c: (b, c, 0)),
            const((SSM_CONV, conv_dim)), const((1, conv_dim)), const((1, LANES)), const((1, LANES)),
            const((1, d_inner)), const((1, d_inner)), const((LANES, d_inner)),
        ],
        out_specs=pl.BlockSpec(blk(d_inner), lambda b, c: (b, c, 0)),
        scratch_shapes=[
            pltpu.VMEM((conv_dim // LANES, SSM_CHUNK + SUBLANES, LANES), F32),
            pltpu.VMEM((SSM_D_STATE, d_inner), F32),
            pltpu.VMEM((conv_dim // LANES, SSM_CHUNK, LANES), F32),
        ],
        compiler_params=pltpu.CompilerParams(dimension_semantics=("parallel", "arbitrary"),
                                             vmem_limit_bytes=_vmem_limit(40 * 1024 * 1024)),
        name="ssd_mixer",
    )(zx3, zx3, zx3, zx3, dt3, conv_w.astype(F32), conv_b.astype(F32)[None, :], pad_heads(dt_bias),
      pad_heads(a_log), jnp.repeat(d_skip.astype(F32), SSM_HEAD_DIM)[None, :], norm_w.astype(F32)[None, :],
      _head_expander(n_heads, SSM_HEAD_DIM))
    return out.reshape(m, d_inner)


def kernel(x, positions, norm_mix_w, norm_mlp_w, a_w_qkv, a_b_qkv, a_sinks, a_w_o, a_b_o, b_in_w, b_conv_w, b_conv_b, b_dt_bias, b_a_log, b_d, b_norm_w, b_out_w, c_w_qkv, c_w_o, mlp_w_up, mlp_w_down, final_norm_w):
    batch, seq, d_model = x.shape
    m = batch * seq
    depth = norm_mix_w.shape[0]
    h = x.reshape(m, d_model).astype(F32)
    rope = _rope_tables(positions.reshape(m, 1))

    for i in range(depth):
        kind, j = i % N_MIXERS, i // N_MIXERS
        if kind == 0:
            q_dim = a_sinks.shape[1] * HEAD_DIM
            kv_dim = (a_w_qkv.shape[2] - q_dim) // 2
            n_rope = q_dim + kv_dim
            qkv = _linear(h, _pair_interleave(a_w_qkv[j], n_rope).astype(BF16), gain=norm_mix_w[i],
                          bias=_pair_interleave(a_b_qkv[j], n_rope), rope=rope, rope_cols=n_rope, q_cols=q_dim,
                          name="swa_qkv")
            mix, w_o, b_o = _swa_attention(qkv, a_sinks[j], batch, seq), a_w_o[j], a_b_o[j]
        elif kind == 1:
            n_heads = b_a_log.shape[1]
            w_in = b_in_w[j]
            main = w_in.shape[1] - n_heads
            w_dt = jnp.pad(w_in[:, main:], ((0, 0), (0, LANES - n_heads))).astype(BF16)
            zx, dt_raw = _linear(h, w_in[:, :main].astype(BF16), gain=norm_mix_w[i], side_w=w_dt, tm=512,
                                 name="ssm_in")
            mix = _ssd_mixer(zx, dt_raw, b_conv_w[j], b_conv_b[j], b_dt_bias[j], b_a_log[j], b_d[j], b_norm_w[j],
                             batch, seq)
            w_o, b_o = b_out_w[j], None
        else:
            width = c_w_qkv.shape[2]
            n_rope = 2 * width // 3
            qkv = _linear(h, _pair_interleave(c_w_qkv[j], n_rope).astype(BF16), gain=norm_mix_w[i], rope=rope,
                          rope_cols=n_rope, q_cols=width // 3, out_dtype=F32, tm=256, name="dilated_qkv")
            mix, w_o, b_o = _dilated_attention(qkv, batch, seq), c_w_o[j], None
        final_gain = final_norm_w if i == depth - 1 else None
        h = _mix_out_mlp(h, mix, w_o.astype(BF16), b_o, norm_mlp_w[i], mlp_w_up[i].astype(BF16),
                         mlp_w_down[i].astype(BF16), final_gain)
    return h.reshape(batch, seq, d_model).astype(x.dtype)
```

```python
import functools
import math

import jax
import jax.numpy as jnp
from jax import lax
from jax.experimental import pallas as pl
from jax.experimental.pallas import tpu as pltpu

F32 = jnp.float32
BF16 = jnp.bfloat16

N_MIXERS = 3
ATTN_BLOCK = 128
HEAD_DIM = 64
ROPE_THETA = 10000.0
NORM_EPS = 1e-5
A_Q_PER_KV = 8
A_WINDOW = 128
C_PATTERNS = ((128, 1), (512, 4), (2048, 16))
SSM_HEAD_DIM = 64
SSM_N_GROUPS = 8
SSM_D_STATE = 128
SSM_CONV = 4
SSM_CHUNK = 128

LANES = 128
LOG2_E = math.log2(math.e)
SUBLANES = 8
V7X_VMEM_LIMIT_CAP = 56 * 1024 * 1024


def _vmem_limit(nbytes):
    return int(min(max(nbytes, 16 * 1024 * 1024), V7X_VMEM_LIMIT_CAP))


def _row_tile(m):
    for t in (1024, 512, 256, 128):
        if m % t == 0:
            return t
    raise ValueError(f"token count {m} must be a multiple of 128")


def _rms(xf):
    return xf * lax.rsqrt(jnp.mean(xf * xf, axis=-1, keepdims=True) + NORM_EPS)


def _sigmoid(v):
    return 1.0 / (1.0 + jnp.exp2(v * (-LOG2_E)))


def _split_bf16(v, n):
    parts, r = [], v
    for _ in range(n):
        p = r.astype(BF16)
        parts.append(p)
        r = r - p.astype(F32)
    return parts


def _dot_split(parts, rhs):
    acc = None
    for p in parts:
        d = jnp.dot(p, rhs, preferred_element_type=F32)
        acc = d if acc is None else acc + d
    return acc


def _expand_heads(v, e_bf16, terms=2):
    return _dot_split(_split_bf16(v, terms), e_bf16)


def _head_expander(n_heads, head_dim):
    rows = jnp.arange(LANES)[:, None]
    cols = jnp.arange(n_heads * head_dim)[None, :] // head_dim
    return (rows == cols).astype(BF16)


ROPE_PACK = LANES // (HEAD_DIM // 2)


def _rope_table_kernel(pos_ref, inv_ref, sign_ref, cos_ref, sin_ref):
    packed_rows = pos_ref.shape[0]
    width = LANES // ROPE_PACK
    quarter = lax.broadcasted_iota(jnp.int32, (packed_rows, LANES), 1) // width
    pos = pos_ref[...].astype(F32)
    packed_pos = pos[:, ROPE_PACK - 1:ROPE_PACK]
    for a in range(ROPE_PACK - 2, -1, -1):
        packed_pos = jnp.where(quarter == a, pos[:, a:a + 1], packed_pos)
    ang = packed_pos * inv_ref[...]
    for table, out_ref, signed in ((jnp.cos(ang), cos_ref, False), (jnp.sin(ang), sin_ref, True)):
        turned = [table] + [pltpu.roll(table, width * k, 1) for k in range(1, ROPE_PACK)]
        for a in range(ROPE_PACK):
            spread = turned[(ROPE_PACK - 1 - a) % ROPE_PACK]
            for ql in range(ROPE_PACK - 1):
                spread = jnp.where(quarter == ql, turned[(ql - a) % ROPE_PACK], spread)
            if signed:
                spread = spread * sign_ref[...]
            out_ref[pl.ds(a, packed_rows, stride=ROPE_PACK), :] = spread


def _rope_tables(pos_col):
    m = pos_col.shape[0]
    tm = _row_tile(m)
    half = HEAD_DIM // 2
    inv = ROPE_THETA ** (-jnp.arange(half, dtype=F32) / half)
    inv_l = jnp.tile(inv, LANES // half)[None, :]
    sign_l = jnp.where(jnp.arange(LANES) < LANES // 2, -1.0, 1.0).astype(F32)[None, :]
    row = pl.BlockSpec((tm, LANES), lambda i: (i, 0))
    const = pl.BlockSpec((1, LANES), lambda i: (0, 0))
    return pl.pallas_call(
        _rope_table_kernel,
        out_shape=(jax.ShapeDtypeStruct((m, LANES), F32), jax.ShapeDtypeStruct((m, LANES), F32)),
        grid=(m // tm,),
        in_specs=[pl.BlockSpec((tm // ROPE_PACK, ROPE_PACK), lambda i: (i, 0)), const, const],
        out_specs=(row, row),
        compiler_params=pltpu.CompilerParams(dimension_semantics=("parallel",)),
        name="rope_tables",
    )(pos_col.reshape(m // ROPE_PACK, ROPE_PACK), inv_l, sign_l)


def _pair_interleave(w, n_rope):
    lead = w.shape[:-1]
    head = w[..., :n_rope].reshape(*lead, n_rope // LANES, 4, LANES // 4)
    head = jnp.stack([head[..., 0, :], head[..., 2, :], head[..., 1, :], head[..., 3, :]], axis=-2)
    return jnp.concatenate([head.reshape(*lead, n_rope), w[..., n_rope:]], axis=-1)


def _rope_lanes(yc, cos, sin_signed):
    return yc * cos + pltpu.roll(yc, LANES // 2, 1) * sin_signed


MXU_COLS = 256


def _linear_kernel(*refs, has_gain, has_bias, has_res, has_side, rope_cols, q_cols, n_cols, prep_x):
    refs = list(refs)
    x_ref, w_ref = refs.pop(0), refs.pop(0)
    g_ref = refs.pop(0) if has_gain else None
    b_ref = refs.pop(0) if has_bias else None
    cos_ref, sin_ref = (refs.pop(0), refs.pop(0)) if rope_cols else (None, None)
    r_ref = refs.pop(0) if has_res else None
    ws_ref = refs.pop(0) if has_side else None
    o_ref = refs.pop(0)
    side_ref = refs.pop(0) if has_side else None
    j = pl.program_id(1)
    tn = o_ref.shape[1]

    def prepared():
        xf = x_ref[...].astype(F32)
        if has_gain:
            xf = _rms(xf) * g_ref[...]
        return xf.astype(BF16)

    if not prep_x:
        xb = x_ref[...]
    elif tn == n_cols:
        xb = prepared()
    else:
        xn_ref = refs.pop(0)

        @pl.when(j == 0)
        def _():
            xn_ref[...] = prepared()
            if has_side:
                side_ref[...] = jnp.dot(xn_ref[...], ws_ref[...], preferred_element_type=F32)
        xb = xn_ref[...]
    if has_side and (not prep_x or tn == n_cols):
        side_ref[...] = jnp.dot(xb, ws_ref[...], preferred_element_type=F32)

    tiled_rope = rope_cols and tn < n_cols
    if rope_cols:
        cos, sin = cos_ref[...], sin_ref[...]
        if tiled_rope:
            is_rope = j < rope_cols // tn
            cos, sin = jnp.where(is_rope, cos, 1.0), jnp.where(is_rope, sin, 0.0)

    if q_cols:
        assert rope_cols and not tiled_rope and q_cols <= rope_cols
        cos_q, sin_q = cos * Q_SCALE, sin * Q_SCALE

    step = MXU_COLS if tn % MXU_COLS == 0 else LANES
    for c0 in range(0, tn, step):
        acc = jnp.dot(xb, w_ref[:, c0:c0 + step], preferred_element_type=F32)
        if has_bias:
            acc = acc + b_ref[:, c0:c0 + step]
        if has_res:
            acc = acc + r_ref[:, c0:c0 + step]
        for l0 in range(0, step, LANES):
            yc = acc[:, l0:l0 + LANES]
            if rope_cols and (tiled_rope or c0 + l0 < rope_cols):
                yc = _rope_lanes(yc, cos_q, sin_q) if c0 + l0 < q_cols else _rope_lanes(yc, cos, sin)
            o_ref[:, c0 + l0:c0 + l0 + LANES] = yc.astype(o_ref.dtype)


def _linear(x, w, *, gain=None, bias=None, rope=None, rope_cols=0, q_cols=0, residual=None, side_w=None,
            out_dtype=BF16, tm=None, tn=None, name="linear"):
    m, k = x.shape
    n = w.shape[1]
    tm = _row_tile(m) if tm is None else tm
    tn = n if tn is None else tn
    assert m % tm == 0 and n % tn == 0 and tn % LANES == 0
    if rope_cols and rope_cols % tn:
        assert tn == n and rope_cols % LANES == 0
    prep_x = gain is not None or x.dtype != BF16
    row = lambda i, j: (i, 0)
    col = lambda i, j: (0, j)
    const = lambda i, j: (0, 0)
    w_spec = pl.BlockSpec((k, tn), col, pipeline_mode=pl.Buffered(1)) if tn == n else pl.BlockSpec((k, tn), col)
    args, specs = [x, w], [pl.BlockSpec((tm, k), row), w_spec]
    if gain is not None:
        args.append(gain.reshape(1, k).astype(F32))
        specs.append(pl.BlockSpec((1, k), const))
    if bias is not None:
        args.append(bias.reshape(1, n).astype(F32))
        specs.append(pl.BlockSpec((1, tn), col))
    if rope_cols:
        args += list(rope)
        specs += [pl.BlockSpec((tm, LANES), row)] * 2
    if residual is not None:
        args.append(residual)
        specs.append(pl.BlockSpec((tm, tn), lambda i, j: (i, j)))
    out_shape = jax.ShapeDtypeStruct((m, n), out_dtype)
    out_specs = pl.BlockSpec((tm, tn), lambda i, j: (i, j))
    if side_w is not None:
        args.append(side_w)
        specs.append(pl.BlockSpec((k, LANES), const))
        out_shape = (out_shape, jax.ShapeDtypeStruct((m, LANES), F32))
        out_specs = (out_specs, pl.BlockSpec((tm, LANES), row))
    out_bytes = jnp.dtype(out_dtype).itemsize
    est = (2 * tm * k * x.dtype.itemsize + 2 * k * tn * 2 + 2 * tm * tn * out_bytes
           + (2 * tm * tn * 4 if residual is not None else 0) + tm * k * 2 + 2 * tm * k * 4
           + 4 * tm * MXU_COLS * 4 + 6 * tm * LANES * 4)
    return pl.pallas_call(
        functools.partial(_linear_kernel, has_gain=gain is not None, has_bias=bias is not None,
                          has_res=residual is not None, has_side=side_w is not None, rope_cols=rope_cols,
                          q_cols=q_cols, n_cols=n, prep_x=prep_x),
        out_shape=out_shape,
        grid=(m // tm, n // tn),
        in_specs=specs,
        out_specs=out_specs,
        scratch_shapes=[pltpu.VMEM((tm, k), BF16)] if prep_x and tn < n else [],
        compiler_params=pltpu.CompilerParams(dimension_semantics=("parallel", "arbitrary"),
                                             vmem_limit_bytes=_vmem_limit(est)),
        name=name,
    )(*args)


MLP_ROWS = 1024
MLP_FF_CHUNK = 512


def _mlp_kernel(*refs, has_bias, has_final):
    refs = list(refs)
    x_ref, a_ref, wo_ref = refs.pop(0), refs.pop(0), refs.pop(0)
    bo_ref = refs.pop(0) if has_bias else None
    g_ref, wu_ref, wd_ref = refs.pop(0), refs.pop(0), refs.pop(0)
    gf_ref = refs.pop(0) if has_final else None
    o_ref = refs.pop(0)
    x = x_ref[...] + jnp.dot(a_ref[...], wo_ref[...], preferred_element_type=F32)
    if has_bias:
        x = x + bo_ref[...]
    xn = (_rms(x) * g_ref[...]).astype(BF16)
    y = x
    ff = wu_ref.shape[1]
    step = MLP_FF_CHUNK if ff % MLP_FF_CHUNK == 0 else ff
    for c0 in range(0, ff, step):
        u = jnp.maximum(jnp.dot(xn, wu_ref[:, c0:c0 + step], preferred_element_type=F32), 0.0)
        y = y + jnp.dot((u * u).astype(BF16), wd_ref[c0:c0 + step, :], preferred_element_type=F32)
    if has_final:
        y = _rms(y) * gf_ref[...]
    o_ref[...] = y


def _mix_out_mlp(h, mix, w_o, b_o, gain, w_up, w_down, layer, final_gain=None):
    m, d = h.shape
    k = mix.shape[1]
    ff = w_up.shape[2]
    tm = MLP_ROWS if m % MLP_ROWS == 0 else _row_tile(m)
    layer_block = lambda shape: pl.BlockSpec((None,) + shape, lambda i: (layer, 0, 0),
                                             pipeline_mode=pl.Buffered(1))
    row = lambda i: (i, 0)
    const = lambda i: (0, 0)
    vec = pl.BlockSpec((1, d), const)
    resident = lambda shape: pl.BlockSpec(shape, const, pipeline_mode=pl.Buffered(1))
    args = [h, mix, w_o]
    specs = [pl.BlockSpec((tm, d), row), pl.BlockSpec((tm, k), row), resident((k, d))]
    if b_o is not None:
        args.append(b_o.reshape(1, d).astype(F32))
        specs.append(vec)
    args += [gain.reshape(1, d).astype(F32), w_up, w_down]
    specs += [vec, layer_block((d, ff)), layer_block((ff, d))]
    if final_gain is not None:
        args.append(final_gain.reshape(1, d).astype(F32))
        specs.append(vec)
    est = 4 * d * ff + 2 * k * d + 4 * tm * k + 4 * tm * d * 4 + 4 * tm * d * 4 + 3 * tm * MLP_FF_CHUNK * 4
    return pl.pallas_call(
        functools.partial(_mlp_kernel, has_bias=b_o is not None, has_final=final_gain is not None),
        out_shape=jax.ShapeDtypeStruct((m, d), F32),
        grid=(m // tm,),
        in_specs=specs,
        out_specs=pl.BlockSpec((tm, d), row),
        compiler_params=pltpu.CompilerParams(dimension_semantics=("parallel",),
                                             vmem_limit_bytes=_vmem_limit(est)),
        name="sqrelu_mlp",
    )(*args)


Q_SCALE = HEAD_DIM ** -0.5 * LOG2_E
STAGE_STRIDE = 4
LAST_STAGE_LAG = 1
ATTN_PIPE_DEPTH = 2


def _low_head_lanes(shape, interleaved):
    lane = lax.broadcasted_iota(jnp.int32, shape, 1)
    return (lane % HEAD_DIM) < HEAD_DIM // 2 if interleaved else lane < HEAD_DIM


def _stack_two_heads(t, interleaved=False):
    low = _low_head_lanes(t.shape, interleaved)
    zero = jnp.zeros_like(t)
    return jnp.concatenate([jnp.where(low, t, zero), jnp.where(low, zero, t)], axis=0)


def _init_band_consts(bias_ref, ones_ref, max_dist):
    r = lax.broadcasted_iota(jnp.int32, (ATTN_BLOCK, 2 * ATTN_BLOCK), 0)
    c = lax.broadcasted_iota(jnp.int32, (ATTN_BLOCK, 2 * ATTN_BLOCK), 1)
    dist = r + ATTN_BLOCK - c
    bias_ref[...] = jnp.where((dist >= 0) & (dist <= max_dist), 0.0, -jnp.inf)
    low = _low_head_lanes((2 * ATTN_BLOCK, LANES), False)
    ones_ref[0:2 * ATTN_BLOCK, :] = jnp.where(low, 1.0, 0.0).astype(BF16)
    ones_ref[2 * ATTN_BLOCK:, :] = jnp.where(low, 0.0, 1.0).astype(BF16)


def _head_pieces(t, interleaved):
    low = _low_head_lanes(t.shape, interleaved)
    return jnp.where(low, t, 0.0).astype(BF16), jnp.where(low, 0.0, t).astype(BF16)


def _piece_cache(load, interleaved):
    cache = {}

    def get(key):
        if key not in cache:
            cache[key] = _head_pieces(load(key), interleaved)
        return cache[key]
    return get


def _stack_blocks(prev, cur):
    return jnp.concatenate(list(cur) if prev is None else [prev[0], cur[0], prev[1], cur[1]], axis=0)


def _band_bias(bias_ref, first):
    return bias_ref[:, ATTN_BLOCK:] if first else bias_ref[...]


def _ones_stack(ones_ref, first):
    if not first:
        return ones_ref[...]
    half = ones_ref.shape[0] // 2
    return jnp.concatenate([ones_ref[half - ATTN_BLOCK:half, :], ones_ref[2 * half - ATTN_BLOCK:, :]], axis=0)


def _scores(q2, k_stack):
    return lax.dot_general(q2, k_stack, (((1,), (1,)), ((), ())), preferred_element_type=F32)


def _probs(s, bias, sinks):
    kw = s.shape[1] // 2
    s0, s1 = s[:, :kw] + bias, s[:, kw:] + bias
    m0 = jnp.max(s0, axis=-1, keepdims=True)
    m1 = jnp.max(s1, axis=-1, keepdims=True)
    if sinks is not None:
        m0, m1 = jnp.maximum(m0, sinks[0]), jnp.maximum(m1, sinks[1])
    p2 = jnp.concatenate([jnp.exp2(s0 - m0), jnp.exp2(s1 - m1)], axis=1).astype(BF16)
    return p2, jnp.where(_low_head_lanes((s.shape[0], LANES), False), m0, m1)


def _weighted(p2, m_l, v_stack, ones_stack, sinks):
    od = jnp.dot(p2, jnp.concatenate([v_stack, ones_stack], axis=1), preferred_element_type=F32)
    o, den = od[:, :LANES], od[:, LANES:]
    if sinks is not None:
        den = den + jnp.exp2(jnp.where(_low_head_lanes(o.shape, False), sinks[0], sinks[1]) - m_l)
    return o / den, m_l + jnp.log2(den)


def _run_pipelined(n, first_stage, mid_stage, last_stage):
    ahead = {i: first_stage(i) for i in range(min(ATTN_PIPE_DEPTH, n))}
    mids = {}
    for i in range(n + LAST_STAGE_LAG):
        if i < n:
            mids[i] = mid_stage(i, ahead.pop(i))
            if i + ATTN_PIPE_DEPTH < n:
                ahead[i + ATTN_PIPE_DEPTH] = first_stage(i + ATTN_PIPE_DEPTH)
        if i >= LAST_STAGE_LAG:
            last_stage(i - LAST_STAGE_LAG, mids.pop(i - LAST_STAGE_LAG))


def _swa_kernel(sink_ref, q_ref, k_ref, v_ref, o_ref, kb_ref, vb_ref, bias_ref, ones_ref):
    seq = q_ref.shape[0]
    p = pl.program_id(1)
    pairs_per_kv = A_Q_PER_KV // 2
    rows = 2 * ATTN_BLOCK

    @pl.when(p == 0)
    def _():
        _init_band_consts(bias_ref, ones_ref, A_WINDOW - 1)

    def spread_kv_head(kv):
        k_low, v_low = _low_head_lanes((rows, LANES), True), _low_head_lanes((rows, LANES), False)
        k_own, v_own = (k_low, v_low) if kv == 0 else (jnp.logical_not(k_low), jnp.logical_not(v_low))
        k_shift = HEAD_DIM // 2 if kv == 0 else LANES - HEAD_DIM // 2
        for c in range(seq // rows):
            sl = slice(c * rows, (c + 1) * rows)
            kt, vt = k_ref[sl, :].astype(F32), v_ref[sl, :].astype(F32)
            kt = jnp.where(k_own, kt, pltpu.roll(kt, k_shift, 1))
            vt = jnp.where(v_own, vt, pltpu.roll(vt, HEAD_DIM, 1))
            for dst, t, low in ((kb_ref, kt, k_low), (vb_ref, vt, v_low)):
                dst[0, sl, :] = jnp.where(low, t, 0.0).astype(BF16)
                dst[1, sl, :] = jnp.where(low, 0.0, t).astype(BF16)

    for kv in range(2):
        pl.when(p == kv * pairs_per_kv)(functools.partial(spread_kv_head, kv))

    sinks = (sink_ref[2 * p] * LOG2_E, sink_ref[2 * p + 1] * LOG2_E)
    blk = lambda i: slice(i * ATTN_BLOCK, (i + 1) * ATTN_BLOCK)

    def stack(ref, i):
        both = slice(max(i - 1, 0) * ATTN_BLOCK, (i + 1) * ATTN_BLOCK)
        return jnp.concatenate([ref[0, both, :], ref[1, both, :]], axis=0)

    def score_stage(i):
        return _scores(q_ref[blk(i), :], stack(kb_ref, i))

    def prob_stage(i, s):
        return _probs(s, _band_bias(bias_ref, i == 0), sinks)

    def out_stage(i, mid):
        o, _ = _weighted(*mid, stack(vb_ref, i), _ones_stack(ones_ref, i == 0), sinks)
        o_ref[blk(i), :] = o.astype(o_ref.dtype)

    _run_pipelined(seq // ATTN_BLOCK, score_stage, prob_stage, out_stage)


def _swa_attention(qkv, sinks, batch, seq):
    m = qkv.shape[0]
    q_dim = sinks.shape[0] * HEAD_DIM
    assert q_dim // HEAD_DIM // A_Q_PER_KV == 2 and seq % (2 * ATTN_BLOCK) == 0
    q3 = qkv.reshape(batch, seq, qkv.shape[1])
    kcol, vcol = q_dim // LANES, q_dim // LANES + 1
    slab = (None, seq, LANES)
    out = pl.pallas_call(
        _swa_kernel,
        out_shape=jax.ShapeDtypeStruct((batch, seq, q_dim), BF16),
        grid=(batch, q_dim // LANES),
        in_specs=[
            pl.BlockSpec(memory_space=pltpu.SMEM),
            pl.BlockSpec(slab, lambda b, p: (b, 0, p)),
            pl.BlockSpec(slab, lambda b, p: (b, 0, kcol)),
            pl.BlockSpec(slab, lambda b, p: (b, 0, vcol)),
        ],
        out_specs=pl.BlockSpec(slab, lambda b, p: (b, 0, p)),
        scratch_shapes=[pltpu.VMEM((2, seq, LANES), BF16), pltpu.VMEM((2, seq, LANES), BF16),
                        pltpu.VMEM((ATTN_BLOCK, 2 * ATTN_BLOCK), F32), pltpu.VMEM((4 * ATTN_BLOCK, LANES), BF16)],
        compiler_params=pltpu.CompilerParams(dimension_semantics=("parallel", "arbitrary"),
                                             vmem_limit_bytes=_vmem_limit(16 * seq * LANES * 2)),
        name="swa_sink_attention",
    )(sinks.astype(F32), q3, q3, q3)
    return out.reshape(m, q_dim)


def _dilated_kernel(*refs):
    n_groups = len(C_PATTERNS)
    in_refs, o_ref = refs[:3 * n_groups], refs[3 * n_groups]
    acc_o, acc_l, bias_ref, ones_ref, stage_ref = refs[3 * n_groups + 1:]
    seq = o_ref.shape[0]
    max_dist = C_PATTERNS[0][0] // C_PATTERNS[0][1]
    assert all(w // d == max_dist for w, d in C_PATTERNS)
    _init_band_consts(bias_ref, ones_ref, max_dist)

    order = sorted(range(n_groups), key=lambda g: -C_PATTERNS[g][1])
    blocks = []
    for g in order:
        dil = C_PATTERNS[g][1]
        for r in range(dil):
            for i in range(seq // dil // ATTN_BLOCK):
                blocks.append((g, dil, r, i))

    def rows(dil, r, i):
        start = i * ATTN_BLOCK * dil + r
        return pl.ds(start, ATTN_BLOCK, stride=dil) if dil > 1 else pl.ds(start, ATTN_BLOCK)

    staged = {g for g in range(n_groups) if C_PATTERNS[g][1] % (4 * STAGE_STRIDE) == 0}
    assert len(staged) <= 1
    for g in staged:
        for which in range(3):
            for a in range(STAGE_STRIDE):
                stage_ref[which, a] = in_refs[3 * g + which][pl.ds(a, seq // STAGE_STRIDE, stride=STAGE_STRIDE), :]

    def read(which, g, dil, r, i):
        if g not in staged:
            return in_refs[3 * g + which][rows(dil, r, i), :]
        inner = dil // STAGE_STRIDE
        return stage_ref[which, r % STAGE_STRIDE,
                         pl.ds(i * ATTN_BLOCK * inner + r // STAGE_STRIDE, ATTN_BLOCK, stride=inner), :]

    def loader(which):
        return lambda key: read(which, *key)

    k_pieces, v_pieces = _piece_cache(loader(1), True), _piece_cache(loader(2), False)

    def stack(pieces, g, dil, r, i):
        return _stack_blocks(pieces((g, dil, r, i - 1)) if i else None, pieces((g, dil, r, i)))

    def score_stage(n):
        g, dil, r, i = blocks[n]
        q2 = read(0, g, dil, r, i).astype(BF16)
        return _scores(q2, stack(k_pieces, g, dil, r, i))

    def prob_stage(n, s):
        return _probs(s, _band_bias(bias_ref, blocks[n][3] == 0), None)

    def out_stage(n, mid):
        g, dil, r, i = blocks[n]
        cur = rows(dil, r, i)
        o, lse = _weighted(*mid, stack(v_pieces, g, dil, r, i), _ones_stack(ones_ref, i == 0), None)
        if g != order[0]:
            o_old, l_old = acc_o[cur, :], acc_l[cur, :]
            l_max = jnp.maximum(l_old, lse)
            e_old, e_new = jnp.exp2(l_old - l_max), jnp.exp2(lse - l_max)
            inv = 1.0 / (e_old + e_new)
            o = o_old * (e_old * inv) + o * (e_new * inv)
            lse = l_max + jnp.log2(e_old + e_new)
        acc_o[cur, :] = o
        if g != order[-1]:
            acc_l[cur, :] = lse

    _run_pipelined(len(blocks), score_stage, prob_stage, out_stage)
    o_ref[...] = acc_o[...].astype(o_ref.dtype)


def _dilated_attention(qkv, batch, seq):
    m, width = qkv.shape
    n_groups = len(C_PATTERNS)
    gw = width // (3 * n_groups)
    assert gw % LANES == 0 and all(seq % (d * ATTN_BLOCK) == 0 for _, d in C_PATTERNS)
    pairs = gw // LANES
    q3 = qkv.reshape(batch, seq, width)
    slab = lambda c: pl.BlockSpec((None, seq, LANES), lambda b, p: (b, 0, c * pairs + p))
    specs = []
    for g in range(n_groups):
        specs += [slab(g), slab(n_groups + g), slab(2 * n_groups + g)]
    out = pl.pallas_call(
        _dilated_kernel,
        out_shape=jax.ShapeDtypeStruct((batch, seq, gw), BF16),
        grid=(batch, pairs),
        in_specs=specs,
        out_specs=pl.BlockSpec((None, seq, LANES), lambda b, p: (b, 0, p)),
        scratch_shapes=[pltpu.VMEM((seq, LANES), F32), pltpu.VMEM((seq, LANES), F32),
                        pltpu.VMEM((ATTN_BLOCK, 2 * ATTN_BLOCK), F32), pltpu.VMEM((4 * ATTN_BLOCK, LANES), BF16),
                        pltpu.VMEM((3, STAGE_STRIDE, seq // STAGE_STRIDE, LANES), F32)],
        compiler_params=pltpu.CompilerParams(
            dimension_semantics=("parallel", "parallel"),
            vmem_limit_bytes=_vmem_limit(2 * (3 * n_groups + 2) * seq * LANES * 4 + 8 * 1024 * 1024)),
        name="dilated_attention",
    )(*([q3] * (3 * n_groups)))
    return out.reshape(m, gw)


def _ssd_kernel(z_ref, x_ref, b_ref, c_ref, dt_ref, cw_ref, cb_ref, dtb_ref, alog_ref, dskip_ref, nw_ref, e_ref,
                o_ref, xpad_ref, state_ref, conv_ref):
    q = SSM_CHUNK
    d_inner = x_ref.shape[1]
    bc_dim = b_ref.shape[1]
    n_heads = d_inner // SSM_HEAD_DIM
    group_w = d_inner // SSM_N_GROUPS
    pad = SUBLANES
    n_slabs = xpad_ref.shape[0]
    x_slabs, bc_slabs = d_inner // LANES, bc_dim // LANES

    @pl.when(pl.program_id(1) == 0)
    def _():
        xpad_ref[:, 0:pad, :] = jnp.zeros((n_slabs, pad, LANES), F32)
        state_ref[...] = jnp.zeros_like(state_ref)

    def conv_slabs(s_lo, s_hi):
        for s in range(s_lo, s_hi):
            src, s0 = (x_ref, s) if s < x_slabs else (b_ref, s - x_slabs) if s < x_slabs + bc_slabs else (
                c_ref, s - x_slabs - bc_slabs)
            xpad_ref[s, pad:pad + q, :] = src[:, s0 * LANES:(s0 + 1) * LANES].astype(F32)
            lanes = slice(s * LANES, (s + 1) * LANES)
            for parity in range(2):
                acc = cb_ref[:, lanes]
                for k in range(SSM_CONV):
                    tap = xpad_ref[s, pl.ds(pad - k + parity, q // 2, stride=2), :]
                    acc = acc + cw_ref[SSM_CONV - 1 - k:SSM_CONV - k, lanes] * tap
                conv_ref[s, pl.ds(parity, q // 2, stride=2), :] = acc * _sigmoid(acc)

    head_lane = lax.broadcasted_iota(jnp.int32, (q, LANES), 1) < n_heads
    dtr = dt_ref[...] + dtb_ref[...]
    dt = jnp.maximum(dtr, 0.0) + jnp.log(1.0 + jnp.exp(-jnp.abs(dtr)))
    dt = jnp.where(head_lane, dt, 0.0)
    d_a = dt * (-jnp.exp(alog_ref[...]))
    rr = lax.broadcasted_iota(jnp.int32, (q, q), 0)
    cc = lax.broadcasted_iota(jnp.int32, (q, q), 1)
    tril = rr >= cc
    tri16 = jnp.where(tril, 1.0, 0.0).astype(BF16)
    d_a_parts = _split_bf16(d_a, 3)
    conv_slabs(0, n_slabs // 4)
    cs = None
    for part in d_a_parts:
        term = jnp.dot(tri16, part, preferred_element_type=F32)
        cs = term if cs is None else cs + term
    conv_slabs(n_slabs // 4, n_slabs // 2)
    cs2 = cs * LOG2_E
    cs2_t = cs2.T
    ecs = jnp.exp(cs)
    dec = jnp.exp(cs[q - 1:q, :] - cs)
    conv_slabs(n_slabs // 2, 3 * n_slabs // 4)
    e16 = e_ref[...]
    dt_e = _expand_heads(dt, e16, terms=1)
    ecs_e = _expand_heads(ecs, e16)
    dec_e = _expand_heads(dec, e16, terms=1)
    chunk_decay_e = ecs_e[q - 1:q, :]
    conv_slabs(3 * n_slabs // 4, n_slabs)
    xpad_ref[:, 0:pad, :] = xpad_ref[:, q:q + pad, :]

    heads_per_group = n_heads // SSM_N_GROUPS
    lanes_of = lambda g: slice(g * group_w, (g + 1) * group_w)
    slabs_per_group = group_w // LANES
    xs_of = lambda g: jnp.concatenate(
        [conv_ref[g * slabs_per_group + j] for j in range(slabs_per_group)], axis=1)

    def state_stage(g):
        gs = lanes_of(g)
        b_f = conv_ref[x_slabs + g]
        c16 = conv_ref[x_slabs + bc_slabs + g].astype(BF16)
        cb = lax.dot_general(c16, b_f.astype(BF16), (((1,), (1,)), ((), ())), preferred_element_type=F32)
        xdt = xs_of(g) * dt_e[:, gs]
        st_old = state_ref[:, gs]
        y_off = jnp.dot(c16, st_old.astype(BF16), preferred_element_type=F32) * ecs_e[:, gs]
        st_new = jnp.dot(b_f.T.astype(BF16), (xdt * dec_e[:, gs]).astype(BF16), preferred_element_type=F32)
        state_ref[:, gs] = st_old * chunk_decay_e[:, gs] + st_new
        return cb, y_off, xdt.astype(BF16)

    def diag_stage(g, carried):
        cb, y_off, xdt16 = carried
        y_parts = []
        for pr in range(heads_per_group // 2):
            h0 = g * heads_per_group + 2 * pr
            ms = []
            for hh in (h0, h0 + 1):
                diff = cs2[:, hh:hh + 1] - cs2_t[hh:hh + 1, :]
                ms.append(cb * jnp.exp2(jnp.where(tril, diff, -jnp.inf)))
            m2 = jnp.concatenate(ms, axis=1).astype(BF16)
            x_stack = _stack_two_heads(xdt16[:, pr * LANES:(pr + 1) * LANES])
            y_parts.append(jnp.dot(m2, x_stack, preferred_element_type=F32))
        return jnp.concatenate(y_parts, axis=1) + y_off

    def gate_stage(g, y):
        gs = lanes_of(g)
        y = y + dskip_ref[:, gs] * xs_of(g)
        zf = z_ref[:, gs].astype(F32)
        gated = y * (zf * _sigmoid(zf))
        o_ref[:, gs] = (_rms(gated) * nw_ref[:, gs]).astype(o_ref.dtype)

    _run_pipelined(SSM_N_GROUPS, state_stage, diag_stage, gate_stage)


def _ssd_mixer(zx, dt_raw, conv_w, conv_b, dt_bias, a_log, d_skip, norm_w, batch, seq):
    m = zx.shape[0]
    n_heads = a_log.shape[0]
    d_inner = n_heads * SSM_HEAD_DIM
    bc_dim = SSM_N_GROUPS * SSM_D_STATE
    conv_dim = d_inner + 2 * bc_dim
    assert zx.shape[1] == d_inner + conv_dim and seq % SSM_CHUNK == 0 and n_heads <= LANES
    assert d_inner % bc_dim == 0 and bc_dim % LANES == 0 and d_inner // SSM_N_GROUPS == 2 * LANES
    zx3 = zx.reshape(batch, seq, zx.shape[1])
    dt3 = dt_raw.reshape(batch, seq, LANES)
    pad_heads = lambda v: jnp.pad(v.astype(F32), (0, LANES - n_heads))[None, :]
    blk = lambda w: (None, SSM_CHUNK, w)
    const = lambda shape: pl.BlockSpec(shape, lambda b, c: (0,) * len(shape))
    r = d_inner // bc_dim
    out = pl.pallas_call(
        _ssd_kernel,
        out_shape=jax.ShapeDtypeStruct((batch, seq, d_inner), BF16),
        grid=(batch, seq // SSM_CHUNK),
        in_specs=[
            pl.BlockSpec(blk(d_inner), lambda b, c: (b, c, 0)),
            pl.BlockSpec(blk(d_inner), lambda b, c: (b, c, 1)),
            pl.BlockSpec(blk(bc_dim), lambda b, c: (b, c, 2 * r)),
            pl.BlockSpec(blk(bc_dim), lambda b, c: (b, c, 2 * r + 1)),
            pl.BlockSpec(blk(LANES), lambda b, c: (b, c, 0)),
            const((SSM_CONV, conv_dim)), const((1, conv_dim)), const((1, LANES)), const((1, LANES)),
            const((1, d_inner)), const((1, d_inner)), const((LANES, d_inner)),
        ],
        out_specs=pl.BlockSpec(blk(d_inner), lambda b, c: (b, c, 0)),
        scratch_shapes=[
            pltpu.VMEM((conv_dim // LANES, SSM_CHUNK + SUBLANES, LANES), F32),
            pltpu.VMEM((SSM_D_STATE, d_inner), F32),
            pltpu.VMEM((conv_dim // LANES, SSM_CHUNK, LANES), F32),
        ],
        compiler_params=pltpu.CompilerParams(dimension_semantics=("parallel", "arbitrary"),
                                             vmem_limit_bytes=_vmem_limit(40 * 1024 * 1024)),
        name="ssd_mixer",
    )(zx3, zx3, zx3, zx3, dt3, conv_w.astype(F32), conv_b.astype(F32)[None, :], pad_heads(dt_bias),
      pad_heads(a_log), jnp.repeat(d_skip.astype(F32), SSM_HEAD_DIM)[None, :], norm_w.astype(F32)[None, :],
      _head_expander(n_heads, SSM_HEAD_DIM))
    return out.reshape(m, d_inner)


def kernel(x, positions, norm_mix_w, norm_mlp_w, a_w_qkv, a_b_qkv, a_sinks, a_w_o, a_b_o, b_in_w, b_conv_w, b_conv_b, b_dt_bias, b_a_log, b_d, b_norm_w, b_out_w, c_w_qkv, c_w_o, mlp_w_up, mlp_w_down, final_norm_w):
    batch, seq, d_model = x.shape
    m = batch * seq
    depth = norm_mix_w.shape[0]
    h = x.reshape(m, d_model).astype(F32)
    rope = _rope_tables(positions.reshape(m, 1))
    w_up_all, w_down_all = mlp_w_up.astype(BF16), mlp_w_down.astype(BF16)

    for i in range(depth):
        kind, j = i % N_MIXERS, i // N_MIXERS
        if kind == 0:
            q_dim = a_sinks.shape[1] * HEAD_DIM
            kv_dim = (a_w_qkv.shape[2] - q_dim) // 2
            n_rope = q_dim + kv_dim
            qkv = _linear(h, _pair_interleave(a_w_qkv[j], n_rope).astype(BF16), gain=norm_mix_w[i],
                          bias=_pair_interleave(a_b_qkv[j], n_rope), rope=rope, rope_cols=n_rope, q_cols=q_dim,
                          name="swa_qkv")
            mix, w_o, b_o = _swa_attention(qkv, a_sinks[j], batch, seq), a_w_o[j], a_b_o[j]
        elif kind == 1:
            n_heads = b_a_log.shape[1]
            w_in = b_in_w[j]
            main = w_in.shape[1] - n_heads
            w_dt = jnp.pad(w_in[:, main:], ((0, 0), (0, LANES - n_heads))).astype(BF16)
            zx, dt_raw = _linear(h, w_in[:, :main].astype(BF16), gain=norm_mix_w[i], side_w=w_dt, tm=512,
                                 name="ssm_in")
            mix = _ssd_mixer(zx, dt_raw, b_conv_w[j], b_conv_b[j], b_dt_bias[j], b_a_log[j], b_d[j], b_norm_w[j],
                             batch, seq)
            w_o, b_o = b_out_w[j], None
        else:
            width = c_w_qkv.shape[2]
            n_rope = 2 * width // 3
            qkv = _linear(h, _pair_interleave(c_w_qkv[j], n_rope).astype(BF16), gain=norm_mix_w[i], rope=rope,
                          rope_cols=n_rope, q_cols=width // 3, out_dtype=F32, tm=256, name="dilated_qkv")
            mix, w_o, b_o = _dilated_attention(qkv, batch, seq), c_w_o[j], None
        final_gain = final_norm_w if i == depth - 1 else None
        h = _mix_out_mlp(h, mix, w_o.astype(BF16), b_o, norm_mlp_w[i], w_up_all, w_down_all, i, final_gain)
    return h.reshape(batch, seq, d_model).astype(x.dtype)
```

```python
import functools
import math

import jax
import jax.numpy as jnp
from jax import lax
from jax.experimental import pallas as pl
from jax.experimental.pallas import tpu as pltpu

F32 = jnp.float32
BF16 = jnp.bfloat16

N_MIXERS = 3
ATTN_BLOCK = 128
HEAD_DIM = 64
ROPE_THETA = 10000.0
NORM_EPS = 1e-5
A_Q_PER_KV = 8
A_WINDOW = 128
C_PATTERNS = ((128, 1), (512, 4), (2048, 16))
SSM_HEAD_DIM = 64
SSM_N_GROUPS = 8
SSM_D_STATE = 128
SSM_CONV = 4
SSM_CHUNK = 128

LANES = 128
LOG2_E = math.log2(math.e)
SUBLANES = 8
V7X_VMEM_LIMIT_CAP = 56 * 1024 * 1024


def _vmem_limit(nbytes):
    return int(min(max(nbytes, 16 * 1024 * 1024), V7X_VMEM_LIMIT_CAP))


def _row_tile(m):
    for t in (1024, 512, 256, 128):
        if m % t == 0:
            return t
    raise ValueError(f"token count {m} must be a multiple of 128")


def _rms(xf):
    return xf * lax.rsqrt(jnp.mean(xf * xf, axis=-1, keepdims=True) + NORM_EPS)


def _sigmoid(v):
    return 1.0 / (1.0 + jnp.exp2(v * (-LOG2_E)))


def _split_bf16(v, n):
    parts, r = [], v
    for _ in range(n):
        p = r.astype(BF16)
        parts.append(p)
        r = r - p.astype(F32)
    return parts


def _dot_split(parts, rhs):
    acc = None
    for p in parts:
        d = jnp.dot(p, rhs, preferred_element_type=F32)
        acc = d if acc is None else acc + d
    return acc


def _expand_heads(v, e_bf16, terms=2):
    return _dot_split(_split_bf16(v, terms), e_bf16)


def _head_expander(n_heads, head_dim):
    rows = jnp.arange(LANES)[:, None]
    cols = jnp.arange(n_heads * head_dim)[None, :] // head_dim
    return (rows == cols).astype(BF16)


ROPE_PACK = LANES // (HEAD_DIM // 2)


def _rope_table_kernel(pos_ref, inv_ref, sign_ref, cos_ref, sin_ref):
    packed_rows = pos_ref.shape[0]
    width = LANES // ROPE_PACK
    quarter = lax.broadcasted_iota(jnp.int32, (packed_rows, LANES), 1) // width
    pos = pos_ref[...].astype(F32)
    packed_pos = pos[:, ROPE_PACK - 1:ROPE_PACK]
    for a in range(ROPE_PACK - 2, -1, -1):
        packed_pos = jnp.where(quarter == a, pos[:, a:a + 1], packed_pos)
    ang = packed_pos * inv_ref[...]
    for table, out_ref, signed in ((jnp.cos(ang), cos_ref, False), (jnp.sin(ang), sin_ref, True)):
        turned = [table] + [pltpu.roll(table, width * k, 1) for k in range(1, ROPE_PACK)]
        for a in range(ROPE_PACK):
            spread = turned[(ROPE_PACK - 1 - a) % ROPE_PACK]
            for ql in range(ROPE_PACK - 1):
                spread = jnp.where(quarter == ql, turned[(ql - a) % ROPE_PACK], spread)
            if signed:
                spread = spread * sign_ref[...]
            out_ref[pl.ds(a, packed_rows, stride=ROPE_PACK), :] = spread


def _rope_tables(pos_col):
    m = pos_col.shape[0]
    tm = _row_tile(m)
    half = HEAD_DIM // 2
    inv = ROPE_THETA ** (-jnp.arange(half, dtype=F32) / half)
    inv_l = jnp.tile(inv, LANES // half)[None, :]
    sign_l = jnp.where(jnp.arange(LANES) < LANES // 2, -1.0, 1.0).astype(F32)[None, :]
    row = pl.BlockSpec((tm, LANES), lambda i: (i, 0))
    const = pl.BlockSpec((1, LANES), lambda i: (0, 0))
    return pl.pallas_call(
        _rope_table_kernel,
        out_shape=(jax.ShapeDtypeStruct((m, LANES), F32), jax.ShapeDtypeStruct((m, LANES), F32)),
        grid=(m // tm,),
        in_specs=[pl.BlockSpec((tm // ROPE_PACK, ROPE_PACK), lambda i: (i, 0)), const, const],
        out_specs=(row, row),
        compiler_params=pltpu.CompilerParams(dimension_semantics=("parallel",)),
        name="rope_tables",
    )(pos_col.reshape(m // ROPE_PACK, ROPE_PACK), inv_l, sign_l)


def _pair_interleave(w, n_rope):
    lead = w.shape[:-1]
    head = w[..., :n_rope].reshape(*lead, n_rope // LANES, 4, LANES // 4)
    head = jnp.stack([head[..., 0, :], head[..., 2, :], head[..., 1, :], head[..., 3, :]], axis=-2)
    return jnp.concatenate([head.reshape(*lead, n_rope), w[..., n_rope:]], axis=-1)


def _rope_lanes(yc, cos, sin_signed):
    return yc * cos + pltpu.roll(yc, LANES // 2, 1) * sin_signed


MXU_COLS = 256


def _linear_kernel(*refs, has_gain, has_bias, has_res, has_side, rope_cols, q_cols, n_cols, prep_x):
    refs = list(refs)
    x_ref, w_ref = refs.pop(0), refs.pop(0)
    g_ref = refs.pop(0) if has_gain else None
    b_ref = refs.pop(0) if has_bias else None
    cos_ref, sin_ref = (refs.pop(0), refs.pop(0)) if rope_cols else (None, None)
    r_ref = refs.pop(0) if has_res else None
    ws_ref = refs.pop(0) if has_side else None
    o_ref = refs.pop(0)
    side_ref = refs.pop(0) if has_side else None
    j = pl.program_id(1)
    tn = o_ref.shape[1]

    def prepared():
        xf = x_ref[...].astype(F32)
        if has_gain:
            xf = _rms(xf) * g_ref[...]
        return xf.astype(BF16)

    if not prep_x:
        xb = x_ref[...]
    elif tn == n_cols:
        xb = prepared()
    else:
        xn_ref = refs.pop(0)

        @pl.when(j == 0)
        def _():
            xn_ref[...] = prepared()
            if has_side:
                side_ref[...] = jnp.dot(xn_ref[...], ws_ref[...], preferred_element_type=F32)
        xb = xn_ref[...]
    if has_side and (not prep_x or tn == n_cols):
        side_ref[...] = jnp.dot(xb, ws_ref[...], preferred_element_type=F32)

    tiled_rope = rope_cols and tn < n_cols
    if rope_cols:
        cos, sin = cos_ref[...], sin_ref[...]
        if tiled_rope:
            is_rope = j < rope_cols // tn
            cos, sin = jnp.where(is_rope, cos, 1.0), jnp.where(is_rope, sin, 0.0)

    if q_cols:
        assert rope_cols and not tiled_rope and q_cols <= rope_cols
        cos_q, sin_q = cos * Q_SCALE, sin * Q_SCALE

    step = MXU_COLS if tn % MXU_COLS == 0 else LANES
    tm = o_ref.shape[0]
    half = tm // 2 if rope_cols and tm >= 1024 else tm
    for r0 in range(0, tm, half):
        rs = slice(r0, r0 + half)
        for c0 in range(0, tn, step):
            acc = jnp.dot(xb[rs], w_ref[:, c0:c0 + step], preferred_element_type=F32)
            if has_bias:
                acc = acc + b_ref[:, c0:c0 + step]
            if has_res:
                acc = acc + r_ref[rs, c0:c0 + step]
            for l0 in range(0, step, LANES):
                yc = acc[:, l0:l0 + LANES]
                if rope_cols and (tiled_rope or c0 + l0 < rope_cols):
                    yc = (_rope_lanes(yc, cos_q[rs], sin_q[rs]) if c0 + l0 < q_cols
                          else _rope_lanes(yc, cos[rs], sin[rs]))
                o_ref[rs, c0 + l0:c0 + l0 + LANES] = yc.astype(o_ref.dtype)


def _linear(x, w, *, gain=None, bias=None, rope=None, rope_cols=0, q_cols=0, residual=None, side_w=None,
            out_dtype=BF16, tm=None, tn=None, name="linear"):
    m, k = x.shape
    n = w.shape[1]
    tm = _row_tile(m) if tm is None else tm
    tn = n if tn is None else tn
    assert m % tm == 0 and n % tn == 0 and tn % LANES == 0
    if rope_cols and rope_cols % tn:
        assert tn == n and rope_cols % LANES == 0
    prep_x = gain is not None or x.dtype != BF16
    row = lambda i, j: (i, 0)
    col = lambda i, j: (0, j)
    const = lambda i, j: (0, 0)
    w_spec = pl.BlockSpec((k, tn), col, pipeline_mode=pl.Buffered(1)) if tn == n else pl.BlockSpec((k, tn), col)
    args, specs = [x, w], [pl.BlockSpec((tm, k), row), w_spec]
    if gain is not None:
        args.append(gain.reshape(1, k).astype(F32))
        specs.append(pl.BlockSpec((1, k), const))
    if bias is not None:
        args.append(bias.reshape(1, n).astype(F32))
        specs.append(pl.BlockSpec((1, tn), col))
    if rope_cols:
        args += list(rope)
        specs += [pl.BlockSpec((tm, LANES), row)] * 2
    if residual is not None:
        args.append(residual)
        specs.append(pl.BlockSpec((tm, tn), lambda i, j: (i, j)))
    out_shape = jax.ShapeDtypeStruct((m, n), out_dtype)
    out_specs = pl.BlockSpec((tm, tn), lambda i, j: (i, j))
    if side_w is not None:
        args.append(side_w)
        specs.append(pl.BlockSpec((k, LANES), const))
        out_shape = (out_shape, jax.ShapeDtypeStruct((m, LANES), F32))
        out_specs = (out_specs, pl.BlockSpec((tm, LANES), row))
    out_bytes = jnp.dtype(out_dtype).itemsize
    est = (2 * tm * k * x.dtype.itemsize + 2 * k * tn * 2 + 2 * tm * tn * out_bytes
           + (2 * tm * tn * 4 if residual is not None else 0) + tm * k * 2 + 2 * tm * k * 4
           + 4 * tm * MXU_COLS * 4 + 6 * tm * LANES * 4)
    return pl.pallas_call(
        functools.partial(_linear_kernel, has_gain=gain is not None, has_bias=bias is not None,
                          has_res=residual is not None, has_side=side_w is not None, rope_cols=rope_cols,
                          q_cols=q_cols, n_cols=n, prep_x=prep_x),
        out_shape=out_shape,
        grid=(m // tm, n // tn),
        in_specs=specs,
        out_specs=out_specs,
        scratch_shapes=[pltpu.VMEM((tm, k), BF16)] if prep_x and tn < n else [],
        compiler_params=pltpu.CompilerParams(dimension_semantics=("parallel", "arbitrary"),
                                             vmem_limit_bytes=_vmem_limit(est)),
        name=name,
    )(*args)


MLP_ROWS = 1024
MLP_FF_CHUNK = 512


def _mlp_kernel(*refs, has_bias, has_final):
    refs = list(refs)
    x_ref, a_ref, wo_ref = refs.pop(0), refs.pop(0), refs.pop(0)
    bo_ref = refs.pop(0) if has_bias else None
    g_ref, wu_ref, wd_ref = refs.pop(0), refs.pop(0), refs.pop(0)
    gf_ref = refs.pop(0) if has_final else None
    o_ref = refs.pop(0)
    x = x_ref[...] + jnp.dot(a_ref[...], wo_ref[...], preferred_element_type=F32)
    if has_bias:
        x = x + bo_ref[...]
    xn = (_rms(x) * g_ref[...]).astype(BF16)
    y = x
    ff = wu_ref.shape[1]
    step = MLP_FF_CHUNK if ff % MLP_FF_CHUNK == 0 else ff
    for c0 in range(0, ff, step):
        u = jnp.maximum(jnp.dot(xn, wu_ref[:, c0:c0 + step], preferred_element_type=F32), 0.0)
        y = y + jnp.dot((u * u).astype(BF16), wd_ref[c0:c0 + step, :], preferred_element_type=F32)
    if has_final:
        y = _rms(y) * gf_ref[...]
    o_ref[...] = y


def _mix_out_mlp(h, mix, w_o, b_o, gain, w_up, w_down, layer, final_gain=None):
    m, d = h.shape
    k = mix.shape[1]
    ff = w_up.shape[2]
    tm = MLP_ROWS if m % MLP_ROWS == 0 else _row_tile(m)
    layer_block = lambda shape: pl.BlockSpec((None,) + shape, lambda i: (layer, 0, 0),
                                             pipeline_mode=pl.Buffered(1))
    row = lambda i: (i, 0)
    const = lambda i: (0, 0)
    vec = pl.BlockSpec((1, d), const)
    resident = lambda shape: pl.BlockSpec(shape, const, pipeline_mode=pl.Buffered(1))
    args = [h, mix, w_o]
    specs = [pl.BlockSpec((tm, d), row), pl.BlockSpec((tm, k), row), resident((k, d))]
    if b_o is not None:
        args.append(b_o.reshape(1, d).astype(F32))
        specs.append(vec)
    args += [gain.reshape(1, d).astype(F32), w_up, w_down]
    specs += [vec, layer_block((d, ff)), layer_block((ff, d))]
    if final_gain is not None:
        args.append(final_gain.reshape(1, d).astype(F32))
        specs.append(vec)
    est = 4 * d * ff + 2 * k * d + 4 * tm * k + 4 * tm * d * 4 + 4 * tm * d * 4 + 3 * tm * MLP_FF_CHUNK * 4
    return pl.pallas_call(
        functools.partial(_mlp_kernel, has_bias=b_o is not None, has_final=final_gain is not None),
        out_shape=jax.ShapeDtypeStruct((m, d), F32),
        grid=(m // tm,),
        in_specs=specs,
        out_specs=pl.BlockSpec((tm, d), row),
        compiler_params=pltpu.CompilerParams(dimension_semantics=("parallel",),
                                             vmem_limit_bytes=_vmem_limit(est)),
        name="sqrelu_mlp",
    )(*args)


Q_SCALE = HEAD_DIM ** -0.5 * LOG2_E
STAGE_STRIDE = 4
LAST_STAGE_LAG = 1
ATTN_PIPE_DEPTH = 2


def _low_head_lanes(shape, interleaved):
    lane = lax.broadcasted_iota(jnp.int32, shape, 1)
    return (lane % HEAD_DIM) < HEAD_DIM // 2 if interleaved else lane < HEAD_DIM


def _stack_two_heads(t, interleaved=False):
    low = _low_head_lanes(t.shape, interleaved)
    zero = jnp.zeros_like(t)
    return jnp.concatenate([jnp.where(low, t, zero), jnp.where(low, zero, t)], axis=0)


def _init_band_consts(bias_ref, ones_ref, max_dist):
    r = lax.broadcasted_iota(jnp.int32, (ATTN_BLOCK, 2 * ATTN_BLOCK), 0)
    c = lax.broadcasted_iota(jnp.int32, (ATTN_BLOCK, 2 * ATTN_BLOCK), 1)
    dist = r + ATTN_BLOCK - c
    bias_ref[...] = jnp.where((dist >= 0) & (dist <= max_dist), 0.0, -jnp.inf)
    low = _low_head_lanes((2 * ATTN_BLOCK, LANES), False)
    ones_ref[0:2 * ATTN_BLOCK, :] = jnp.where(low, 1.0, 0.0).astype(BF16)
    ones_ref[2 * ATTN_BLOCK:, :] = jnp.where(low, 0.0, 1.0).astype(BF16)


def _head_pieces(t, interleaved):
    low = _low_head_lanes(t.shape, interleaved)
    return jnp.where(low, t, 0.0).astype(BF16), jnp.where(low, 0.0, t).astype(BF16)


def _piece_cache(load, interleaved):
    cache = {}

    def get(key):
        if key not in cache:
            cache[key] = _head_pieces(load(key), interleaved)
        return cache[key]
    return get


def _stack_blocks(prev, cur):
    return jnp.concatenate(list(cur) if prev is None else [prev[0], cur[0], prev[1], cur[1]], axis=0)


def _band_bias(bias_ref, first):
    return bias_ref[:, ATTN_BLOCK:] if first else bias_ref[...]


def _ones_stack(ones_ref, first):
    if not first:
        return ones_ref[...]
    half = ones_ref.shape[0] // 2
    return jnp.concatenate([ones_ref[half - ATTN_BLOCK:half, :], ones_ref[2 * half - ATTN_BLOCK:, :]], axis=0)


def _scores(q2, k_stack):
    return lax.dot_general(q2, k_stack, (((1,), (1,)), ((), ())), preferred_element_type=F32)


def _probs(s, bias, sinks):
    kw = s.shape[1] // 2
    s0, s1 = s[:, :kw] + bias, s[:, kw:] + bias
    m0 = jnp.max(s0, axis=-1, keepdims=True)
    m1 = jnp.max(s1, axis=-1, keepdims=True)
    if sinks is not None:
        m0, m1 = jnp.maximum(m0, sinks[0]), jnp.maximum(m1, sinks[1])
    p2 = jnp.concatenate([jnp.exp2(s0 - m0), jnp.exp2(s1 - m1)], axis=1).astype(BF16)
    return p2, jnp.where(_low_head_lanes((s.shape[0], LANES), False), m0, m1)


def _weighted(p2, m_l, v_stack, ones_stack, sinks):
    od = jnp.dot(p2, jnp.concatenate([v_stack, ones_stack], axis=1), preferred_element_type=F32)
    o, den = od[:, :LANES], od[:, LANES:]
    if sinks is not None:
        den = den + jnp.exp2(jnp.where(_low_head_lanes(o.shape, False), sinks[0], sinks[1]) - m_l)
    return o / den, m_l + jnp.log2(den)


def _run_pipelined(n, first_stage, mid_stage, last_stage):
    ahead = {i: first_stage(i) for i in range(min(ATTN_PIPE_DEPTH, n))}
    mids = {}
    for i in range(n + LAST_STAGE_LAG):
        if i < n:
            mids[i] = mid_stage(i, ahead.pop(i))
            if i + ATTN_PIPE_DEPTH < n:
                ahead[i + ATTN_PIPE_DEPTH] = first_stage(i + ATTN_PIPE_DEPTH)
        if i >= LAST_STAGE_LAG:
            last_stage(i - LAST_STAGE_LAG, mids.pop(i - LAST_STAGE_LAG))


def _swa_kernel(sink_ref, q_ref, k_ref, v_ref, o_ref, kb_ref, vb_ref, bias_ref, ones_ref):
    seq = q_ref.shape[0]
    p = pl.program_id(1)
    pairs_per_kv = A_Q_PER_KV // 2
    rows = 2 * ATTN_BLOCK

    @pl.when(p == 0)
    def _():
        _init_band_consts(bias_ref, ones_ref, A_WINDOW - 1)

    def spread_kv_head(kv):
        k_low, v_low = _low_head_lanes((rows, LANES), True), _low_head_lanes((rows, LANES), False)
        k_own, v_own = (k_low, v_low) if kv == 0 else (jnp.logical_not(k_low), jnp.logical_not(v_low))
        k_shift = HEAD_DIM // 2 if kv == 0 else LANES - HEAD_DIM // 2
        for c in range(seq // rows):
            sl = slice(c * rows, (c + 1) * rows)
            kt, vt = k_ref[sl, :].astype(F32), v_ref[sl, :].astype(F32)
            kt = jnp.where(k_own, kt, pltpu.roll(kt, k_shift, 1))
            vt = jnp.where(v_own, vt, pltpu.roll(vt, HEAD_DIM, 1))
            for dst, t, low in ((kb_ref, kt, k_low), (vb_ref, vt, v_low)):
                dst[0, sl, :] = jnp.where(low, t, 0.0).astype(BF16)
                dst[1, sl, :] = jnp.where(low, 0.0, t).astype(BF16)

    for kv in range(2):
        pl.when(p == kv * pairs_per_kv)(functools.partial(spread_kv_head, kv))

    sinks = (sink_ref[2 * p] * LOG2_E, sink_ref[2 * p + 1] * LOG2_E)
    blk = lambda i: slice(i * ATTN_BLOCK, (i + 1) * ATTN_BLOCK)

    def stack(ref, i):
        both = slice(max(i - 1, 0) * ATTN_BLOCK, (i + 1) * ATTN_BLOCK)
        return jnp.concatenate([ref[0, both, :], ref[1, both, :]], axis=0)

    def score_stage(i):
        return _scores(q_ref[blk(i), :], stack(kb_ref, i))

    def prob_stage(i, s):
        return _probs(s, _band_bias(bias_ref, i == 0), sinks)

    def out_stage(i, mid):
        o, _ = _weighted(*mid, stack(vb_ref, i), _ones_stack(ones_ref, i == 0), sinks)
        o_ref[blk(i), :] = o.astype(o_ref.dtype)

    _run_pipelined(seq // ATTN_BLOCK, score_stage, prob_stage, out_stage)


def _swa_attention(qkv, sinks, batch, seq):
    m = qkv.shape[0]
    q_dim = sinks.shape[0] * HEAD_DIM
    assert q_dim // HEAD_DIM // A_Q_PER_KV == 2 and seq % (2 * ATTN_BLOCK) == 0
    q3 = qkv.reshape(batch, seq, qkv.shape[1])
    kcol, vcol = q_dim // LANES, q_dim // LANES + 1
    slab = (None, seq, LANES)
    out = pl.pallas_call(
        _swa_kernel,
        out_shape=jax.ShapeDtypeStruct((batch, seq, q_dim), BF16),
        grid=(batch, q_dim // LANES),
        in_specs=[
            pl.BlockSpec(memory_space=pltpu.SMEM),
            pl.BlockSpec(slab, lambda b, p: (b, 0, p)),
            pl.BlockSpec(slab, lambda b, p: (b, 0, kcol)),
            pl.BlockSpec(slab, lambda b, p: (b, 0, vcol)),
        ],
        out_specs=pl.BlockSpec(slab, lambda b, p: (b, 0, p)),
        scratch_shapes=[pltpu.VMEM((2, seq, LANES), BF16), pltpu.VMEM((2, seq, LANES), BF16),
                        pltpu.VMEM((ATTN_BLOCK, 2 * ATTN_BLOCK), F32), pltpu.VMEM((4 * ATTN_BLOCK, LANES), BF16)],
        compiler_params=pltpu.CompilerParams(dimension_semantics=("parallel", "arbitrary"),
                                             vmem_limit_bytes=_vmem_limit(16 * seq * LANES * 2)),
        name="swa_sink_attention",
    )(sinks.astype(F32), q3, q3, q3)
    return out.reshape(m, q_dim)


def _dilated_kernel(*refs):
    n_groups = len(C_PATTERNS)
    in_refs, o_ref = refs[:3 * n_groups], refs[3 * n_groups]
    acc_o, acc_l, bias_ref, ones_ref, stage_ref = refs[3 * n_groups + 1:]
    seq = o_ref.shape[0]
    max_dist = C_PATTERNS[0][0] // C_PATTERNS[0][1]
    assert all(w // d == max_dist for w, d in C_PATTERNS)
    _init_band_consts(bias_ref, ones_ref, max_dist)

    order = sorted(range(n_groups), key=lambda g: -C_PATTERNS[g][1])
    blocks = []
    for g in order:
        dil = C_PATTERNS[g][1]
        for r in range(dil):
            for i in range(seq // dil // ATTN_BLOCK):
                blocks.append((g, dil, r, i))

    def rows(dil, r, i):
        start = i * ATTN_BLOCK * dil + r
        return pl.ds(start, ATTN_BLOCK, stride=dil) if dil > 1 else pl.ds(start, ATTN_BLOCK)

    staged = {g for g in range(n_groups) if C_PATTERNS[g][1] % (4 * STAGE_STRIDE) == 0}
    assert len(staged) <= 1
    for g in staged:
        for which in range(3):
            for a in range(STAGE_STRIDE):
                stage_ref[which, a] = in_refs[3 * g + which][pl.ds(a, seq // STAGE_STRIDE, stride=STAGE_STRIDE), :]

    def read(which, g, dil, r, i):
        if g not in staged:
            return in_refs[3 * g + which][rows(dil, r, i), :]
        inner = dil // STAGE_STRIDE
        return stage_ref[which, r % STAGE_STRIDE,
                         pl.ds(i * ATTN_BLOCK * inner + r // STAGE_STRIDE, ATTN_BLOCK, stride=inner), :]

    def loader(which):
        return lambda key: read(which, *key)

    k_pieces, v_pieces = _piece_cache(loader(1), True), _piece_cache(loader(2), False)

    def stack(pieces, g, dil, r, i):
        return _stack_blocks(pieces((g, dil, r, i - 1)) if i else None, pieces((g, dil, r, i)))

    def score_stage(n):
        g, dil, r, i = blocks[n]
        q2 = read(0, g, dil, r, i).astype(BF16)
        return _scores(q2, stack(k_pieces, g, dil, r, i))

    def prob_stage(n, s):
        return _probs(s, _band_bias(bias_ref, blocks[n][3] == 0), None)

    def out_stage(n, mid):
        g, dil, r, i = blocks[n]
        cur = rows(dil, r, i)
        o, lse = _weighted(*mid, stack(v_pieces, g, dil, r, i), _ones_stack(ones_ref, i == 0), None)
        if g != order[0]:
            o_old, l_old = acc_o[cur, :], acc_l[cur, :]
            l_max = jnp.maximum(l_old, lse)
            e_old, e_new = jnp.exp2(l_old - l_max), jnp.exp2(lse - l_max)
            inv = 1.0 / (e_old + e_new)
            o = o_old * (e_old * inv) + o * (e_new * inv)
            lse = l_max + jnp.log2(e_old + e_new)
        acc_o[cur, :] = o
        if g != order[-1]:
            acc_l[cur, :] = lse

    _run_pipelined(len(blocks), score_stage, prob_stage, out_stage)
    o_ref[...] = acc_o[...].astype(o_ref.dtype)


def _dilated_attention(qkv, batch, seq):
    m, width = qkv.shape
    n_groups = len(C_PATTERNS)
    gw = width // (3 * n_groups)
    assert gw % LANES == 0 and all(seq % (d * ATTN_BLOCK) == 0 for _, d in C_PATTERNS)
    pairs = gw // LANES
    q3 = qkv.reshape(batch, seq, width)
    slab = lambda c: pl.BlockSpec((None, seq, LANES), lambda b, p: (b, 0, c * pairs + p))
    specs = []
    for g in range(n_groups):
        specs += [slab(g), slab(n_groups + g), slab(2 * n_groups + g)]
    out = pl.pallas_call(
        _dilated_kernel,
        out_shape=jax.ShapeDtypeStruct((batch, seq, gw), BF16),
        grid=(batch, pairs),
        in_specs=specs,
        out_specs=pl.BlockSpec((None, seq, LANES), lambda b, p: (b, 0, p)),
        scratch_shapes=[pltpu.VMEM((seq, LANES), F32), pltpu.VMEM((seq, LANES), F32),
                        pltpu.VMEM((ATTN_BLOCK, 2 * ATTN_BLOCK), F32), pltpu.VMEM((4 * ATTN_BLOCK, LANES), BF16),
                        pltpu.VMEM((3, STAGE_STRIDE, seq // STAGE_STRIDE, LANES), F32)],
        compiler_params=pltpu.CompilerParams(
            dimension_semantics=("parallel", "parallel"),
            vmem_limit_bytes=_vmem_limit(2 * (3 * n_groups + 2) * seq * LANES * 4 + 8 * 1024 * 1024)),
        name="dilated_attention",
    )(*([q3] * (3 * n_groups)))
    return out.reshape(m, gw)


def _ssd_kernel(z_ref, x_ref, b_ref, c_ref, dt_ref, cw_ref, cb_ref, dtb_ref, alog_ref, dskip_ref, nw_ref, e_ref,
                o_ref, xpad_ref, state_ref, conv_ref):
    q = SSM_CHUNK
    d_inner = x_ref.shape[1]
    bc_dim = b_ref.shape[1]
    n_heads = d_inner // SSM_HEAD_DIM
    group_w = d_inner // SSM_N_GROUPS
    pad = SUBLANES
    n_slabs = xpad_ref.shape[0]
    x_slabs, bc_slabs = d_inner // LANES, bc_dim // LANES

    @pl.when(pl.program_id(1) == 0)
    def _():
        xpad_ref[:, 0:pad, :] = jnp.zeros((n_slabs, pad, LANES), F32)
        state_ref[...] = jnp.zeros_like(state_ref)

    def conv_slabs(s_lo, s_hi):
        for s in range(s_lo, s_hi):
            src, s0 = (x_ref, s) if s < x_slabs else (b_ref, s - x_slabs) if s < x_slabs + bc_slabs else (
                c_ref, s - x_slabs - bc_slabs)
            xpad_ref[s, pad:pad + q, :] = src[:, s0 * LANES:(s0 + 1) * LANES].astype(F32)
            lanes = slice(s * LANES, (s + 1) * LANES)
            for parity in range(2):
                acc = cb_ref[:, lanes]
                for k in range(SSM_CONV):
                    tap = xpad_ref[s, pl.ds(pad - k + parity, q // 2, stride=2), :]
                    acc = acc + cw_ref[SSM_CONV - 1 - k:SSM_CONV - k, lanes] * tap
                conv_ref[s, pl.ds(parity, q // 2, stride=2), :] = acc * _sigmoid(acc)

    head_lane = lax.broadcasted_iota(jnp.int32, (q, LANES), 1) < n_heads
    dtr = dt_ref[...] + dtb_ref[...]
    dt = jnp.maximum(dtr, 0.0) + jnp.log(1.0 + jnp.exp(-jnp.abs(dtr)))
    dt = jnp.where(head_lane, dt, 0.0)
    d_a = dt * (-jnp.exp(alog_ref[...]))
    rr = lax.broadcasted_iota(jnp.int32, (q, q), 0)
    cc = lax.broadcasted_iota(jnp.int32, (q, q), 1)
    tril = rr >= cc
    tri16 = jnp.where(tril, 1.0, 0.0).astype(BF16)
    d_a_parts = _split_bf16(d_a, 3)
    conv_slabs(0, n_slabs // 4)
    cs = None
    for part in d_a_parts:
        term = jnp.dot(tri16, part, preferred_element_type=F32)
        cs = term if cs is None else cs + term
    conv_slabs(n_slabs // 4, n_slabs // 2)
    cs2 = cs * LOG2_E
    cs2_t = cs2.T
    ecs = jnp.exp(cs)
    dec = jnp.exp(cs[q - 1:q, :] - cs)
    conv_slabs(n_slabs // 2, 3 * n_slabs // 4)
    e16 = e_ref[...]
    dt_e = _expand_heads(dt, e16, terms=1)
    ecs_e = _expand_heads(ecs, e16)
    dec_e = _expand_heads(dec, e16, terms=1)
    chunk_decay_e = ecs_e[q - 1:q, :]
    conv_slabs(3 * n_slabs // 4, n_slabs)
    xpad_ref[:, 0:pad, :] = xpad_ref[:, q:q + pad, :]

    heads_per_group = n_heads // SSM_N_GROUPS
    lanes_of = lambda g: slice(g * group_w, (g + 1) * group_w)
    slabs_per_group = group_w // LANES
    xs_of = lambda g: jnp.concatenate(
        [conv_ref[g * slabs_per_group + j] for j in range(slabs_per_group)], axis=1)

    def state_stage(g):
        gs = lanes_of(g)
        b_f = conv_ref[x_slabs + g]
        c16 = conv_ref[x_slabs + bc_slabs + g].astype(BF16)
        cb = lax.dot_general(c16, b_f.astype(BF16), (((1,), (1,)), ((), ())), preferred_element_type=F32)
        xdt = xs_of(g) * dt_e[:, gs]
        st_old = state_ref[:, gs]
        y_off = jnp.dot(c16, st_old.astype(BF16), preferred_element_type=F32) * ecs_e[:, gs]
        st_new = jnp.dot(b_f.T.astype(BF16), (xdt * dec_e[:, gs]).astype(BF16), preferred_element_type=F32)
        state_ref[:, gs] = st_old * chunk_decay_e[:, gs] + st_new
        return cb, y_off, xdt.astype(BF16)

    def diag_stage(g, carried):
        cb, y_off, xdt16 = carried
        y_parts = []
        for pr in range(heads_per_group // 2):
            h0 = g * heads_per_group + 2 * pr
            ms = []
            for hh in (h0, h0 + 1):
                diff = cs2[:, hh:hh + 1] - cs2_t[hh:hh + 1, :]
                ms.append(cb * jnp.exp2(jnp.where(tril, diff, -jnp.inf)))
            m2 = jnp.concatenate(ms, axis=1).astype(BF16)
            x_stack = _stack_two_heads(xdt16[:, pr * LANES:(pr + 1) * LANES])
            y_parts.append(jnp.dot(m2, x_stack, preferred_element_type=F32))
        return jnp.concatenate(y_parts, axis=1) + y_off

    def gate_stage(g, y):
        gs = lanes_of(g)
        y = y + dskip_ref[:, gs] * xs_of(g)
        zf = z_ref[:, gs].astype(F32)
        gated = y * (zf * _sigmoid(zf))
        o_ref[:, gs] = (_rms(gated) * nw_ref[:, gs]).astype(o_ref.dtype)

    _run_pipelined(SSM_N_GROUPS, state_stage, diag_stage, gate_stage)


def _ssd_mixer(zx, dt_raw, conv_w, conv_b, dt_bias, a_log, d_skip, norm_w, batch, seq):
    m = zx.shape[0]
    n_heads = a_log.shape[0]
    d_inner = n_heads * SSM_HEAD_DIM
    bc_dim = SSM_N_GROUPS * SSM_D_STATE
    conv_dim = d_inner + 2 * bc_dim
    assert zx.shape[1] == d_inner + conv_dim and seq % SSM_CHUNK == 0 and n_heads <= LANES
    assert d_inner % bc_dim == 0 and bc_dim % LANES == 0 and d_inner // SSM_N_GROUPS == 2 * LANES
    zx3 = zx.reshape(batch, seq, zx.shape[1])
    dt3 = dt_raw.reshape(batch, seq, LANES)
    pad_heads = lambda v: jnp.pad(v.astype(F32), (0, LANES - n_heads))[None, :]
    blk = lambda w: (None, SSM_CHUNK, w)
    const = lambda shape: pl.BlockSpec(shape, lambda b, c: (0,) * len(shape))
    r = d_inner // bc_dim
    out = pl.pallas_call(
        _ssd_kernel,
        out_shape=jax.ShapeDtypeStruct((batch, seq, d_inner), BF16),
        grid=(batch, seq // SSM_CHUNK),
        in_specs=[
            pl.BlockSpec(blk(d_inner), lambda b, c: (b, c, 0)),
            pl.BlockSpec(blk(d_inner), lambda b, c: (b, c, 1)),
            pl.BlockSpec(blk(bc_dim), lambda b, c: (b, c, 2 * r)),
            pl.BlockSpec(blk(bc_dim), lambda b, c: (b, c, 2 * r + 1)),
            pl.BlockSpec(blk(LANES), lambda b, c: (b, c, 0)),
            const((SSM_CONV, conv_dim)), const((1, conv_dim)), const((1, LANES)), const((1, LANES)),
            const((1, d_inner)), const((1, d_inner)), const((LANES, d_inner)),
        ],
        out_specs=pl.BlockSpec(blk(d_inner), lambda b, c: (b, c, 0)),
        scratch_shapes=[
            pltpu.VMEM((conv_dim // LANES, SSM_CHUNK + SUBLANES, LANES), F32),
            pltpu.VMEM((SSM_D_STATE, d_inner), F32),
            pltpu.VMEM((conv_dim // LANES, SSM_CHUNK, LANES), F32),
        ],
        compiler_params=pltpu.CompilerParams(dimension_semantics=("parallel", "arbitrary"),
                                             vmem_limit_bytes=_vmem_limit(40 * 1024 * 1024)),
        name="ssd_mixer",
    )(zx3, zx3, zx3, zx3, dt3, conv_w.astype(F32), conv_b.astype(F32)[None, :], pad_heads(dt_bias),
      pad_heads(a_log), jnp.repeat(d_skip.astype(F32), SSM_HEAD_DIM)[None, :], norm_w.astype(F32)[None, :],
      _head_expander(n_heads, SSM_HEAD_DIM))
    return out.reshape(m, d_inner)


def kernel(x, positions, norm_mix_w, norm_mlp_w, a_w_qkv, a_b_qkv, a_sinks, a_w_o, a_b_o, b_in_w, b_conv_w, b_conv_b, b_dt_bias, b_a_log, b_d, b_norm_w, b_out_w, c_w_qkv, c_w_o, mlp_w_up, mlp_w_down, final_norm_w):
    batch, seq, d_model = x.shape
    m = batch * seq
    depth = norm_mix_w.shape[0]
    h = x.reshape(m, d_model).astype(F32)
    rope = _rope_tables(positions.reshape(m, 1))
    w_up_all, w_down_all = mlp_w_up.astype(BF16), mlp_w_down.astype(BF16)

    for i in range(depth):
        kind, j = i % N_MIXERS, i // N_MIXERS
        if kind == 0:
            q_dim = a_sinks.shape[1] * HEAD_DIM
            kv_dim = (a_w_qkv.shape[2] - q_dim) // 2
            n_rope = q_dim + kv_dim
            qkv = _linear(h, _pair_interleave(a_w_qkv[j], n_rope).astype(BF16), gain=norm_mix_w[i],
                          bias=_pair_interleave(a_b_qkv[j], n_rope), rope=rope, rope_cols=n_rope, q_cols=q_dim,
                          name="swa_qkv")
            mix, w_o, b_o = _swa_attention(qkv, a_sinks[j], batch, seq), a_w_o[j], a_b_o[j]
        elif kind == 1:
            n_heads = b_a_log.shape[1]
            w_in = b_in_w[j]
            main = w_in.shape[1] - n_heads
            w_dt = jnp.pad(w_in[:, main:], ((0, 0), (0, LANES - n_heads))).astype(BF16)
            zx, dt_raw = _linear(h, w_in[:, :main].astype(BF16), gain=norm_mix_w[i], side_w=w_dt, tm=512,
                                 name="ssm_in")
            mix = _ssd_mixer(zx, dt_raw, b_conv_w[j], b_conv_b[j], b_dt_bias[j], b_a_log[j], b_d[j], b_norm_w[j],
                             batch, seq)
            w_o, b_o = b_out_w[j], None
        else:
            width = c_w_qkv.shape[2]
            n_rope = 2 * width // 3
            qkv = _linear(h, _pair_interleave(c_w_qkv[j], n_rope).astype(BF16), gain=norm_mix_w[i], rope=rope,
                          rope_cols=n_rope, q_cols=width // 3, out_dtype=F32, tm=256, name="dilated_qkv")
            mix, w_o, b_o = _dilated_attention(qkv, batch, seq), c_w_o[j], None
        final_gain = final_norm_w if i == depth - 1 else None
        h = _mix_out_mlp(h, mix, w_o.astype(BF16), b_o, norm_mlp_w[i], w_up_all, w_down_all, i, final_gain)
    return h.reshape(batch, seq, d_model).astype(x.dtype)
```
